```python
import math
import jax, jax.numpy as jnp
from jax import lax
import numpy as np

D_MODEL = 1024
BATCH = 16
SEQ = 256
DEPTH = 4
DEC_BATCH = 4
DEC_SEQ = 2048
PAST_LEN = 512

GRID_W = 64
N_MIXERS = 3
D_FF = -(-8 * D_MODEL // (3 * 256)) * 256
NORM_EPS = 1e-6
FN_GROUPS = 4
FN_GROUP_W = D_MODEL // FN_GROUPS
HY_PROJ = 3
HY_EMB = 33
HY_BANDS = (HY_EMB - 1) // 2
HY_FILT = 64
HY_TARGET = 1e-2
HY_FAST = 0.3
HY_SLOW = 1.5
RW_HEAD = 64
RW_HEADS = D_MODEL // RW_HEAD
RW_DECAY_LORA = 64
RW_AAA_LORA = 64
RW_GATE_LORA = 128
RW_LN_EPS = 64e-5

kernel_name = 'hybrid_fnet_hyena_rwkv7_diffusion_step'


def _n_layers_of_kind(kind):
    return len(range(kind, DEPTH, N_MIXERS))


def rms_norm(x, g):
    xf = x.astype(jnp.float32)
    y = xf * lax.rsqrt(jnp.mean(xf * xf, axis=-1, keepdims=True) + NORM_EPS)
    return (y * g.astype(jnp.float32)).astype(x.dtype)


def shift_prev(x):
    return jnp.pad(x[:, :-1], ((0, 0), (1, 0), (0, 0)))


def shift_next(x):
    return jnp.pad(x[:, 1:], ((0, 0), (0, 1), (0, 0)))


def swiglu(h, w1, w3, w2):
    return (jax.nn.silu(h @ w1) * (h @ w3)) @ w2


def fourier_mixer(h, w_out, b_out):
    b, l, d = h.shape
    hg = h.astype(jnp.float32).reshape(b, l, FN_GROUPS, FN_GROUP_W)
    f = jnp.fft.fft2(hg, axes=(1, 3), norm='ortho').real.reshape(b, l, d)
    return f.astype(h.dtype) @ w_out + b_out


def hyena_filter(l, w1, b1, freq, w2, b2, w3):
    f32 = jnp.float32
    t = jnp.linspace(0.0, 1.0, l, dtype=f32)[:, None]
    ang = 2.0 * math.pi * jnp.arange(l, dtype=f32)[:, None] / l
    bands = jnp.linspace(1e-4, HY_BANDS - 1, HY_BANDS, dtype=f32)[None]
    z = jnp.concatenate([t, jnp.cos(bands * ang), -jnp.sin(bands * ang)], axis=-1)
    hdn = jnp.sin(freq[0] * (z @ w1 + b1))
    hdn = jnp.sin(freq[1] * (hdn @ w2 + b2))
    k = hdn @ w3
    deltas = jnp.linspace(math.log(HY_TARGET) / HY_FAST, math.log(HY_TARGET) / HY_SLOW, D_MODEL, dtype=f32)
    win = jnp.exp(-t * jnp.abs(deltas)[None])
    k_fwd = k[:, :D_MODEL] * win
    k_bwd = k[:, D_MODEL:] * win
    zero = jnp.zeros((1, D_MODEL), f32)
    return jnp.concatenate([k_fwd[:1] + k_bwd[:1], k_fwd[1:], zero, k_bwd[1:][::-1]], axis=0)


def hyena_mixer(h, w_in, b_in, conv_w, conv_b, f_w1, f_b1, f_freq, f_w2, f_b2, f_w3, d_skip, w_out, b_out):
    b, l, d = h.shape
    f32 = jnp.float32
    u = h @ w_in + b_in
    u = shift_prev(u) * conv_w[0] + u * conv_w[1] + shift_next(u) * conv_w[2] + conv_b
    x0, x1, v = jnp.split(u, HY_PROJ, axis=-1)
    k_full = hyena_filter(l, f_w1.astype(f32), f_b1.astype(f32), f_freq.astype(f32),
                          f_w2.astype(f32), f_b2.astype(f32), f_w3.astype(f32))
    z = (v * x1).astype(f32)
    n = 2 * l
    y = jnp.fft.irfft(jnp.fft.rfft(z, n=n, axis=1) * jnp.fft.rfft(k_full, axis=0)[None], n=n, axis=1)[:, :l]
    y = y + z * d_skip.astype(f32)
    return (y.astype(h.dtype) * x0) @ w_out + b_out


def wkv7_scan(r, decay, k, v, kk, a, s0, reverse):
    def step(s, inp):
        r_t, w_t, k_t, v_t, kk_t, a_t = inp
        sa = jnp.einsum('bhvk,bhk->bhv', s, -kk_t)
        s = (s * w_t[:, :, None, :] + sa[..., None] * (kk_t * a_t)[:, :, None, :]
             + v_t[..., None] * k_t[:, :, None, :])
        return s, jnp.einsum('bhvk,bhk->bhv', s, r_t)
    xs = tuple(jnp.swapaxes(t, 0, 1) for t in (r, decay, k, v, kk, a))
    s_fin, o = lax.scan(step, s0, xs, reverse=reverse)
    return jnp.swapaxes(o, 0, 1), s_fin


def rwkv7_mixer(h, s0_fwd, s0_bwd, mix, wr, wk, wv, wo, w0, w1, w2, a0, a1, a2, g1, g2,
                k_k, k_a, r_k, lnx_w, lnx_b):
    b, l, d = h.shape
    f32 = jnp.float32

    def heads(t):
        return t.astype(f32).reshape(b, l, RW_HEADS, RW_HEAD)

    xx = 0.5 * (shift_prev(h) + shift_next(h)) - h
    xr, xw, xk, xv, xa, xg = (h + xx * mix[m] for m in range(6))
    r = heads(xr @ wr)
    k = heads(xk @ wk)
    v = heads(xv @ wv)
    g = jax.nn.sigmoid(xg @ g1) @ g2
    kk = k * k_k.astype(f32).reshape(RW_HEADS, RW_HEAD)
    kk = kk / jnp.maximum(jnp.sqrt(jnp.sum(kk * kk, axis=-1, keepdims=True)), 1e-12)
    ka = k_a.astype(f32).reshape(RW_HEADS, RW_HEAD)
    rk = r_k.astype(f32)
    outs, bonuses, finals = [], [], []
    for dd, (s0, rev) in enumerate(((s0_fwd, False), (s0_bwd, True))):
        w_log = -jax.nn.softplus(-(w0[dd] + jnp.tanh(xw @ w1[dd]) @ w2[dd]).astype(f32)) - 0.5
        decay = heads(jnp.exp(-jnp.exp(w_log)))
        a = heads(jax.nn.sigmoid(a0[dd] + (xa @ a1[dd]) @ a2[dd]))
        kd = k * (1.0 + (a - 1.0) * ka)
        o_d, s_fin = wkv7_scan(r, decay, kd, v, kk, a, s0.astype(f32), rev)
        outs.append(o_d)
        bonuses.append(jnp.sum(r * kd * rk, axis=-1, keepdims=True) * v)
        finals.append(s_fin)
    o = outs[0] + outs[1]
    mu = jnp.mean(o, axis=-1, keepdims=True)
    var = jnp.mean(jnp.square(o - mu), axis=-1, keepdims=True)
    o = ((o - mu) * lax.rsqrt(var + RW_LN_EPS)).reshape(b, l, d) * lnx_w.astype(f32) + lnx_b.astype(f32)
    o = o + (bonuses[0] + bonuses[1]).reshape(b, l, d)
    y = (o * g.astype(f32)).astype(h.dtype) @ wo
    return y, finals[0], finals[1]


def run_trunk(x, cond, s0_fwd, s0_bwd, p):
    new_fwd, new_bwd = [], []
    for i in range(DEPTH):
        kind, j = i % N_MIXERS, i // N_MIXERS
        mod = jax.nn.silu(cond) @ p['ada_w'][i] + p['ada_b'][i]
        sh1, sc1, gt1, sh2, sc2, gt2 = jnp.split(mod[:, None, :], 6, axis=-1)
        ng = p['norm_g'][i]
        h = rms_norm(x, ng[0]) * (1 + sc1) + sh1
        if kind == 0:
            y = fourier_mixer(h, p['fn_w_out'][j], p['fn_b_out'][j])
        elif kind == 1:
            y = hyena_mixer(h, p['hy_w_in'][j], p['hy_b_in'][j], p['hy_conv_w'][j], p['hy_conv_b'][j],
                            p['hy_f_w1'][j], p['hy_f_b1'][j], p['hy_f_freq'][j], p['hy_f_w2'][j],
                            p['hy_f_b2'][j], p['hy_f_w3'][j], p['hy_d'][j], p['hy_w_out'][j], p['hy_b_out'][j])
        else:
            y, sf, sb = rwkv7_mixer(h, s0_fwd[:, j], s0_bwd[:, j], p['rw_mix'][j], p['rw_wr'][j],
                                    p['rw_wk'][j], p['rw_wv'][j], p['rw_wo'][j], p['rw_w0'][j],
                                    p['rw_w1'][j], p['rw_w2'][j], p['rw_a0'][j], p['rw_a1'][j],
                                    p['rw_a2'][j], p['rw_g1'][j], p['rw_g2'][j], p['rw_kk'][j],
                                    p['rw_ka'][j], p['rw_rk'][j], p['rw_lnx_w'][j], p['rw_lnx_b'][j])
            new_fwd.append(sf)
            new_bwd.append(sb)
        x = x + gt1 * rms_norm(y.astype(x.dtype), ng[1])
        h = rms_norm(x, ng[2]) * (1 + sc2) + sh2
        y = swiglu(h, p['ffn_w1'][i], p['ffn_w3'][i], p['ffn_w2'][i])
        x = x + gt2 * rms_norm(y.astype(x.dtype), ng[3])
    return x, jnp.stack(new_fwd, axis=1), jnp.stack(new_bwd, axis=1)


def setup_inputs(seed: int = 0) -> dict:
    key = jax.random.key(seed)
    ks = iter(jax.random.split(key, 64))
    f32 = jnp.float32

    def nrm(shape, s):
        return jax.random.normal(next(ks), shape, f32) * s

    def uni(shape, lo, hi):
        return jax.random.uniform(next(ks), shape, f32, lo, hi)

    D, F, H, N = D_MODEL, D_FF, RW_HEADS, RW_HEAD
    nA, nB, nC = (_n_layers_of_kind(kd) for kd in range(N_MIXERS))
    return {
        'x_prompt': nrm((BATCH, SEQ, D), 1.0),
        'x_sample': nrm((DEC_BATCH, DEC_SEQ, D), 1.0),
        'state_wkv_fwd': nrm((DEC_BATCH, nC, H, N, N), 0.5),
        'state_wkv_bwd': nrm((DEC_BATCH, nC, H, N, N), 0.5),
        'c': nrm((DEC_BATCH, D), 1.0),
        'c_ctx': nrm((D,), 1.0),
        'ada_w': nrm((DEPTH, D, 6 * D), 0.5 * D ** -0.5),
        'ada_b': nrm((DEPTH, 6 * D), 0.02),
        'norm_g': 1.0 + nrm((DEPTH, 4, D), 0.02),
        'ffn_w1': nrm((DEPTH, D, F), D ** -0.5),
        'ffn_w3': nrm((DEPTH, D, F), D ** -0.5),
        'ffn_w2': nrm((DEPTH, F, D), F ** -0.5),
        'fn_w_out': nrm((nA, D, D), D ** -0.5),
        'fn_b_out': nrm((nA, D), 0.02),
        'hy_w_in': nrm((nB, D, HY_PROJ * D), D ** -0.5),
        'hy_b_in': nrm((nB, HY_PROJ * D), 0.02),
        'hy_conv_w': nrm((nB, 3, HY_PROJ * D), 3 ** -0.5),
        'hy_conv_b': nrm((nB, HY_PROJ * D), 0.02),
        'hy_f_w1': nrm((nB, HY_EMB, HY_FILT), HY_EMB ** -0.5),
        'hy_f_b1': nrm((nB, HY_FILT), 0.02),
        'hy_f_freq': 1.0 + nrm((nB, 2, HY_FILT), 0.1),
        'hy_f_w2': nrm((nB, HY_FILT, HY_FILT), HY_FILT ** -0.5),
        'hy_f_b2': nrm((nB, HY_FILT), 0.02),
        'hy_f_w3': nrm((nB, HY_FILT, 2 * D), HY_FILT ** -0.5),
        'hy_d': nrm((nB, D), 1.0),
        'hy_w_out': nrm((nB, D, D), D ** -0.5),
        'hy_b_out': nrm((nB, D), 0.02),
        'rw_mix': uni((nC, 6, D), 0.0, 1.0),
        'rw_wr': nrm((nC, D, D), D ** -0.5),
        'rw_wk': nrm((nC, D, D), D ** -0.5),
        'rw_wv': nrm((nC, D, D), D ** -0.5),
        'rw_wo': nrm((nC, D, D), D ** -0.5),
        'rw_w0': uni((nC, 2, D), -6.0, -1.0),
        'rw_w1': nrm((nC, 2, D, RW_DECAY_LORA), 0.1 * D ** -0.5),
        'rw_w2': nrm((nC, 2, RW_DECAY_LORA, D), 0.1 * RW_DECAY_LORA ** -0.5),
        'rw_a0': nrm((nC, 2, D), 0.1),
        'rw_a1': nrm((nC, 2, D, RW_AAA_LORA), 0.1 * D ** -0.5),
        'rw_a2': nrm((nC, 2, RW_AAA_LORA, D), 0.1 * RW_AAA_LORA ** -0.5),
        'rw_g1': nrm((nC, D, RW_GATE_LORA), D ** -0.5),
        'rw_g2': nrm((nC, RW_GATE_LORA, D), RW_GATE_LORA ** -0.5),
        'rw_kk': 0.85 + nrm((nC, D), 0.02),
        'rw_ka': 1.0 + nrm((nC, D), 0.02),
        'rw_rk': nrm((nC, H, N), 0.1),
        'rw_lnx_w': 1.0 + nrm((nC, D), 0.02),
        'rw_lnx_b': nrm((nC, D), 0.02),
    }


def reference(x_prompt, x_sample, state_wkv_fwd, state_wkv_bwd, c, c_ctx,
              ada_w, ada_b, norm_g, ffn_w1, ffn_w3, ffn_w2, fn_w_out, fn_b_out,
              hy_w_in, hy_b_in, hy_conv_w, hy_conv_b, hy_f_w1, hy_f_b1, hy_f_freq,
              hy_f_w2, hy_f_b2, hy_f_w3, hy_d, hy_w_out, hy_b_out,
              rw_mix, rw_wr, rw_wk, rw_wv, rw_wo, rw_w0, rw_w1, rw_w2, rw_a0, rw_a1, rw_a2,
              rw_g1, rw_g2, rw_kk, rw_ka, rw_rk, rw_lnx_w, rw_lnx_b):
    p = {
        'ada_w': ada_w, 'ada_b': ada_b, 'norm_g': norm_g,
        'ffn_w1': ffn_w1, 'ffn_w3': ffn_w3, 'ffn_w2': ffn_w2,
        'fn_w_out': fn_w_out, 'fn_b_out': fn_b_out,
        'hy_w_in': hy_w_in, 'hy_b_in': hy_b_in, 'hy_conv_w': hy_conv_w, 'hy_conv_b': hy_conv_b,
        'hy_f_w1': hy_f_w1, 'hy_f_b1': hy_f_b1, 'hy_f_freq': hy_f_freq, 'hy_f_w2': hy_f_w2,
        'hy_f_b2': hy_f_b2, 'hy_f_w3': hy_f_w3, 'hy_d': hy_d, 'hy_w_out': hy_w_out, 'hy_b_out': hy_b_out,
        'rw_mix': rw_mix, 'rw_wr': rw_wr, 'rw_wk': rw_wk, 'rw_wv': rw_wv, 'rw_wo': rw_wo,
        'rw_w0': rw_w0, 'rw_w1': rw_w1, 'rw_w2': rw_w2, 'rw_a0': rw_a0, 'rw_a1': rw_a1, 'rw_a2': rw_a2,
        'rw_g1': rw_g1, 'rw_g2': rw_g2, 'rw_kk': rw_kk, 'rw_ka': rw_ka, 'rw_rk': rw_rk,
        'rw_lnx_w': rw_lnx_w, 'rw_lnx_b': rw_lnx_b,
    }
    n_rwkv = rw_w0.shape[0]
    zeros = jnp.zeros((x_prompt.shape[0], n_rwkv, RW_HEADS, RW_HEAD, RW_HEAD), jnp.float32)
    y_prompt, new_state_wkv_fwd, new_state_wkv_bwd = run_trunk(x_prompt, c_ctx[None, :], zeros, zeros, p)
    y_sample, _, _ = run_trunk(x_sample, c, state_wkv_fwd, state_wkv_bwd, p)
    return (y_prompt, y_sample, new_state_wkv_fwd, new_state_wkv_bwd)
```

```python
import functools
import math
from typing import NamedTuple

import numpy as np
import jax
import jax.numpy as jnp
from jax import lax
from jax.experimental import pallas as pl
from jax.experimental.pallas import tpu as pltpu

f32 = jnp.float32
bf16 = jnp.bfloat16

D_MODEL = 1024
DEPTH = 4
N_MIXERS = 3
D_FF = 2816
NORM_EPS = 1e-6
FN_GROUP_W = 256
HY_EMB = 33
HY_EMB_PAD = 128
HY_BANDS = 16
HY_FILT = 64
HY_TARGET = 1e-2
HY_FAST = 0.3
HY_SLOW = 1.5
RW_HEAD = 64
RW_HEADS = 16
RW_LN_EPS = 64e-5
SCAN_CHUNK = 64
MOD_ROWS = 8
VMEM_LIMIT = 56 * 1024 * 1024


class Trunk(NamedTuple):
    batch: int
    seq: int
    mod_base: int
    per_batch_mod: bool


def _params(*sem):
    return pltpu.CompilerParams(dimension_semantics=sem, vmem_limit_bytes=VMEM_LIMIT)


def _resident(shape):
    nd = len(shape)
    return pl.BlockSpec(shape, lambda *_: (0,) * nd, pipeline_mode=pl.Buffered(1))


def _dot(a, b):
    return jnp.dot(a.astype(bf16), b.astype(bf16), preferred_element_type=f32)


def _split(x):
    hi = x.astype(bf16)
    lo = (x - hi.astype(f32)).astype(bf16)
    return hi, lo


def _dot3(a, b):
    ah, al = _split(a)
    bh, bl = _split(b)
    d = functools.partial(jnp.dot, preferred_element_type=f32)
    return d(ah, bh) + (d(ah, bl) + d(al, bh))


def _dot_exact_rhs(a, b_exact):
    a0 = a.astype(bf16)
    r1 = a - a0.astype(f32)
    a1 = r1.astype(bf16)
    a2 = (r1 - a1.astype(f32)).astype(bf16)
    d = functools.partial(jnp.dot, preferred_element_type=f32)
    return d(a0, b_exact) + (d(a1, b_exact) + d(a2, b_exact))


def _sigmoid(x):
    return 1.0 / (1.0 + jnp.exp(-x))


def _norm_mod(x, g, shift, scale):
    ms = jnp.mean(x * x, axis=-1, keepdims=True)
    return (x * lax.rsqrt(ms + NORM_EPS) * g) * (1.0 + scale) + shift


def _residual(x, y, g, gate):
    ms = jnp.mean(y * y, axis=-1, keepdims=True)
    return x + gate * (y * lax.rsqrt(ms + NORM_EPS) * g)


def _shifted(h, prev_row, next_row):
    rows = h.shape[0]
    ridx = lax.broadcasted_iota(jnp.int32, h.shape, 0)
    h_prev = jnp.where(ridx == 0, prev_row, pltpu.roll(h, 1, 0))
    h_next = jnp.where(ridx == rows - 1, next_row, pltpu.roll(h, rows - 1, 0))
    return h_prev, h_next


def _tile_rows(trunk, cap):
    return min(trunk.seq, cap)


def _x_spec(tl):
    return pl.BlockSpec((None, tl, D_MODEL), lambda b, j: (b, j, 0))


def _halo_specs(trunk, tl):
    g = tl // 8
    last = trunk.seq // 8 - 1
    prev = pl.BlockSpec((None, None, 8, D_MODEL), lambda b, j: (b, jnp.maximum(j * g - 1, 0), 0, 0))
    nxt = pl.BlockSpec((None, None, 8, D_MODEL), lambda b, j: (b, jnp.minimum((j + 1) * g, last), 0, 0))
    return prev, nxt


def _mod_spec(trunk, layer):
    if trunk.per_batch_mod:
        return pl.BlockSpec((None, None, 6, D_MODEL), lambda b, *_: (layer, trunk.mod_base + b, 0, 0))
    return pl.BlockSpec((None, None, 6, D_MODEL), lambda b, *_: (layer, trunk.mod_base, 0, 0))


def _ng_spec(layer):
    return pl.BlockSpec((None, 4, D_MODEL), lambda *_: (layer, 0, 0))


def _mod_kernel(c_ref, w_ref, b_ref, o_ref):
    c = c_ref[...]
    o_ref[...] = _dot3(c * _sigmoid(c), w_ref[...]) + b_ref[...]


def _adaln(cond, ada_w, ada_b):
    tn = 1536
    out = pl.pallas_call(
        _mod_kernel,
        grid=(DEPTH, 6 * D_MODEL // tn),
        in_specs=[pl.BlockSpec((MOD_ROWS, D_MODEL), lambda l, j: (0, 0)),
                  pl.BlockSpec((None, D_MODEL, tn), lambda l, j: (l, 0, j)),
                  pl.BlockSpec((None, 1, tn), lambda l, j: (l, 0, j))],
        out_specs=pl.BlockSpec((None, MOD_ROWS, tn), lambda l, j: (l, 0, j)),
        out_shape=jax.ShapeDtypeStruct((DEPTH, MOD_ROWS, 6 * D_MODEL), f32),
        compiler_params=_params("arbitrary", "arbitrary"),
    )(cond, ada_w, ada_b.reshape(DEPTH, 1, 6 * D_MODEL))
    return out.reshape(DEPTH, MOD_ROWS, 6, D_MODEL)


def _ffn_kernel(x_ref, mod_ref, ng_ref, w1_ref, w3_ref, w2_ref, o_ref):
    x = x_ref[...]
    h = _norm_mod(x, ng_ref[2:3, :], mod_ref[3:4, :], mod_ref[4:5, :]).astype(bf16)
    a = jnp.dot(h, w1_ref[...], preferred_element_type=f32)
    b = jnp.dot(h, w3_ref[...], preferred_element_type=f32)
    gated = (a * _sigmoid(a) * b).astype(bf16)
    y = jnp.dot(gated, w2_ref[...], preferred_element_type=f32)
    o_ref[...] = _residual(x, y, ng_ref[3:4, :], mod_ref[5:6, :])


def _ffn(x, trunk, layer, mod, norm_g, w1, w3, w2):
    tl = _tile_rows(trunk, 512)
    return pl.pallas_call(
        _ffn_kernel,
        grid=(trunk.batch, trunk.seq // tl),
        in_specs=[_x_spec(tl), _mod_spec(trunk, layer), _ng_spec(layer),
                  _resident((D_MODEL, D_FF)), _resident((D_MODEL, D_FF)), _resident((D_FF, D_MODEL))],
        out_specs=_x_spec(tl),
        out_shape=jax.ShapeDtypeStruct(x.shape, f32),
        compiler_params=_params("parallel", "parallel"),
    )(x, mod, norm_g, w1, w3, w2)


@functools.lru_cache(maxsize=None)
def _channel_dft():
    w = FN_GROUP_W
    idx = np.arange(w)
    ang = 2.0 * np.pi * ((idx[:, None] * idx[None, :]) % w) / w
    out = np.zeros((D_MODEL, 2 * D_MODEL), np.float64)
    for g in range(D_MODEL // w):
        out[g * w:(g + 1) * w, g * w:(g + 1) * w] = np.cos(ang) / math.sqrt(w)
        out[g * w:(g + 1) * w, D_MODEL + g * w:D_MODEL + (g + 1) * w] = np.sin(ang) / math.sqrt(w)
    return out.astype(np.float32)


@functools.lru_cache(maxsize=None)
def _time_dft(n_rows, period, scale):
    idx = np.arange(n_rows)
    ang = 2.0 * np.pi * ((idx[:, None] * idx[None, :]) % period) / period
    return (np.cos(ang) * scale).astype(np.float32), (np.sin(ang) * scale).astype(np.float32)


def _mxu_table(table):
    return jnp.asarray(table).astype(bf16)


def _fn_a_kernel(x_ref, mod_ref, ng_ref, w_ref, o_ref):
    h = _norm_mod(x_ref[...], ng_ref[0:1, :], mod_ref[0:1, :], mod_ref[1:2, :]).astype(bf16)
    o_ref[...] = jnp.dot(h, w_ref[...], preferred_element_type=f32).astype(bf16)


def _fn_b_kernel(c_ref, s_ref, ab_ref, x_ref, mod_ref, ng_ref, wo_ref, bo_ref, o_ref, acc_ref):
    k = pl.program_id(2)

    @pl.when(k == 0)
    def _():
        acc_ref[...] = jnp.zeros_like(acc_ref)

    a = ab_ref[:, :D_MODEL]
    b = ab_ref[:, D_MODEL:]
    acc_ref[...] += (jnp.dot(c_ref[...], a, preferred_element_type=f32)
                     - jnp.dot(s_ref[...], b, preferred_element_type=f32))

    @pl.when(k == pl.num_programs(2) - 1)
    def _():
        y = jnp.dot(acc_ref[...].astype(bf16), wo_ref[...], preferred_element_type=f32) + bo_ref[...]
        o_ref[...] = _residual(x_ref[...], y, ng_ref[1:2, :], mod_ref[2:3, :])


def _fourier_layer(x, trunk, layer, mod, norm_g, w_out, b_out):
    tl = _tile_rows(trunk, 512)
    ab = pl.pallas_call(
        _fn_a_kernel,
        grid=(trunk.batch, trunk.seq // tl),
        in_specs=[_x_spec(tl), _mod_spec(trunk, layer), _ng_spec(layer), _resident((D_MODEL, 2 * D_MODEL))],
        out_specs=pl.BlockSpec((None, tl, 2 * D_MODEL), lambda b, j: (b, j, 0)),
        out_shape=jax.ShapeDtypeStruct((trunk.batch, trunk.seq, 2 * D_MODEL), bf16),
        compiler_params=_params("parallel", "parallel"),
    )(x, mod, norm_g, _mxu_table(_channel_dft()))

    cos_t, sin_t = _time_dft(trunk.seq, trunk.seq, 1.0 / math.sqrt(trunk.seq))
    tm = _tile_rows(trunk, 512)
    tk = _tile_rows(trunk, 512)
    return pl.pallas_call(
        _fn_b_kernel,
        grid=(trunk.batch, trunk.seq // tm, trunk.seq // tk),
        in_specs=[pl.BlockSpec((tm, tk), lambda b, i, k: (i, k)),
                  pl.BlockSpec((tm, tk), lambda b, i, k: (i, k)),
                  pl.BlockSpec((None, tk, 2 * D_MODEL), lambda b, i, k: (b, k, 0)),
                  pl.BlockSpec((None, tm, D_MODEL), lambda b, i, k: (b, i, 0)),
                  _mod_spec(trunk, layer), _ng_spec(layer),
                  _resident((D_MODEL, D_MODEL)), _resident((1, D_MODEL))],
        out_specs=pl.BlockSpec((None, tm, D_MODEL), lambda b, i, k: (b, i, 0)),
        out_shape=jax.ShapeDtypeStruct(x.shape, f32),
        scratch_shapes=[pltpu.VMEM((tm, D_MODEL), f32)],
        compiler_params=_params("parallel", "parallel", "arbitrary"),
    )(_mxu_table(cos_t), _mxu_table(sin_t), ab, x, mod, norm_g, w_out, b_out)


@functools.lru_cache(maxsize=None)
def _hyena_features(seq):
    t = np.linspace(0.0, 1.0, seq)[:, None]
    ang = 2.0 * np.pi * np.arange(seq)[:, None] / seq
    bands = np.linspace(1e-4, HY_BANDS - 1, HY_BANDS)[None]
    z = np.concatenate([t, np.cos(bands * ang), -np.sin(bands * ang)], axis=-1)
    out = np.zeros((seq, HY_EMB_PAD), np.float32)
    out[:, :HY_EMB] = z
    return out


@functools.lru_cache(maxsize=None)
def _hyena_decay_rates():
    d = np.linspace(math.log(HY_TARGET) / HY_FAST, math.log(HY_TARGET) / HY_SLOW, D_MODEL)
    return np.abs(d)[None].astype(np.float32)


def _hy_in_kernel(x_ref, xp_ref, xn_ref, mod_ref, ng_ref, w_ref, b_ref, cw_ref, cb_ref, x0_ref, z_ref):
    j = pl.program_id(1)
    g, sh, sc = ng_ref[0:1, :], mod_ref[0:1, :], mod_ref[1:2, :]
    tl = x_ref.shape[0]
    rows = jnp.concatenate([x_ref[...], xp_ref[...], xn_ref[...]], axis=0)
    u_all = jnp.dot(_norm_mod(rows, g, sh, sc).astype(bf16), w_ref[...], preferred_element_type=f32) + b_ref[...]
    u = u_all[:tl, :]
    prev_row = jnp.where(j == 0, 0.0, u_all[tl + 7:tl + 8, :])
    next_row = jnp.where(j == pl.num_programs(1) - 1, 0.0, u_all[tl + 8:tl + 9, :])
    u_prev, u_next = _shifted(u, prev_row, next_row)
    uc = u_prev * cw_ref[0:1, :] + u * cw_ref[1:2, :] + u_next * cw_ref[2:3, :] + cb_ref[...]
    x0_ref[...] = uc[:, :D_MODEL]
    z_ref[...] = uc[:, 2 * D_MODEL:] * uc[:, D_MODEL:2 * D_MODEL]


def _hy_filter_kernel(feat_ref, w1_ref, b1_ref, fq_ref, w2_ref, b2_ref, w3f_ref, w3b_ref, rate_ref,
                      cos_ref, sin_ref, o_ref, *, seq):
    hdn = jnp.sin(fq_ref[0:1, :] * (_dot3(feat_ref[...], w1_ref[...]) + b1_ref[...]))
    hdn = jnp.sin(fq_ref[1:2, :] * (_dot3(hdn, w2_ref[...]) + b2_ref[...]))
    rows = lax.broadcasted_iota(jnp.int32, (seq, 1), 0)
    t = rows.astype(f32) * (1.0 / (seq - 1))
    win = jnp.exp(-t * rate_ref[...])
    k_fwd = _dot3(hdn, w3f_ref[...]) * win
    k_bwd = _dot3(hdn, w3b_ref[...]) * win
    both = k_fwd + k_bwd
    diff = jnp.where(rows == 0, both, k_fwd - k_bwd)
    k_re = jnp.dot(cos_ref[...], both.astype(bf16), preferred_element_type=f32)
    k_im = jnp.dot(sin_ref[...], diff.astype(bf16), preferred_element_type=f32)
    sign = (1 - 2 * (rows & 1)).astype(f32)
    k_nyq = jnp.sum(both * sign, axis=0, keepdims=True)
    wgt = jnp.where(rows == 0, 0.5 / seq, 1.0 / seq)
    o_ref[0] = k_re * wgt
    o_ref[1] = jnp.where(rows == 0, k_nyq, -k_im) * wgt


def _hy_fwd_kernel(cos_ref, sin_ref, z_ref, ks_ref, o_ref, re_ref, im_ref, *, tk):
    i = pl.program_id(1)
    k = pl.program_id(2)

    @pl.when(k == 0)
    def _():
        re_ref[...] = jnp.zeros_like(re_ref)
        im_ref[...] = jnp.zeros_like(im_ref)

    z = z_ref[...]
    zb = z.astype(bf16)
    re_ref[...] += jnp.dot(cos_ref[...], zb, preferred_element_type=f32)
    im_ref[...] -= jnp.dot(sin_ref[...], zb, preferred_element_type=f32)

    @pl.when(i == 0)
    def _():
        t = k * tk + lax.broadcasted_iota(jnp.int32, (tk, 1), 0)
        sign = (1 - 2 * (t & 1)).astype(f32)
        im_ref[0:1, :] += jnp.sum(z * sign, axis=0, keepdims=True)

    @pl.when(k == pl.num_programs(2) - 1)
    def _():
        zr, zi = re_ref[...], im_ref[...]
        kr, ki = ks_ref[0], ks_ref[1]
        f = i * zr.shape[0] + lax.broadcasted_iota(jnp.int32, (zr.shape[0], 1), 0)
        packed = f == 0
        o_ref[0] = jnp.where(packed, zr * kr, zr * kr - zi * ki).astype(bf16)
        o_ref[1] = jnp.where(packed, zi * ki, zr * ki + zi * kr).astype(bf16)


def _hy_inv_kernel(cos_ref, sin_ref, ys_ref, z_ref, x0_ref, x_ref, mod_ref, ng_ref, d_ref, wo_ref, bo_ref,
                   o_ref, acc_ref):
    i = pl.program_id(1)
    k = pl.program_id(2)
    tm = acc_ref.shape[0]

    @pl.when(k == 0)
    def _():
        acc_ref[...] = jnp.zeros_like(acc_ref)

    acc_ref[...] += (jnp.dot(cos_ref[...], ys_ref[0], preferred_element_type=f32)
                     - jnp.dot(sin_ref[...], ys_ref[1], preferred_element_type=f32))

    @pl.when(k == 0)
    def _():
        t = i * tm + lax.broadcasted_iota(jnp.int32, (tm, 1), 0)
        sign = (1 - 2 * (t & 1)).astype(f32)
        acc_ref[...] += sign * ys_ref[1, 0:1, :].astype(f32)

    @pl.when(k == pl.num_programs(2) - 1)
    def _():
        z = z_ref[...]
        y = acc_ref[...] + z * d_ref[...]
        out = jnp.dot((y * x0_ref[...]).astype(bf16), wo_ref[...], preferred_element_type=f32) + bo_ref[...]
        o_ref[...] = _residual(x_ref[...], out, ng_ref[1:2, :], mod_ref[2:3, :])


def _hyena_layer(x, trunk, layer, mod, norm_g, p):
    nb, seq = trunk.batch, trunk.seq
    tl = _tile_rows(trunk, 256)
    x8 = x.reshape(nb, seq // 8, 8, D_MODEL)
    prev_spec, next_spec = _halo_specs(trunk, tl)
    x0, z = pl.pallas_call(
        _hy_in_kernel,
        grid=(nb, seq // tl),
        in_specs=[_x_spec(tl), prev_spec, next_spec, _mod_spec(trunk, layer), _ng_spec(layer),
                  _resident((D_MODEL, 3 * D_MODEL)), _resident((1, 3 * D_MODEL)),
                  _resident((3, 3 * D_MODEL)), _resident((1, 3 * D_MODEL))],
        out_specs=[_x_spec(tl), _x_spec(tl)],
        out_shape=[jax.ShapeDtypeStruct(x.shape, f32)] * 2,
        compiler_params=_params("parallel", "parallel"),
    )(x, x8, x8, mod, norm_g, p["w_in"], p["b_in"], p["conv_w"], p["conv_b"])

    n = 2 * seq
    cos_t, sin_t = _time_dft(seq, n, 1.0)
    cos_t, sin_t = _mxu_table(cos_t), _mxu_table(sin_t)
    tn = 256
    nblk = D_MODEL // tn
    ks = pl.pallas_call(
        functools.partial(_hy_filter_kernel, seq=seq),
        grid=(nblk,),
        in_specs=[_resident((seq, HY_EMB_PAD)), _resident((HY_EMB_PAD, HY_FILT)), _resident((1, HY_FILT)),
                  _resident((2, HY_FILT)), _resident((HY_FILT, HY_FILT)), _resident((1, HY_FILT)),
                  pl.BlockSpec((HY_FILT, tn), lambda j: (0, j)),
                  pl.BlockSpec((HY_FILT, tn), lambda j: (0, nblk + j)),
                  pl.BlockSpec((1, tn), lambda j: (0, j)),
                  _resident((seq, seq)), _resident((seq, seq))],
        out_specs=pl.BlockSpec((2, seq, tn), lambda j: (0, 0, j)),
        out_shape=jax.ShapeDtypeStruct((2, seq, D_MODEL), f32),
        compiler_params=_params("parallel"),
    )(jnp.asarray(_hyena_features(seq)), p["f_w1"], p["f_b1"], p["f_freq"], p["f_w2"], p["f_b2"],
      p["f_w3"], p["f_w3"], jnp.asarray(_hyena_decay_rates()), cos_t, sin_t)

    tf = _tile_rows(trunk, 512)
    tk = _tile_rows(trunk, 512)
    dft_spec = pl.BlockSpec((tf, tk), lambda b, i, k: (i, k))
    ys = pl.pallas_call(
        functools.partial(_hy_fwd_kernel, tk=tk),
        grid=(nb, seq // tf, seq // tk),
        in_specs=[dft_spec, dft_spec,
                  pl.BlockSpec((None, tk, D_MODEL), lambda b, i, k: (b, k, 0)),
                  pl.BlockSpec((2, tf, D_MODEL), lambda b, i, k: (0, i, 0))],
        out_specs=pl.BlockSpec((None, 2, tf, D_MODEL), lambda b, i, k: (b, 0, i, 0)),
        out_shape=jax.ShapeDtypeStruct((nb, 2, seq, D_MODEL), bf16),
        scratch_shapes=[pltpu.VMEM((tf, D_MODEL), f32), pltpu.VMEM((tf, D_MODEL), f32)],
        compiler_params=_params("parallel", "parallel", "arbitrary"),
    )(cos_t, sin_t, z, ks)

    row_spec = pl.BlockSpec((None, tf, D_MODEL), lambda b, i, k: (b, i, 0))
    return pl.pallas_call(
        _hy_inv_kernel,
        grid=(nb, seq // tf, seq // tk),
        in_specs=[dft_spec, dft_spec,
                  pl.BlockSpec((None, 2, tk, D_MODEL), lambda b, i, k: (b, 0, k, 0)),
                  row_spec, row_spec, row_spec, _mod_spec(trunk, layer), _ng_spec(layer),
                  _resident((1, D_MODEL)), _resident((D_MODEL, D_MODEL)), _resident((1, D_MODEL))],
        out_specs=row_spec,
        out_shape=jax.ShapeDtypeStruct(x.shape, f32),
        scratch_shapes=[pltpu.VMEM((tf, D_MODEL), f32)],
        compiler_params=_params("parallel", "parallel", "arbitrary"),
    )(cos_t, sin_t, ys, z, x0, x, mod, norm_g, p["d"], p["w_out"], p["b_out"])


def _softplus(y):
    return jnp.maximum(y, 0.0) + jnp.log(1.0 + jnp.exp(-jnp.abs(y)))


def _rw_pre_kernel(x_ref, xp_ref, xn_ref, mod_ref, ng_ref, mix_ref, wr_ref, wk_ref, wv_ref, g1_ref, g2_ref,
                   w0_ref, w1_ref, w2_ref, a0_ref, a1_ref, a2_ref,
                   r_ref, k_ref, v_ref, g_ref, lwf_ref, lwb_ref, af_ref, ab_ref):
    j = pl.program_id(1)
    g, sh, sc = ng_ref[0:1, :], mod_ref[0:1, :], mod_ref[1:2, :]
    h = _norm_mod(x_ref[...], g, sh, sc)
    prev_row = jnp.where(j == 0, 0.0, _norm_mod(xp_ref[...], g, sh, sc)[7:8, :])
    next_row = jnp.where(j == pl.num_programs(1) - 1, 0.0, _norm_mod(xn_ref[...], g, sh, sc)[0:1, :])
    h_prev, h_next = _shifted(h, prev_row, next_row)
    xx = 0.5 * (h_prev + h_next) - h

    def mixed(m):
        return (h + xx * mix_ref[m:m + 1, :]).astype(bf16)

    r_ref[...] = jnp.dot(mixed(0), wr_ref[...], preferred_element_type=f32)
    xw = mixed(1)
    k_ref[...] = jnp.dot(mixed(2), wk_ref[...], preferred_element_type=f32)
    v_ref[...] = jnp.dot(mixed(3), wv_ref[...], preferred_element_type=f32)
    xa = mixed(4)
    gate = _sigmoid(jnp.dot(mixed(5), g1_ref[...], preferred_element_type=f32))
    g_ref[...] = jnp.dot(gate.astype(bf16), g2_ref[...], preferred_element_type=f32)
    for dd, (lw_ref, a_ref) in enumerate(((lwf_ref, af_ref), (lwb_ref, ab_ref))):
        lora = jnp.tanh(jnp.dot(xw, w1_ref[dd], preferred_element_type=f32))
        wl = w0_ref[dd:dd + 1, :] + jnp.dot(lora.astype(bf16), w2_ref[dd], preferred_element_type=f32)
        w_log = -_softplus(-wl) - 0.5
        lw_ref[...] = -jnp.exp(w_log)
        al = jnp.dot(xa, a1_ref[dd], preferred_element_type=f32)
        a_ref[...] = _sigmoid(a0_ref[dd:dd + 1, :] + jnp.dot(al.astype(bf16), a2_ref[dd], preferred_element_type=f32))


def _bmm(a, b):
    return jnp.einsum("hij,hjk->hik", a.astype(bf16), b.astype(bf16), preferred_element_type=f32)


def _bmm_nt(a, b):
    return jnp.einsum("hik,hjk->hij", a.astype(bf16), b.astype(bf16), preferred_element_type=f32)


def _rw_scan_kernel(r_ref, k_ref, v_ref, lw_ref, a_ref, kkp_ref, kap_ref, rkp_ref, s0_ref,
                    o_ref, bn_ref, s_ref, *, reverse):
    c = pl.program_id(1)
    C = SCAN_CHUNK

    @pl.when(c == 0)
    def _():
        s_ref[...] = s0_ref[...]

    r, k, v, lw, a = r_ref[...], k_ref[...], v_ref[...], lw_ref[...], a_ref[...]
    kk = k * kkp_ref[...]
    kk = kk / jnp.maximum(jnp.sqrt(jnp.sum(kk * kk, axis=-1, keepdims=True)), 1e-12)
    kd = k * (1.0 + (a - 1.0) * kap_ref[...])
    b = kk * a
    bn_ref[...] = jnp.sum(r * kd * rkp_ref[...], axis=-1, keepdims=True) * v

    row = lax.broadcasted_iota(jnp.int32, (C, C), 0)
    col = lax.broadcasted_iota(jnp.int32, (C, C), 1)
    before = (col > row) if reverse else (col < row)
    before_eq = before | (col == row)
    eye = (col == row).astype(f32)

    tri = jnp.broadcast_to(before_eq.astype(bf16)[None], (RW_HEADS, C, C))
    l0 = lw.astype(bf16)
    res = lw - l0.astype(f32)
    l1 = res.astype(bf16)
    l2 = (res - l1.astype(f32)).astype(bf16)
    bdot = functools.partial(jnp.einsum, "hij,hjk->hik", preferred_element_type=f32)
    cum = bdot(tri, l0) + (bdot(tri, l1) + bdot(tri, l2))
    tot = jnp.sum(lw, axis=1, keepdims=True)

    e_in = jnp.exp(cum)
    e_out = jnp.exp(-cum)
    e_end = jnp.exp(tot - cum)
    kap = kk * jnp.exp(cum - lw)
    kt = kd * e_out
    bt = b * e_out
    rt = r * e_in
    kh = kd * e_end
    bh = b * e_end

    lhs = jnp.concatenate([kap, rt], axis=1)
    gk = _bmm_nt(lhs, kt)
    gb = _bmm_nt(lhs, bt)
    m_kk = jnp.where(before, gk[:, :C, :], 0.0)
    n_kb = jnp.where(before, gb[:, :C, :], 0.0)
    a_qk = jnp.where(before_eq, gk[:, C:, :], 0.0)
    a_qb = jnp.where(before_eq, gb[:, C:, :], 0.0)

    s = 1
    tinv = None
    while s < C:
        sh = s.bit_length() - 1
        same_pair = (row >> (sh + 1)) == (col >> (sh + 1))
        later, earlier = ((row >> sh) & 1, (col >> sh) & 1)
        off = same_pair & (((later == 0) & (earlier == 1)) if reverse else ((later == 1) & (earlier == 0)))
        n_off = jnp.where(off, n_kb, 0.0)
        tinv = (eye - n_off) if tinv is None else tinv - _bmm(_bmm(tinv, n_off), tinv)
        s *= 2

    mv = _bmm(m_kk, v)
    kt_c = _bmm(tinv, kap)
    w1 = _bmm(tinv, mv)
    q_eff = rt - _bmm(a_qb, kt_c)
    o_intra = _bmm(a_qk, v) - _bmm(a_qb, w1)
    phi = eye * jnp.exp(tot) - _bmm(jnp.swapaxes(kt_c, 1, 2), bh)
    d_s = _bmm(jnp.swapaxes(v, 1, 2), kh) - _bmm(jnp.swapaxes(w1, 1, 2), bh)

    s0 = s_ref[...]
    o_ref[...] = _bmm_nt(q_eff, s0) + o_intra
    s_ref[...] = _bmm(s0, phi) + d_s


def _rw_post_kernel(of_ref, ob_ref, bf_ref, bb_ref, g_ref, x_ref, mod_ref, ng_ref, lw_ref, lb_ref, ones_ref,
                    wo_ref, o_ref):
    o = of_ref[...] + ob_ref[...]
    ones = ones_ref[...]
    inv_n = 1.0 / RW_HEAD

    def head_mean(t):
        hi, lo = _split(t)
        return (jnp.dot(hi, ones, preferred_element_type=f32)
                + jnp.dot(lo, ones, preferred_element_type=f32)) * inv_n

    dev = o - head_mean(o)
    var = head_mean(dev * dev)
    on = dev * lax.rsqrt(var + RW_LN_EPS) * lw_ref[...] + lb_ref[...] + (bf_ref[...] + bb_ref[...])
    y = jnp.dot((on * g_ref[...]).astype(bf16), wo_ref[...], preferred_element_type=f32)
    o_ref[...] = _residual(x_ref[...], y, ng_ref[1:2, :], mod_ref[2:3, :])


@functools.lru_cache(maxsize=None)
def _head_ones():
    idx = np.arange(D_MODEL) // RW_HEAD
    return np.asarray(idx[:, None] == idx[None, :], dtype=bf16)


def _rwkv_layer(x, trunk, layer, mod, norm_g, p, s0_fwd, s0_bwd):
    nb, seq = trunk.batch, trunk.seq
    tl = _tile_rows(trunk, 256)
    x8 = x.reshape(nb, seq // 8, 8, D_MODEL)
    prev_spec, next_spec = _halo_specs(trunk, tl)
    lora_w, lora_a, lora_g = p["w1"].shape[-1], p["a1"].shape[-1], p["g1"].shape[-1]
    outs = pl.pallas_call(
        _rw_pre_kernel,
        grid=(nb, seq // tl),
        in_specs=[_x_spec(tl), prev_spec, next_spec, _mod_spec(trunk, layer), _ng_spec(layer),
                  _resident((6, D_MODEL)),
                  _resident((D_MODEL, D_MODEL)), _resident((D_MODEL, D_MODEL)), _resident((D_MODEL, D_MODEL)),
                  _resident((D_MODEL, lora_g)), _resident((lora_g, D_MODEL)),
                  _resident((2, D_MODEL)), _resident((2, D_MODEL, lora_w)), _resident((2, lora_w, D_MODEL)),
                  _resident((2, D_MODEL)), _resident((2, D_MODEL, lora_a)), _resident((2, lora_a, D_MODEL))],
        out_specs=[_x_spec(tl)] * 8,
        out_shape=[jax.ShapeDtypeStruct(x.shape, f32)] * 8,
        compiler_params=_params("parallel", "parallel"),
    )(x, x8, x8, mod, norm_g, p["mix"], p["wr"], p["wk"], p["wv"], p["g1"], p["g2"],
      p["w0"], p["w1"], p["w2"], p["a0"], p["a1"], p["a2"])
    r, k, v, gate, lw_f, lw_b, a_f, a_b = outs

    def to_heads(t):
        return t.reshape(nb, seq, RW_HEADS, RW_HEAD).transpose(0, 2, 1, 3)

    def from_heads(t):
        return t.transpose(0, 2, 1, 3).reshape(nb, seq, D_MODEL)

    r_h, k_h, v_h = to_heads(r), to_heads(k), to_heads(v)
    n_chunks = seq // SCAN_CHUNK
    head_shape = jax.ShapeDtypeStruct((nb, RW_HEADS, seq, RW_HEAD), f32)
    state_spec = pl.BlockSpec((None, RW_HEADS, RW_HEAD, RW_HEAD), lambda b, c: (b, 0, 0, 0))
    results = []
    for reverse, lw, a, s0 in ((False, lw_f, a_f, s0_fwd), (True, lw_b, a_b, s0_bwd)):
        if reverse:
            chunk_spec = pl.BlockSpec((None, RW_HEADS, SCAN_CHUNK, RW_HEAD),
                                      lambda b, c: (b, 0, n_chunks - 1 - c, 0))
        else:
            chunk_spec = pl.BlockSpec((None, RW_HEADS, SCAN_CHUNK, RW_HEAD), lambda b, c: (b, 0, c, 0))
        results.append(pl.pallas_call(
            functools.partial(_rw_scan_kernel, reverse=reverse),
            grid=(nb, n_chunks),
            in_specs=[chunk_spec] * 5 + [_resident((RW_HEADS, 1, RW_HEAD))] * 3 + [state_spec],
            out_specs=[chunk_spec, chunk_spec, state_spec],
            out_shape=[head_shape, head_shape,
                       jax.ShapeDtypeStruct((nb, RW_HEADS, RW_HEAD, RW_HEAD), f32)],
            compiler_params=_params("parallel", "arbitrary"),
        )(r_h, k_h, v_h, to_heads(lw), to_heads(a), p["kk"], p["ka"], p["rk"], s0))
    (o_f, bn_f, s_f), (o_b, bn_b, s_b) = results

    y = pl.pallas_call(
        _rw_post_kernel,
        grid=(nb, seq // tl),
        in_specs=[_x_spec(tl)] * 6 + [_mod_spec(trunk, layer), _ng_spec(layer),
                                      _resident((1, D_MODEL)), _resident((1, D_MODEL)),
                                      _resident((D_MODEL, D_MODEL)), _resident((D_MODEL, D_MODEL))],
        out_specs=_x_spec(tl),
        out_shape=jax.ShapeDtypeStruct(x.shape, f32),
        compiler_params=_params("parallel", "parallel"),
    )(from_heads(o_f), from_heads(o_b), from_heads(bn_f), from_heads(bn_b), gate, x, mod, norm_g,
      p["lnx_w"], p["lnx_b"], jnp.asarray(_head_ones()), p["wo"])
    return y, s_f, s_b


def kernel(x_prompt, x_sample, state_wkv_fwd, state_wkv_bwd, c, c_ctx, ada_w, ada_b, norm_g, ffn_w1, ffn_w3, ffn_w2, fn_w_out, fn_b_out, hy_w_in, hy_b_in, hy_conv_w, hy_conv_b, hy_f_w1, hy_f_b1, hy_f_freq, hy_f_w2, hy_f_b2, hy_f_w3, hy_d, hy_w_out, hy_b_out, rw_mix, rw_wr, rw_wk, rw_wv, rw_wo, rw_w0, rw_w1, rw_w2, rw_a0, rw_a1, rw_a2, rw_g1, rw_g2, rw_kk, rw_ka, rw_rk, rw_lnx_w, rw_lnx_b):
    n_ctx, n_dec = x_prompt.shape[0], x_sample.shape[0]
    assert 1 + n_dec <= MOD_ROWS
    trunks = (Trunk(n_ctx, x_prompt.shape[1], 0, False), Trunk(n_dec, x_sample.shape[1], 1, True))
    cond = jnp.zeros((MOD_ROWS, D_MODEL), f32).at[0].set(c_ctx).at[1:1 + n_dec].set(c)
    mod = _adaln(cond, ada_w, ada_b)

    n_rwkv = rw_w0.shape[0]
    zero_state = jnp.zeros((n_ctx, n_rwkv, RW_HEADS, RW_HEAD, RW_HEAD), f32)
    states = ((zero_state, zero_state), (state_wkv_fwd, state_wkv_bwd))
    xs = [x_prompt, x_sample]
    new_fwd, new_bwd = [], []
    for i in range(DEPTH):
        kind, j = i % N_MIXERS, i // N_MIXERS
        if kind == 0:
            w_out, b_out = fn_w_out[j].astype(bf16), fn_b_out[j][None]
            xs = [_fourier_layer(x, t, i, mod, norm_g, w_out, b_out) for x, t in zip(xs, trunks)]
        elif kind == 1:
            w1_pad = jnp.zeros((HY_EMB_PAD, HY_FILT), f32).at[:HY_EMB].set(hy_f_w1[j])
            p = dict(w_in=hy_w_in[j].astype(bf16), b_in=hy_b_in[j][None], conv_w=hy_conv_w[j],
                     conv_b=hy_conv_b[j][None], f_w1=w1_pad, f_b1=hy_f_b1[j][None], f_freq=hy_f_freq[j],
                     f_w2=hy_f_w2[j], f_b2=hy_f_b2[j][None], f_w3=hy_f_w3[j], d=hy_d[j][None],
                     w_out=hy_w_out[j].astype(bf16), b_out=hy_b_out[j][None])
            xs = [_hyena_layer(x, t, i, mod, norm_g, p) for x, t in zip(xs, trunks)]
        else:
            per_head = lambda t: t.reshape(RW_HEADS, 1, RW_HEAD)
            p = dict(mix=rw_mix[j], wr=rw_wr[j].astype(bf16), wk=rw_wk[j].astype(bf16),
                     wv=rw_wv[j].astype(bf16), wo=rw_wo[j].astype(bf16), w0=rw_w0[j],
                     w1=rw_w1[j].astype(bf16), w2=rw_w2[j].astype(bf16), a0=rw_a0[j],
                     a1=rw_a1[j].astype(bf16), a2=rw_a2[j].astype(bf16), g1=rw_g1[j].astype(bf16),
                     g2=rw_g2[j].astype(bf16), kk=per_head(rw_kk[j]), ka=per_head(rw_ka[j]),
                     rk=per_head(rw_rk[j]), lnx_w=rw_lnx_w[j][None], lnx_b=rw_lnx_b[j][None])
            outs = [_rwkv_layer(x, t, i, mod, norm_g, p, sf[:, j], sb[:, j])
                    for x, t, (sf, sb) in zip(xs, trunks, states)]
            xs = [o[0] for o in outs]
            new_fwd.append(outs[0][1])
            new_bwd.append(outs[0][2])
        w1, w3, w2 = ffn_w1[i].astype(bf16), ffn_w3[i].astype(bf16), ffn_w2[i].astype(bf16)
        xs = [_ffn(x, t, i, mod, norm_g, w1, w3, w2) for x, t in zip(xs, trunks)]
    return xs[0], xs[1], jnp.stack(new_fwd, axis=1), jnp.stack(new_bwd, axis=1)
```

```python
import functools
import math
from typing import NamedTuple

import numpy as np
import jax
import jax.numpy as jnp
from jax import lax
from jax.experimental import pallas as pl
from jax.experimental.pallas import tpu as pltpu

f32 = jnp.float32
bf16 = jnp.bfloat16

D_MODEL = 1024
DEPTH = 4
N_MIXERS = 3
D_FF = 2816
NORM_EPS = 1e-6
FN_GROUP_W = 256
HY_EMB = 33
HY_EMB_PAD = 128
HY_BANDS = 16
HY_FILT = 64
HY_TARGET = 1e-2
HY_FAST = 0.3
HY_SLOW = 1.5
RW_HEAD = 64
RW_HEADS = 16
RW_LN_EPS = 64e-5
SCAN_CHUNK = 64
SCAN_GROUP = 2
PAIR_W = 2 * RW_HEAD
assert SCAN_CHUNK == RW_HEAD, "the scan keeps (chunk, chunk) and (chunk, head) tiles in one lane layout"
MOD_ROWS = 8
VMEM_LIMIT = 56 * 1024 * 1024


class Trunk(NamedTuple):
    batch: int
    seq: int
    mod_base: int
    per_batch_mod: bool


def _params(*sem):
    return pltpu.CompilerParams(dimension_semantics=sem, vmem_limit_bytes=VMEM_LIMIT)


def _resident(shape):
    nd = len(shape)
    return pl.BlockSpec(shape, lambda *_: (0,) * nd, pipeline_mode=pl.Buffered(1))


def _dot(a, b):
    return jnp.dot(a.astype(bf16), b.astype(bf16), preferred_element_type=f32)


def _split(x):
    hi = x.astype(bf16)
    lo = (x - hi.astype(f32)).astype(bf16)
    return hi, lo


def _dot3(a, b):
    ah, al = _split(a)
    bh, bl = _split(b)
    d = functools.partial(jnp.dot, preferred_element_type=f32)
    return d(ah, bh) + (d(ah, bl) + d(al, bh))


def _dot_exact_rhs_left(a_exact, b):
    b0 = b.astype(bf16)
    r1 = b - b0.astype(f32)
    b1 = r1.astype(bf16)
    b2 = (r1 - b1.astype(f32)).astype(bf16)
    d = functools.partial(jnp.dot, preferred_element_type=f32)
    return d(a_exact, b0) + (d(a_exact, b1) + d(a_exact, b2))


def _sigmoid(x):
    return 1.0 / (1.0 + jnp.exp(-x))


def _norm_mod(x, g, shift, scale):
    ms = jnp.mean(x * x, axis=-1, keepdims=True)
    return (x * lax.rsqrt(ms + NORM_EPS) * g) * (1.0 + scale) + shift


def _residual(x, y, g, gate):
    ms = jnp.mean(y * y, axis=-1, keepdims=True)
    return x + gate * (y * lax.rsqrt(ms + NORM_EPS) * g)


def _shifted(h, prev_row, next_row):
    rows = h.shape[0]
    ridx = lax.broadcasted_iota(jnp.int32, h.shape, 0)
    h_prev = jnp.where(ridx == 0, prev_row, pltpu.roll(h, 1, 0))
    h_next = jnp.where(ridx == rows - 1, next_row, pltpu.roll(h, rows - 1, 0))
    return h_prev, h_next


def _tile_rows(trunk, cap):
    return min(trunk.seq, cap)


def _x_spec(tl):
    return pl.BlockSpec((None, tl, D_MODEL), lambda b, j: (b, j, 0))


def _halo_specs(trunk, tl):
    g = tl // 8
    last = trunk.seq // 8 - 1
    prev = pl.BlockSpec((None, None, 8, D_MODEL), lambda b, j: (b, jnp.maximum(j * g - 1, 0), 0, 0))
    nxt = pl.BlockSpec((None, None, 8, D_MODEL), lambda b, j: (b, jnp.minimum((j + 1) * g, last), 0, 0))
    return prev, nxt


def _mod_spec(trunk, layer):
    if trunk.per_batch_mod:
        return pl.BlockSpec((None, None, 6, D_MODEL), lambda b, *_: (layer, trunk.mod_base + b, 0, 0))
    return pl.BlockSpec((None, None, 6, D_MODEL), lambda b, *_: (layer, trunk.mod_base, 0, 0))


def _ng_spec(layer):
    return pl.BlockSpec((None, 4, D_MODEL), lambda *_: (layer, 0, 0))


def _mod_kernel(c_ref, w_ref, b_ref, o_ref):
    c = c_ref[...]
    o_ref[...] = _dot3(c * _sigmoid(c), w_ref[...]) + b_ref[...]


def _adaln(cond, ada_w, ada_b):
    tn = 1536
    out = pl.pallas_call(
        _mod_kernel,
        grid=(DEPTH, 6 * D_MODEL // tn),
        in_specs=[pl.BlockSpec((MOD_ROWS, D_MODEL), lambda l, j: (0, 0)),
                  pl.BlockSpec((None, D_MODEL, tn), lambda l, j: (l, 0, j)),
                  pl.BlockSpec((None, 1, tn), lambda l, j: (l, 0, j))],
        out_specs=pl.BlockSpec((None, MOD_ROWS, tn), lambda l, j: (l, 0, j)),
        out_shape=jax.ShapeDtypeStruct((DEPTH, MOD_ROWS, 6 * D_MODEL), f32),
        compiler_params=_params("arbitrary", "arbitrary"),
    )(cond, ada_w, ada_b.reshape(DEPTH, 1, 6 * D_MODEL))
    return out.reshape(DEPTH, MOD_ROWS, 6, D_MODEL)


def _ffn_kernel(x_ref, mod_ref, ng_ref, w1_ref, w3_ref, w2_ref, o_ref):
    x = x_ref[...]
    h = _norm_mod(x, ng_ref[2:3, :], mod_ref[3:4, :], mod_ref[4:5, :]).astype(bf16)
    a = jnp.dot(h, w1_ref[...], preferred_element_type=f32)
    b = jnp.dot(h, w3_ref[...], preferred_element_type=f32)
    gated = (a * _sigmoid(a) * b).astype(bf16)
    y = jnp.dot(gated, w2_ref[...], preferred_element_type=f32)
    o_ref[...] = _residual(x, y, ng_ref[3:4, :], mod_ref[5:6, :])


def _ffn(x, trunk, layer, mod, norm_g, w1, w3, w2):
    tl = _tile_rows(trunk, 512)
    return pl.pallas_call(
        _ffn_kernel,
        grid=(trunk.batch, trunk.seq // tl),
        in_specs=[_x_spec(tl), _mod_spec(trunk, layer), _ng_spec(layer),
                  _resident((D_MODEL, D_FF)), _resident((D_MODEL, D_FF)), _resident((D_FF, D_MODEL))],
        out_specs=_x_spec(tl),
        out_shape=jax.ShapeDtypeStruct(x.shape, f32),
        compiler_params=_params("parallel", "parallel"),
    )(x, mod, norm_g, w1, w3, w2)


@functools.lru_cache(maxsize=None)
def _channel_dft():
    w = FN_GROUP_W
    idx = np.arange(w)
    ang = 2.0 * np.pi * ((idx[:, None] * idx[None, :]) % w) / w
    out = np.zeros((D_MODEL, 2 * D_MODEL), np.float64)
    for g in range(D_MODEL // w):
        out[g * w:(g + 1) * w, g * w:(g + 1) * w] = np.cos(ang) / math.sqrt(w)
        out[g * w:(g + 1) * w, D_MODEL + g * w:D_MODEL + (g + 1) * w] = np.sin(ang) / math.sqrt(w)
    return out.astype(np.float32)


@functools.lru_cache(maxsize=None)
def _time_dft(n_rows, period, scale):
    idx = np.arange(n_rows)
    ang = 2.0 * np.pi * ((idx[:, None] * idx[None, :]) % period) / period
    return (np.cos(ang) * scale).astype(np.float32), (np.sin(ang) * scale).astype(np.float32)


def _mxu_table(table):
    return jnp.asarray(table).astype(bf16)


def _fn_a_kernel(x_ref, mod_ref, ng_ref, w_ref, o_ref):
    h = _norm_mod(x_ref[...], ng_ref[0:1, :], mod_ref[0:1, :], mod_ref[1:2, :]).astype(bf16)
    o_ref[...] = jnp.dot(h, w_ref[...], preferred_element_type=f32).astype(bf16)


def _fn_b_kernel(c_ref, s_ref, ab_ref, x_ref, mod_ref, ng_ref, wo_ref, bo_ref, o_ref, acc_ref):
    k = pl.program_id(2)

    @pl.when(k == 0)
    def _():
        acc_ref[...] = jnp.zeros_like(acc_ref)

    a = ab_ref[:, :D_MODEL]
    b = ab_ref[:, D_MODEL:]
    acc_ref[...] += (jnp.dot(c_ref[...], a, preferred_element_type=f32)
                     - jnp.dot(s_ref[...], b, preferred_element_type=f32))

    @pl.when(k == pl.num_programs(2) - 1)
    def _():
        y = jnp.dot(acc_ref[...].astype(bf16), wo_ref[...], preferred_element_type=f32) + bo_ref[...]
        o_ref[...] = _residual(x_ref[...], y, ng_ref[1:2, :], mod_ref[2:3, :])


def _fourier_layer(x, trunk, layer, mod, norm_g, w_out, b_out):
    tl = _tile_rows(trunk, 512)
    ab = pl.pallas_call(
        _fn_a_kernel,
        grid=(trunk.batch, trunk.seq // tl),
        in_specs=[_x_spec(tl), _mod_spec(trunk, layer), _ng_spec(layer), _resident((D_MODEL, 2 * D_MODEL))],
        out_specs=pl.BlockSpec((None, tl, 2 * D_MODEL), lambda b, j: (b, j, 0)),
        out_shape=jax.ShapeDtypeStruct((trunk.batch, trunk.seq, 2 * D_MODEL), bf16),
        compiler_params=_params("parallel", "parallel"),
    )(x, mod, norm_g, _mxu_table(_channel_dft()))

    cos_t, sin_t = _time_dft(trunk.seq, trunk.seq, 1.0 / math.sqrt(trunk.seq))
    tm = _tile_rows(trunk, 512)
    tk = _tile_rows(trunk, 512)
    return pl.pallas_call(
        _fn_b_kernel,
        grid=(trunk.batch, trunk.seq // tm, trunk.seq // tk),
        in_specs=[pl.BlockSpec((tm, tk), lambda b, i, k: (i, k)),
                  pl.BlockSpec((tm, tk), lambda b, i, k: (i, k)),
                  pl.BlockSpec((None, tk, 2 * D_MODEL), lambda b, i, k: (b, k, 0)),
                  pl.BlockSpec((None, tm, D_MODEL), lambda b, i, k: (b, i, 0)),
                  _mod_spec(trunk, layer), _ng_spec(layer),
                  _resident((D_MODEL, D_MODEL)), _resident((1, D_MODEL))],
        out_specs=pl.BlockSpec((None, tm, D_MODEL), lambda b, i, k: (b, i, 0)),
        out_shape=jax.ShapeDtypeStruct(x.shape, f32),
        scratch_shapes=[pltpu.VMEM((tm, D_MODEL), f32)],
        compiler_params=_params("parallel", "parallel", "arbitrary"),
    )(_mxu_table(cos_t), _mxu_table(sin_t), ab, x, mod, norm_g, w_out, b_out)


@functools.lru_cache(maxsize=None)
def _hyena_features(seq):
    t = np.linspace(0.0, 1.0, seq)[:, None]
    ang = 2.0 * np.pi * np.arange(seq)[:, None] / seq
    bands = np.linspace(1e-4, HY_BANDS - 1, HY_BANDS)[None]
    z = np.concatenate([t, np.cos(bands * ang), -np.sin(bands * ang)], axis=-1)
    out = np.zeros((seq, HY_EMB_PAD), np.float32)
    out[:, :HY_EMB] = z
    return out


@functools.lru_cache(maxsize=None)
def _hyena_decay_rates():
    d = np.linspace(math.log(HY_TARGET) / HY_FAST, math.log(HY_TARGET) / HY_SLOW, D_MODEL)
    return np.abs(d)[None].astype(np.float32)


def _hy_in_kernel(x_ref, xp_ref, xn_ref, mod_ref, ng_ref, w_ref, b_ref, cw_ref, cb_ref, x0_ref, z_ref):
    j = pl.program_id(1)
    g, sh, sc = ng_ref[0:1, :], mod_ref[0:1, :], mod_ref[1:2, :]
    tl = x_ref.shape[0]
    rows = jnp.concatenate([x_ref[...], xp_ref[...], xn_ref[...]], axis=0)
    u_all = jnp.dot(_norm_mod(rows, g, sh, sc).astype(bf16), w_ref[...], preferred_element_type=f32) + b_ref[...]
    u = u_all[:tl, :]
    prev_row = jnp.where(j == 0, 0.0, u_all[tl + 7:tl + 8, :])
    next_row = jnp.where(j == pl.num_programs(1) - 1, 0.0, u_all[tl + 8:tl + 9, :])
    u_prev, u_next = _shifted(u, prev_row, next_row)
    uc = u_prev * cw_ref[0:1, :] + u * cw_ref[1:2, :] + u_next * cw_ref[2:3, :] + cb_ref[...]
    x0_ref[...] = uc[:, :D_MODEL]
    z_ref[...] = uc[:, 2 * D_MODEL:] * uc[:, D_MODEL:2 * D_MODEL]


def _hy_filter_kernel(feat_ref, w1_ref, b1_ref, fq_ref, w2_ref, b2_ref, w3f_ref, w3b_ref, rate_ref,
                      cos_ref, sin_ref, o_ref, *, seq):
    hdn = jnp.sin(fq_ref[0:1, :] * (_dot3(feat_ref[...], w1_ref[...]) + b1_ref[...]))
    hdn = jnp.sin(fq_ref[1:2, :] * (_dot3(hdn, w2_ref[...]) + b2_ref[...]))
    rows = lax.broadcasted_iota(jnp.int32, (seq, 1), 0)
    t = rows.astype(f32) * (1.0 / (seq - 1))
    win = jnp.exp(-t * rate_ref[...])
    k_fwd = _dot3(hdn, w3f_ref[...]) * win
    k_bwd = _dot3(hdn, w3b_ref[...]) * win
    both = k_fwd + k_bwd
    diff = jnp.where(rows == 0, both, k_fwd - k_bwd)
    k_re = jnp.dot(cos_ref[...], both.astype(bf16), preferred_element_type=f32)
    k_im = jnp.dot(sin_ref[...], diff.astype(bf16), preferred_element_type=f32)
    sign = (1 - 2 * (rows & 1)).astype(f32)
    k_nyq = jnp.sum(both * sign, axis=0, keepdims=True)
    wgt = jnp.where(rows == 0, 0.5 / seq, 1.0 / seq)
    o_ref[0] = k_re * wgt
    o_ref[1] = jnp.where(rows == 0, k_nyq, -k_im) * wgt


def _hy_fwd_kernel(cos_ref, sin_ref, z_ref, ks_ref, o_ref, re_ref, im_ref, *, tk):
    i = pl.program_id(1)
    k = pl.program_id(2)

    @pl.when(k == 0)
    def _():
        re_ref[...] = jnp.zeros_like(re_ref)
        im_ref[...] = jnp.zeros_like(im_ref)

    z = z_ref[...]
    zb = z.astype(bf16)
    re_ref[...] += jnp.dot(cos_ref[...], zb, preferred_element_type=f32)
    im_ref[...] -= jnp.dot(sin_ref[...], zb, preferred_element_type=f32)

    @pl.when(i == 0)
    def _():
        t = k * tk + lax.broadcasted_iota(jnp.int32, (tk, 1), 0)
        sign = (1 - 2 * (t & 1)).astype(f32)
        im_ref[0:1, :] += jnp.sum(z * sign, axis=0, keepdims=True)

    @pl.when(k == pl.num_programs(2) - 1)
    def _():
        zr, zi = re_ref[...], im_ref[...]
        kr, ki = ks_ref[0], ks_ref[1]
        f = i * zr.shape[0] + lax.broadcasted_iota(jnp.int32, (zr.shape[0], 1), 0)
        packed = f == 0
        o_ref[0] = jnp.where(packed, zr * kr, zr * kr - zi * ki).astype(bf16)
        o_ref[1] = jnp.where(packed, zi * ki, zr * ki + zi * kr).astype(bf16)


def _hy_inv_kernel(cos_ref, sin_ref, ys_ref, z_ref, x0_ref, x_ref, mod_ref, ng_ref, d_ref, wo_ref, bo_ref,
                   o_ref, acc_ref):
    i = pl.program_id(1)
    k = pl.program_id(2)
    tm = acc_ref.shape[0]

    @pl.when(k == 0)
    def _():
        acc_ref[...] = jnp.zeros_like(acc_ref)

    acc_ref[...] += (jnp.dot(cos_ref[...], ys_ref[0], preferred_element_type=f32)
                     - jnp.dot(sin_ref[...], ys_ref[1], preferred_element_type=f32))

    @pl.when(k == 0)
    def _():
        t = i * tm + lax.broadcasted_iota(jnp.int32, (tm, 1), 0)
        sign = (1 - 2 * (t & 1)).astype(f32)
        acc_ref[...] += sign * ys_ref[1, 0:1, :].astype(f32)

    @pl.when(k == pl.num_programs(2) - 1)
    def _():
        z = z_ref[...]
        y = acc_ref[...] + z * d_ref[...]
        out = jnp.dot((y * x0_ref[...]).astype(bf16), wo_ref[...], preferred_element_type=f32) + bo_ref[...]
        o_ref[...] = _residual(x_ref[...], out, ng_ref[1:2, :], mod_ref[2:3, :])


def _hyena_layer(x, trunk, layer, mod, norm_g, p):
    nb, seq = trunk.batch, trunk.seq
    tl = _tile_rows(trunk, 256)
    x8 = x.reshape(nb, seq // 8, 8, D_MODEL)
    prev_spec, next_spec = _halo_specs(trunk, tl)
    x0, z = pl.pallas_call(
        _hy_in_kernel,
        grid=(nb, seq // tl),
        in_specs=[_x_spec(tl), prev_spec, next_spec, _mod_spec(trunk, layer), _ng_spec(layer),
                  _resident((D_MODEL, 3 * D_MODEL)), _resident((1, 3 * D_MODEL)),
                  _resident((3, 3 * D_MODEL)), _resident((1, 3 * D_MODEL))],
        out_specs=[_x_spec(tl), _x_spec(tl)],
        out_shape=[jax.ShapeDtypeStruct(x.shape, f32)] * 2,
        compiler_params=_params("parallel", "parallel"),
    )(x, x8, x8, mod, norm_g, p["w_in"], p["b_in"], p["conv_w"], p["conv_b"])

    n = 2 * seq
    cos_t, sin_t = _time_dft(seq, n, 1.0)
    cos_t, sin_t = _mxu_table(cos_t), _mxu_table(sin_t)
    tn = 256
    nblk = D_MODEL // tn
    ks = pl.pallas_call(
        functools.partial(_hy_filter_kernel, seq=seq),
        grid=(nblk,),
        in_specs=[_resident((seq, HY_EMB_PAD)), _resident((HY_EMB_PAD, HY_FILT)), _resident((1, HY_FILT)),
                  _resident((2, HY_FILT)), _resident((HY_FILT, HY_FILT)), _resident((1, HY_FILT)),
                  pl.BlockSpec((HY_FILT, tn), lambda j: (0, j)),
                  pl.BlockSpec((HY_FILT, tn), lambda j: (0, nblk + j)),
                  pl.BlockSpec((1, tn), lambda j: (0, j)),
                  _resident((seq, seq)), _resident((seq, seq))],
        out_specs=pl.BlockSpec((2, seq, tn), lambda j: (0, 0, j)),
        out_shape=jax.ShapeDtypeStruct((2, seq, D_MODEL), f32),
        compiler_params=_params("parallel"),
    )(jnp.asarray(_hyena_features(seq)), p["f_w1"], p["f_b1"], p["f_freq"], p["f_w2"], p["f_b2"],
      p["f_w3"], p["f_w3"], jnp.asarray(_hyena_decay_rates()), cos_t, sin_t)

    tf = _tile_rows(trunk, 512)
    tk = _tile_rows(trunk, 512)
    dft_spec = pl.BlockSpec((tf, tk), lambda b, i, k: (i, k))
    ys = pl.pallas_call(
        functools.partial(_hy_fwd_kernel, tk=tk),
        grid=(nb, seq // tf, seq // tk),
        in_specs=[dft_spec, dft_spec,
                  pl.BlockSpec((None, tk, D_MODEL), lambda b, i, k: (b, k, 0)),
                  pl.BlockSpec((2, tf, D_MODEL), lambda b, i, k: (0, i, 0))],
        out_specs=pl.BlockSpec((None, 2, tf, D_MODEL), lambda b, i, k: (b, 0, i, 0)),
        out_shape=jax.ShapeDtypeStruct((nb, 2, seq, D_MODEL), bf16),
        scratch_shapes=[pltpu.VMEM((tf, D_MODEL), f32), pltpu.VMEM((tf, D_MODEL), f32)],
        compiler_params=_params("parallel", "parallel", "arbitrary"),
    )(cos_t, sin_t, z, ks)

    row_spec = pl.BlockSpec((None, tf, D_MODEL), lambda b, i, k: (b, i, 0))
    return pl.pallas_call(
        _hy_inv_kernel,
        grid=(nb, seq // tf, seq // tk),
        in_specs=[dft_spec, dft_spec,
                  pl.BlockSpec((None, 2, tk, D_MODEL), lambda b, i, k: (b, 0, k, 0)),
                  row_spec, row_spec, row_spec, _mod_spec(trunk, layer), _ng_spec(layer),
                  _resident((1, D_MODEL)), _resident((D_MODEL, D_MODEL)), _resident((1, D_MODEL))],
        out_specs=row_spec,
        out_shape=jax.ShapeDtypeStruct(x.shape, f32),
        scratch_shapes=[pltpu.VMEM((tf, D_MODEL), f32)],
        compiler_params=_params("parallel", "parallel", "arbitrary"),
    )(cos_t, sin_t, ys, z, x0, x, mod, norm_g, p["d"], p["w_out"], p["b_out"])


def _softplus(y):
    return jnp.maximum(y, 0.0) + jnp.log(1.0 + jnp.exp(-jnp.abs(y)))


def _head_sum(t, ones_pair):
    hi, lo = _split(t)
    cols = []
    for p in range(D_MODEL // PAIR_W):
        sl = slice(p * PAIR_W, (p + 1) * PAIR_W)
        cols.append(jnp.dot(hi[:, sl], ones_pair, preferred_element_type=f32)
                    + jnp.dot(lo[:, sl], ones_pair, preferred_element_type=f32))
    return jnp.concatenate(cols, axis=1)


def _rw_pre_kernel(x_ref, xp_ref, xn_ref, mod_ref, ng_ref, mix_ref, wr_ref, wk_ref, wv_ref, g1_ref, g2_ref,
                   w0_ref, w1_ref, w2_ref, a0_ref, a1_ref, a2_ref, kkp_ref, kap_ref, rkp_ref, ones_ref,
                   r_ref, k_ref, v_ref, kk_ref, g_ref, lwf_ref, lwb_ref, af_ref, ab_ref, bn_ref):
    j = pl.program_id(1)
    g, sh, sc = ng_ref[0:1, :], mod_ref[0:1, :], mod_ref[1:2, :]
    h = _norm_mod(x_ref[...], g, sh, sc)
    prev_row = jnp.where(j == 0, 0.0, _norm_mod(xp_ref[...], g, sh, sc)[7:8, :])
    next_row = jnp.where(j == pl.num_programs(1) - 1, 0.0, _norm_mod(xn_ref[...], g, sh, sc)[0:1, :])
    h_prev, h_next = _shifted(h, prev_row, next_row)
    xx = 0.5 * (h_prev + h_next) - h

    def mixed(m):
        return (h + xx * mix_ref[m:m + 1, :]).astype(bf16)

    r = jnp.dot(mixed(0), wr_ref[...], preferred_element_type=f32)
    xw = mixed(1)
    k = jnp.dot(mixed(2), wk_ref[...], preferred_element_type=f32)
    v = jnp.dot(mixed(3), wv_ref[...], preferred_element_type=f32)
    xa = mixed(4)
    r_ref[...] = r
    k_ref[...] = k
    v_ref[...] = v
    gate = _sigmoid(jnp.dot(mixed(5), g1_ref[...], preferred_element_type=f32))
    g_ref[...] = jnp.dot(gate.astype(bf16), g2_ref[...], preferred_element_type=f32)
    ones = ones_ref[...]
    kk = k * kkp_ref[...]
    kk_ref[...] = kk * lax.rsqrt(jnp.maximum(_head_sum(kk * kk, ones), 1e-24))
    a_sum = None
    for dd, (lw_ref, a_ref) in enumerate(((lwf_ref, af_ref), (lwb_ref, ab_ref))):
        lora = jnp.tanh(jnp.dot(xw, w1_ref[dd], preferred_element_type=f32))
        wl = w0_ref[dd:dd + 1, :] + jnp.dot(lora.astype(bf16), w2_ref[dd], preferred_element_type=f32)
        w_log = -_softplus(-wl) - 0.5
        lw_ref[...] = -jnp.exp(w_log)
        al = jnp.dot(xa, a1_ref[dd], preferred_element_type=f32)
        a = _sigmoid(a0_ref[dd:dd + 1, :] + jnp.dot(al.astype(bf16), a2_ref[dd], preferred_element_type=f32))
        a_ref[...] = a
        a_sum = a if a_sum is None else a_sum + a
    kd_sum = k * (2.0 + (a_sum - 2.0) * kap_ref[...])
    bn_ref[...] = _head_sum(r * kd_sum * rkp_ref[...], ones) * v


def _stack_heads(x, first_head):
    return jnp.concatenate([jnp.where(first_head, x, 0.0), jnp.where(first_head, 0.0, x)], axis=0).astype(bf16)


def _dot_nt(a, b):
    return lax.dot_general(a, b, (((1,), (1,)), ((), ())), preferred_element_type=f32)


def _dot_tn(a, b):
    return lax.dot_general(a, b, (((0,), (0,)), ((), ())), preferred_element_type=f32)


def _rw_scan_kernel(r_ref, k_ref, v_ref, kk_ref, lw_ref, a_ref, kap_ref, s0_ref, o_ref, s_ref, *, reverse):
    c = pl.program_id(1)
    C, W, G = SCAN_CHUNK, PAIR_W, SCAN_GROUP

    @pl.when(c == 0)
    def _():
        s_ref[...] = s0_ref[...]

    row = lax.broadcasted_iota(jnp.int32, (C, W), 0)
    lane = lax.broadcasted_iota(jnp.int32, (C, W), 1)
    col = lane & (C - 1)
    first_head = lane < RW_HEAD
    before = (col > row) if reverse else (col < row)
    before_eq = before | (col == row)
    eye = (col == row).astype(f32)
    row_w = lax.broadcasted_iota(jnp.int32, (W, W), 0)
    lane_w = lax.broadcasted_iota(jnp.int32, (W, W), 1)
    same_head = (row_w >= RW_HEAD) == (lane_w >= RW_HEAD)
    diag_w = row_w == lane_w
    merge_masks = []
    s = 1
    while s < C:
        sh = s.bit_length() - 1
        same_pair = (row >> (sh + 1)) == (col >> (sh + 1))
        later, earlier = ((row >> sh) & 1, (col >> sh) & 1)
        merge_masks.append(same_pair & (((later == 0) & (earlier == 1)) if reverse
                                        else ((later == 1) & (earlier == 0))))
        s *= 2

    row_c = lax.broadcasted_iota(jnp.int32, (G * C, G * C), 0)
    col_c = lax.broadcasted_iota(jnp.int32, (G * C, G * C), 1)
    chunk_bits = C.bit_length() - 1
    same_chunk = (row_c >> chunk_bits) == (col_c >> chunk_bits)
    tri = (same_chunk & ((col_c >= row_c) if reverse else (col_c <= row_c))).astype(bf16)
    lw_all = lw_ref[...]
    cum_all = _dot_exact_rhs_left(tri, lw_all)
    n_pairs = D_MODEL // W

    def pairs(x):
        return [x[g * C:(g + 1) * C, p * W:(p + 1) * W] for g in range(G) for p in range(n_pairs)]

    def per_chunk_rows(rows):
        return [rows[g][:, p * W:(p + 1) * W] for g in range(G) for p in range(n_pairs)]

    def each(fn, *lists):
        return [fn(*args) for args in zip(*lists)]

    def stack(x):
        return _stack_heads(x, first_head)

    def mm(x, y):
        return jnp.dot(x.astype(bf16), stack(y), preferred_element_type=f32)

    def mm2(x, y0, y1):
        return jnp.dot(x.astype(bf16), jnp.concatenate([stack(y0), stack(y1)], axis=1),
                       preferred_element_type=f32)

    r, k, v, kk, a = (pairs(ref[...]) for ref in (r_ref, k_ref, v_ref, kk_ref, a_ref))
    lw, cum = pairs(lw_all), pairs(cum_all)
    tot = per_chunk_rows([jnp.sum(lw_all[g * C:(g + 1) * C, :], axis=0, keepdims=True) for g in range(G)])
    ka = per_chunk_rows([kap_ref[...]] * G)
    kd = each(lambda k_, a_, ka_: k_ * (1.0 + (a_ - 1.0) * ka_), k, a, ka)
    b = each(lambda kk_, a_: kk_ * a_, kk, a)
    e_in = each(jnp.exp, cum)
    e_out = each(lambda c_: jnp.exp(-c_), cum)
    e_end = each(lambda t_, c_: jnp.exp(t_ - c_), tot, cum)
    kap = each(lambda kk_, c_, l_: kk_ * jnp.exp(c_ - l_), kk, cum, lw)
    mul = lambda x_, y_: x_ * y_
    kt, bt, rt, kh, bh = each(mul, kd, e_out), each(mul, b, e_out), each(mul, r, e_in), each(mul, kd, e_end), \
        each(mul, b, e_end)

    lhs = each(lambda x_, y_: jnp.concatenate([x_, y_], axis=0).astype(bf16), kap, rt)
    gk = each(lambda l_, y_: _dot_nt(l_, stack(y_)), lhs, kt)
    gb = each(lambda l_, y_: _dot_nt(l_, stack(y_)), lhs, bt)
    m_kk = each(lambda g_: jnp.where(before, g_[:C, :], 0.0), gk)
    n_kb = each(lambda g_: jnp.where(before, g_[:C, :], 0.0), gb)
    a_qk = each(lambda g_: jnp.where(before_eq, g_[C:, :], 0.0), gk)
    a_qb = each(lambda g_: jnp.where(before_eq, g_[C:, :], 0.0), gb)

    tinv = each(lambda n_: eye - jnp.where(merge_masks[0], n_, 0.0), n_kb)
    for off in merge_masks[1:]:
        tn = each(lambda t_, n_: mm(t_, jnp.where(off, n_, 0.0)), tinv, n_kb)
        tinv = each(lambda t_, tn_: t_ - mm(tn_, t_), tinv, tn)

    mv = each(mm, m_kk, v)
    tk = each(mm2, tinv, kap, mv)
    kt_c, w1 = each(lambda t_: t_[:, :W], tk), each(lambda t_: t_[:, W:], tk)
    qa = each(mm2, a_qb, kt_c, w1)
    av = each(mm, a_qk, v)
    q_eff = each(lambda rt_, qa_: rt_ - qa_[:, :W], rt, qa)
    o_intra = each(lambda av_, qa_: av_ - qa_[:, W:], av, qa)
    kb = each(lambda x_, y_, b_: _dot_tn(jnp.concatenate([x_, y_], axis=1).astype(bf16), b_.astype(bf16)),
              kt_c, w1, bh)
    vk = each(lambda v_, kh_: _dot_tn(v_.astype(bf16), kh_.astype(bf16)), v, kh)
    phi = each(lambda t_, kb_: jnp.where(same_head, jnp.where(diag_w, jnp.exp(t_), 0.0) - kb_[:W, :], 0.0),
               tot, kb)
    d_s = each(lambda vk_, kb_: jnp.where(first_head, (vk_ - kb_[W:, :])[:RW_HEAD, :],
                                          (vk_ - kb_[W:, :])[RW_HEAD:, :]), vk, kb)
    state = [s_ref[:, p * W:(p + 1) * W] for p in range(n_pairs)]
    for g in (range(G - 1, -1, -1) if reverse else range(G)):
        sel = slice(g * n_pairs, (g + 1) * n_pairs)
        o = each(lambda q_, s_, oi_: _dot_nt(q_.astype(bf16), stack(s_)) + oi_, q_eff[sel], state, o_intra[sel])
        state = each(lambda s_, ph_, ds_: jnp.dot(s_.astype(bf16), ph_.astype(bf16),
                                                  preferred_element_type=f32) + ds_, state, phi[sel], d_s[sel])
        o_ref[g * C:(g + 1) * C, :] = jnp.concatenate(o, axis=1)
    s_ref[...] = jnp.concatenate(state, axis=1)


def _rw_post_kernel(of_ref, ob_ref, bn_ref, g_ref, x_ref, mod_ref, ng_ref, lw_ref, lb_ref, ones_ref,
                    wo_ref, o_ref):
    o = of_ref[...] + ob_ref[...]
    ones = ones_ref[...]
    inv_n = 1.0 / RW_HEAD
    dev = o - _head_sum(o, ones) * inv_n
    var = _head_sum(dev * dev, ones) * inv_n
    on = dev * lax.rsqrt(var + RW_LN_EPS) * lw_ref[...] + lb_ref[...] + bn_ref[...]
    y = jnp.dot((on * g_ref[...]).astype(bf16), wo_ref[...], preferred_element_type=f32)
    o_ref[...] = _residual(x_ref[...], y, ng_ref[1:2, :], mod_ref[2:3, :])


@functools.lru_cache(maxsize=None)
def _pair_ones():
    idx = np.arange(PAIR_W) // RW_HEAD
    return np.asarray(idx[:, None] == idx[None, :], dtype=bf16)


def _rwkv_layer(x, trunk, layer, mod, norm_g, p, s0_fwd, s0_bwd):
    nb, seq = trunk.batch, trunk.seq
    tl = _tile_rows(trunk, 256)
    x8 = x.reshape(nb, seq // 8, 8, D_MODEL)
    prev_spec, next_spec = _halo_specs(trunk, tl)
    lora_w, lora_a, lora_g = p["w1"].shape[-1], p["a1"].shape[-1], p["g1"].shape[-1]
    ones = jnp.asarray(_pair_ones())
    outs = pl.pallas_call(
        _rw_pre_kernel,
        grid=(nb, seq // tl),
        in_specs=[_x_spec(tl), prev_spec, next_spec, _mod_spec(trunk, layer), _ng_spec(layer),
                  _resident((6, D_MODEL)),
                  _resident((D_MODEL, D_MODEL)), _resident((D_MODEL, D_MODEL)), _resident((D_MODEL, D_MODEL)),
                  _resident((D_MODEL, lora_g)), _resident((lora_g, D_MODEL)),
                  _resident((2, D_MODEL)), _resident((2, D_MODEL, lora_w)), _resident((2, lora_w, D_MODEL)),
                  _resident((2, D_MODEL)), _resident((2, D_MODEL, lora_a)), _resident((2, lora_a, D_MODEL)),
                  _resident((1, D_MODEL)), _resident((1, D_MODEL)), _resident((1, D_MODEL)),
                  _resident((PAIR_W, PAIR_W))],
        out_specs=[_x_spec(tl)] * 10,
        out_shape=[jax.ShapeDtypeStruct(x.shape, f32)] * 10,
        compiler_params=_params("parallel", "parallel"),
    )(x, x8, x8, mod, norm_g, p["mix"], p["wr"], p["wk"], p["wv"], p["g1"], p["g2"],
      p["w0"], p["w1"], p["w2"], p["a0"], p["a1"], p["a2"], p["kk"], p["ka"], p["rk"], ones)
    r, k, v, kk, gate, lw_f, lw_b, a_f, a_b, bonus = outs

    def state_in(s):
        return s.transpose(0, 2, 1, 3).reshape(nb, RW_HEAD, D_MODEL)

    def state_out(s):
        return s.reshape(nb, RW_HEAD, RW_HEADS, RW_HEAD).transpose(0, 2, 1, 3)

    rows = SCAN_CHUNK * SCAN_GROUP
    n_chunks = seq // rows
    state_spec = pl.BlockSpec((None, RW_HEAD, D_MODEL), lambda b, c: (b, 0, 0))
    results = []
    for reverse, lw, a, s0 in ((False, lw_f, a_f, s0_fwd), (True, lw_b, a_b, s0_bwd)):
        if reverse:
            chunk_spec = pl.BlockSpec((None, rows, D_MODEL), lambda b, c: (b, n_chunks - 1 - c, 0))
        else:
            chunk_spec = pl.BlockSpec((None, rows, D_MODEL), lambda b, c: (b, c, 0))
        results.append(pl.pallas_call(
            functools.partial(_rw_scan_kernel, reverse=reverse),
            grid=(nb, n_chunks),
            in_specs=[chunk_spec] * 6 + [_resident((1, D_MODEL)), state_spec],
            out_specs=[chunk_spec, state_spec],
            out_shape=[jax.ShapeDtypeStruct(x.shape, f32),
                       jax.ShapeDtypeStruct((nb, RW_HEAD, D_MODEL), f32)],
            compiler_params=_params("parallel", "arbitrary"),
        )(r, k, v, kk, lw, a, p["ka"], state_in(s0)))
    (o_f, s_f), (o_b, s_b) = results

    y = pl.pallas_call(
        _rw_post_kernel,
        grid=(nb, seq // tl),
        in_specs=[_x_spec(tl)] * 5 + [_mod_spec(trunk, layer), _ng_spec(layer),
                                      _resident((1, D_MODEL)), _resident((1, D_MODEL)),
                                      _resident((PAIR_W, PAIR_W)), _resident((D_MODEL, D_MODEL))],
        out_specs=_x_spec(tl),
        out_shape=jax.ShapeDtypeStruct(x.shape, f32),
        compiler_params=_params("parallel", "parallel"),
    )(o_f, o_b, bonus, gate, x, mod, norm_g, p["lnx_w"], p["lnx_b"], ones, p["wo"])
    return y, state_out(s_f), state_out(s_b)


def kernel(x_prompt, x_sample, state_wkv_fwd, state_wkv_bwd, c, c_ctx, ada_w, ada_b, norm_g, ffn_w1, ffn_w3, ffn_w2, fn_w_out, fn_b_out, hy_w_in, hy_b_in, hy_conv_w, hy_conv_b, hy_f_w1, hy_f_b1, hy_f_freq, hy_f_w2, hy_f_b2, hy_f_w3, hy_d, hy_w_out, hy_b_out, rw_mix, rw_wr, rw_wk, rw_wv, rw_wo, rw_w0, rw_w1, rw_w2, rw_a0, rw_a1, rw_a2, rw_g1, rw_g2, rw_kk, rw_ka, rw_rk, rw_lnx_w, rw_lnx_b):
    n_ctx, n_dec = x_prompt.shape[0], x_sample.shape[0]
    assert 1 + n_dec <= MOD_ROWS
    trunks = (Trunk(n_ctx, x_prompt.shape[1], 0, False), Trunk(n_dec, x_sample.shape[1], 1, True))
    cond = jnp.zeros((MOD_ROWS, D_MODEL), f32).at[0].set(c_ctx).at[1:1 + n_dec].set(c)
    mod = _adaln(cond, ada_w, ada_b)

    n_rwkv = rw_w0.shape[0]
    zero_state = jnp.zeros((n_ctx, n_rwkv, RW_HEADS, RW_HEAD, RW_HEAD), f32)
    states = ((zero_state, zero_state), (state_wkv_fwd, state_wkv_bwd))
    xs = [x_prompt, x_sample]
    new_fwd, new_bwd = [], []
    for i in range(DEPTH):
        kind, j = i % N_MIXERS, i // N_MIXERS
        if kind == 0:
            w_out, b_out = fn_w_out[j].astype(bf16), fn_b_out[j][None]
            xs = [_fourier_layer(x, t, i, mod, norm_g, w_out, b_out) for x, t in zip(xs, trunks)]
        elif kind == 1:
            w1_pad = jnp.zeros((HY_EMB_PAD, HY_FILT), f32).at[:HY_EMB].set(hy_f_w1[j])
            p = dict(w_in=hy_w_in[j].astype(bf16), b_in=hy_b_in[j][None], conv_w=hy_conv_w[j],
                     conv_b=hy_conv_b[j][None], f_w1=w1_pad, f_b1=hy_f_b1[j][None], f_freq=hy_f_freq[j],
                     f_w2=hy_f_w2[j], f_b2=hy_f_b2[j][None], f_w3=hy_f_w3[j], d=hy_d[j][None],
                     w_out=hy_w_out[j].astype(bf16), b_out=hy_b_out[j][None])
            xs = [_hyena_layer(x, t, i, mod, norm_g, p) for x, t in zip(xs, trunks)]
        else:
            per_head = lambda t: t.reshape(1, D_MODEL)
            p = dict(mix=rw_mix[j], wr=rw_wr[j].astype(bf16), wk=rw_wk[j].astype(bf16),
                     wv=rw_wv[j].astype(bf16), wo=rw_wo[j].astype(bf16), w0=rw_w0[j],
                     w1=rw_w1[j].astype(bf16), w2=rw_w2[j].astype(bf16), a0=rw_a0[j],
                     a1=rw_a1[j].astype(bf16), a2=rw_a2[j].astype(bf16), g1=rw_g1[j].astype(bf16),
                     g2=rw_g2[j].astype(bf16), kk=per_head(rw_kk[j]), ka=per_head(rw_ka[j]),
                     rk=per_head(rw_rk[j]), lnx_w=rw_lnx_w[j][None], lnx_b=rw_lnx_b[j][None])
            outs = [_rwkv_layer(x, t, i, mod, norm_g, p, sf[:, j], sb[:, j])
                    for x, t, (sf, sb) in zip(xs, trunks, states)]
            xs = [o[0] for o in outs]
            new_fwd.append(outs[0][1])
            new_bwd.append(outs[0][2])
        w1, w3, w2 = ffn_w1[i].astype(bf16), ffn_w3[i].astype(bf16), ffn_w2[i].astype(bf16)
        xs = [_ffn(x, t, i, mod, norm_g, w1, w3, w2) for x, t in zip(xs, trunks)]
    return xs[0], xs[1], jnp.stack(new_fwd, axis=1), jnp.stack(new_bwd, axis=1)
```

```python
import functools
import math
from typing import NamedTuple

import numpy as np
import jax
import jax.numpy as jnp
from jax import lax
from jax.experimental import pallas as pl
from jax.experimental.pallas import tpu as pltpu

f32 = jnp.float32
bf16 = jnp.bfloat16

D_MODEL = 1024
DEPTH = 4
N_MIXERS = 3
D_FF = 2816
NORM_EPS = 1e-6
FN_GROUP_W = 256
HY_EMB = 33
HY_EMB_PAD = 128
HY_BANDS = 16
HY_FILT = 64
HY_TARGET = 1e-2
HY_FAST = 0.3
HY_SLOW = 1.5
RW_HEAD = 64
RW_HEADS = 16
RW_LN_EPS = 64e-5
SCAN_CHUNK = 64
SCAN_GROUP = 2
PAIR_W = 2 * RW_HEAD
assert SCAN_CHUNK == RW_HEAD, "the scan keeps (chunk, chunk) and (chunk, head) tiles in one lane layout"
MOD_ROWS = 8
VMEM_LIMIT = 56 * 1024 * 1024


class Trunk(NamedTuple):
    batch: int
    seq: int
    mod_base: int
    per_batch_mod: bool


def _params(*sem):
    return pltpu.CompilerParams(dimension_semantics=sem, vmem_limit_bytes=VMEM_LIMIT)


def _resident(shape):
    nd = len(shape)
    return pl.BlockSpec(shape, lambda *_: (0,) * nd, pipeline_mode=pl.Buffered(1))


def _dot(a, b):
    return jnp.dot(a.astype(bf16), b.astype(bf16), preferred_element_type=f32)


def _split(x):
    hi = x.astype(bf16)
    lo = (x - hi.astype(f32)).astype(bf16)
    return hi, lo


def _dot3(a, b):
    ah, al = _split(a)
    bh, bl = _split(b)
    d = functools.partial(jnp.dot, preferred_element_type=f32)
    return d(ah, bh) + (d(ah, bl) + d(al, bh))


def _dot_exact_rhs_left(a_exact, b):
    b0 = b.astype(bf16)
    r1 = b - b0.astype(f32)
    b1 = r1.astype(bf16)
    b2 = (r1 - b1.astype(f32)).astype(bf16)
    d = functools.partial(jnp.dot, preferred_element_type=f32)
    return d(a_exact, b0) + (d(a_exact, b1) + d(a_exact, b2))


def _sigmoid(x):
    return 1.0 / (1.0 + jnp.exp(-x))


def _norm_mod(x, g, shift, scale):
    ms = jnp.mean(x * x, axis=-1, keepdims=True)
    return (x * lax.rsqrt(ms + NORM_EPS) * g) * (1.0 + scale) + shift


def _residual(x, y, g, gate):
    ms = jnp.mean(y * y, axis=-1, keepdims=True)
    return x + gate * (y * lax.rsqrt(ms + NORM_EPS) * g)


def _shifted(h, prev_row, next_row):
    rows = h.shape[0]
    ridx = lax.broadcasted_iota(jnp.int32, h.shape, 0)
    h_prev = jnp.where(ridx == 0, prev_row, pltpu.roll(h, 1, 0))
    h_next = jnp.where(ridx == rows - 1, next_row, pltpu.roll(h, rows - 1, 0))
    return h_prev, h_next


def _tile_rows(trunk, cap):
    return min(trunk.seq, cap)


def _x_spec(tl):
    return pl.BlockSpec((None, tl, D_MODEL), lambda b, j: (b, j, 0))


def _halo_specs(trunk, tl):
    g = tl // 8
    last = trunk.seq // 8 - 1
    prev = pl.BlockSpec((None, None, 8, D_MODEL), lambda b, j: (b, jnp.maximum(j * g - 1, 0), 0, 0))
    nxt = pl.BlockSpec((None, None, 8, D_MODEL), lambda b, j: (b, jnp.minimum((j + 1) * g, last), 0, 0))
    return prev, nxt


def _mod_spec(trunk, layer):
    if trunk.per_batch_mod:
        return pl.BlockSpec((None, None, 6, D_MODEL), lambda b, *_: (layer, trunk.mod_base + b, 0, 0))
    return pl.BlockSpec((None, None, 6, D_MODEL), lambda b, *_: (layer, trunk.mod_base, 0, 0))


def _ng_spec(layer):
    return pl.BlockSpec((None, 4, D_MODEL), lambda *_: (layer, 0, 0))


def _mod_kernel(c_ref, w_ref, b_ref, o_ref):
    c = c_ref[...]
    o_ref[...] = _dot3(c * _sigmoid(c), w_ref[...]) + b_ref[...]


def _adaln(cond, ada_w, ada_b):
    tn = 1536
    out = pl.pallas_call(
        _mod_kernel,
        grid=(DEPTH, 6 * D_MODEL // tn),
        in_specs=[pl.BlockSpec((MOD_ROWS, D_MODEL), lambda l, j: (0, 0)),
                  pl.BlockSpec((None, D_MODEL, tn), lambda l, j: (l, 0, j)),
                  pl.BlockSpec((None, 1, tn), lambda l, j: (l, 0, j))],
        out_specs=pl.BlockSpec((None, MOD_ROWS, tn), lambda l, j: (l, 0, j)),
        out_shape=jax.ShapeDtypeStruct((DEPTH, MOD_ROWS, 6 * D_MODEL), f32),
        compiler_params=_params("arbitrary", "arbitrary"),
    )(cond, ada_w, ada_b.reshape(DEPTH, 1, 6 * D_MODEL))
    return out.reshape(DEPTH, MOD_ROWS, 6, D_MODEL)


def _ffn_kernel(x_ref, mod_ref, ng_ref, w1_ref, w3_ref, w2_ref, o_ref):
    x = x_ref[...]
    h = _norm_mod(x, ng_ref[2:3, :], mod_ref[3:4, :], mod_ref[4:5, :]).astype(bf16)
    a = jnp.dot(h, w1_ref[...], preferred_element_type=f32)
    b = jnp.dot(h, w3_ref[...], preferred_element_type=f32)
    gated = (a * _sigmoid(a) * b).astype(bf16)
    y = jnp.dot(gated, w2_ref[...], preferred_element_type=f32)
    o_ref[...] = _residual(x, y, ng_ref[3:4, :], mod_ref[5:6, :])


def _ffn(x, trunk, layer, mod, norm_g, w1, w3, w2):
    tl = _tile_rows(trunk, 512)
    return pl.pallas_call(
        _ffn_kernel,
        grid=(trunk.batch, trunk.seq // tl),
        in_specs=[_x_spec(tl), _mod_spec(trunk, layer), _ng_spec(layer),
                  _resident((D_MODEL, D_FF)), _resident((D_MODEL, D_FF)), _resident((D_FF, D_MODEL))],
        out_specs=_x_spec(tl),
        out_shape=jax.ShapeDtypeStruct(x.shape, f32),
        compiler_params=_params("parallel", "parallel"),
    )(x, mod, norm_g, w1, w3, w2)


@functools.lru_cache(maxsize=None)
def _channel_dft():
    w = FN_GROUP_W
    idx = np.arange(w)
    ang = 2.0 * np.pi * ((idx[:, None] * idx[None, :]) % w) / w
    out = np.zeros((D_MODEL, 2 * D_MODEL), np.float64)
    for g in range(D_MODEL // w):
        out[g * w:(g + 1) * w, g * w:(g + 1) * w] = np.cos(ang) / math.sqrt(w)
        out[g * w:(g + 1) * w, D_MODEL + g * w:D_MODEL + (g + 1) * w] = np.sin(ang) / math.sqrt(w)
    return out.astype(np.float32)


@functools.lru_cache(maxsize=None)
def _time_dft(n_rows, period, scale):
    idx = np.arange(n_rows)
    ang = 2.0 * np.pi * ((idx[:, None] * idx[None, :]) % period) / period
    return (np.concatenate([np.cos(ang), -np.sin(ang)], axis=1) * scale).astype(np.float32)


def _table_rows(cs_ref, tile, rows):
    return cs_ref[pl.ds(pl.multiple_of(tile * rows, rows), rows), :]


def _mxu_table(table):
    return jnp.asarray(table).astype(bf16)


def _fn_a_kernel(x_ref, mod_ref, ng_ref, w_ref, o_ref):
    h = _norm_mod(x_ref[...], ng_ref[0:1, :], mod_ref[0:1, :], mod_ref[1:2, :]).astype(bf16)
    ab = jnp.dot(h, w_ref[...], preferred_element_type=f32).astype(bf16)
    o_ref[0] = ab[:, :D_MODEL]
    o_ref[1] = ab[:, D_MODEL:]


def _fn_b_kernel(cs_ref, ab_ref, x_ref, mod_ref, ng_ref, wo_ref, bo_ref, o_ref):
    rows = _table_rows(cs_ref, pl.program_id(1), x_ref.shape[0])
    f = jnp.dot(rows, ab_ref[...], preferred_element_type=f32)
    y = jnp.dot(f.astype(bf16), wo_ref[...], preferred_element_type=f32) + bo_ref[...]
    o_ref[...] = _residual(x_ref[...], y, ng_ref[1:2, :], mod_ref[2:3, :])


def _fourier_layer(x, trunk, layer, mod, norm_g, w_out, b_out):
    nb, seq = trunk.batch, trunk.seq
    tl = _tile_rows(trunk, 512)
    ab = pl.pallas_call(
        _fn_a_kernel,
        grid=(nb, seq // tl),
        in_specs=[_x_spec(tl), _mod_spec(trunk, layer), _ng_spec(layer), _resident((D_MODEL, 2 * D_MODEL))],
        out_specs=pl.BlockSpec((None, 2, tl, D_MODEL), lambda b, j: (b, 0, j, 0)),
        out_shape=jax.ShapeDtypeStruct((nb, 2, seq, D_MODEL), bf16),
        compiler_params=_params("parallel", "parallel"),
    )(x, mod, norm_g, _mxu_table(_channel_dft()))

    table = _mxu_table(_time_dft(seq, seq, 1.0 / math.sqrt(seq)))
    return pl.pallas_call(
        _fn_b_kernel,
        grid=(nb, seq // tl),
        in_specs=[_resident((seq, 2 * seq)),
                  pl.BlockSpec((None, 2 * seq, D_MODEL), lambda b, i: (b, 0, 0), pipeline_mode=pl.Buffered(1)),
                  _x_spec(tl), _mod_spec(trunk, layer), _ng_spec(layer),
                  _resident((D_MODEL, D_MODEL)), _resident((1, D_MODEL))],
        out_specs=_x_spec(tl),
        out_shape=jax.ShapeDtypeStruct(x.shape, f32),
        compiler_params=_params("parallel", "parallel"),
    )(table, ab.reshape(nb, 2 * seq, D_MODEL), x, mod, norm_g, w_out, b_out)


@functools.lru_cache(maxsize=None)
def _hyena_features(seq):
    t = np.linspace(0.0, 1.0, seq)[:, None]
    ang = 2.0 * np.pi * np.arange(seq)[:, None] / seq
    bands = np.linspace(1e-4, HY_BANDS - 1, HY_BANDS)[None]
    z = np.concatenate([t, np.cos(bands * ang), -np.sin(bands * ang)], axis=-1)
    out = np.zeros((seq, HY_EMB_PAD), np.float32)
    out[:, :HY_EMB] = z
    return out


@functools.lru_cache(maxsize=None)
def _hyena_decay_rates():
    d = np.linspace(math.log(HY_TARGET) / HY_FAST, math.log(HY_TARGET) / HY_SLOW, D_MODEL)
    return np.abs(d)[None].astype(np.float32)


def _hy_in_kernel(x_ref, xp_ref, xn_ref, mod_ref, ng_ref, w_ref, b_ref, cw_ref, cb_ref, x0_ref, z_ref, zb_ref):
    j = pl.program_id(1)
    g, sh, sc = ng_ref[0:1, :], mod_ref[0:1, :], mod_ref[1:2, :]
    tl = x_ref.shape[0]
    rows = jnp.concatenate([x_ref[...], xp_ref[...], xn_ref[...]], axis=0)
    u_all = jnp.dot(_norm_mod(rows, g, sh, sc).astype(bf16), w_ref[...], preferred_element_type=f32) + b_ref[...]
    u = u_all[:tl, :]
    prev_row = jnp.where(j == 0, 0.0, u_all[tl + 7:tl + 8, :])
    next_row = jnp.where(j == pl.num_programs(1) - 1, 0.0, u_all[tl + 8:tl + 9, :])
    u_prev, u_next = _shifted(u, prev_row, next_row)
    uc = u_prev * cw_ref[0:1, :] + u * cw_ref[1:2, :] + u_next * cw_ref[2:3, :] + cb_ref[...]
    x0_ref[...] = uc[:, :D_MODEL]
    z = uc[:, 2 * D_MODEL:] * uc[:, D_MODEL:2 * D_MODEL]
    z_ref[...] = z
    zb_ref[...] = z.astype(bf16)


def _hy_filter_kernel(feat_ref, w1_ref, b1_ref, fq_ref, w2_ref, b2_ref, w3f_ref, w3b_ref, rate_ref,
                      cs_ref, o_ref, *, seq):
    hdn = jnp.sin(fq_ref[0:1, :] * (_dot3(feat_ref[...], w1_ref[...]) + b1_ref[...]))
    hdn = jnp.sin(fq_ref[1:2, :] * (_dot3(hdn, w2_ref[...]) + b2_ref[...]))
    rows = lax.broadcasted_iota(jnp.int32, (seq, 1), 0)
    t = rows.astype(f32) * (1.0 / (seq - 1))
    win = jnp.exp(-t * rate_ref[...])
    k_fwd = _dot3(hdn, w3f_ref[...]) * win
    k_bwd = _dot3(hdn, w3b_ref[...]) * win
    both = k_fwd + k_bwd
    diff = jnp.where(rows == 0, both, k_fwd - k_bwd)
    k_re = jnp.dot(cs_ref[:, :seq], both.astype(bf16), preferred_element_type=f32)
    k_im = jnp.dot(cs_ref[:, seq:], diff.astype(bf16), preferred_element_type=f32)
    sign = (1 - 2 * (rows & 1)).astype(f32)
    k_nyq = jnp.sum(both * sign, axis=0, keepdims=True)
    wgt = jnp.where(rows == 0, 0.5 / seq, 1.0 / seq)
    o_ref[0] = k_re * wgt
    o_ref[1] = jnp.where(rows == 0, k_nyq, k_im) * wgt


def _alternating_sign(first, rows):
    t = first + lax.broadcasted_iota(jnp.int32, (rows, 1), 0)
    return (1 - 2 * (t & 1)).astype(f32)


def _hy_fwd_kernel(cs_ref, zb_ref, ks_ref, o_ref, nyq_ref, *, seq):
    i = pl.program_id(1)
    tf = o_ref.shape[1]
    rows = _table_rows(cs_ref, i, tf)
    zb = zb_ref[...]
    zr = jnp.dot(rows[:, :seq], zb, preferred_element_type=f32)
    zi = jnp.dot(rows[:, seq:], zb, preferred_element_type=f32)

    @pl.when(i == 0)
    def _():
        nyq_ref[...] = jnp.sum(zb.astype(f32) * _alternating_sign(0, seq), axis=0, keepdims=True)

    @pl.when(i != 0)
    def _():
        nyq_ref[...] = jnp.zeros_like(nyq_ref)

    kr, ki = ks_ref[0], ks_ref[1]
    packed = (i * tf + lax.broadcasted_iota(jnp.int32, (tf, 1), 0)) == 0
    o_ref[0] = (zr * kr - zi * ki).astype(bf16)
    o_ref[1] = jnp.where(packed, nyq_ref[...] * ki, zr * ki + zi * kr).astype(bf16)


def _hy_inv_kernel(cs_ref, ys_ref, z_ref, x0_ref, x_ref, mod_ref, ng_ref, d_ref, wo_ref, bo_ref, o_ref, *, seq):
    i = pl.program_id(1)
    tm = x_ref.shape[0]
    conv = jnp.dot(_table_rows(cs_ref, i, tm), ys_ref[...], preferred_element_type=f32)
    conv = conv + _alternating_sign(i * tm, tm) * ys_ref[seq:seq + 1, :].astype(f32)
    y = conv + z_ref[...] * d_ref[...]
    out = jnp.dot((y * x0_ref[...]).astype(bf16), wo_ref[...], preferred_element_type=f32) + bo_ref[...]
    o_ref[...] = _residual(x_ref[...], out, ng_ref[1:2, :], mod_ref[2:3, :])


def _hyena_layer(x, trunk, layer, mod, norm_g, p):
    nb, seq = trunk.batch, trunk.seq
    tl = _tile_rows(trunk, 256)
    x8 = x.reshape(nb, seq // 8, 8, D_MODEL)
    prev_spec, next_spec = _halo_specs(trunk, tl)
    x0, z, zb = pl.pallas_call(
        _hy_in_kernel,
        grid=(nb, seq // tl),
        in_specs=[_x_spec(tl), prev_spec, next_spec, _mod_spec(trunk, layer), _ng_spec(layer),
                  _resident((D_MODEL, 3 * D_MODEL)), _resident((1, 3 * D_MODEL)),
                  _resident((3, 3 * D_MODEL)), _resident((1, 3 * D_MODEL))],
        out_specs=[_x_spec(tl)] * 3,
        out_shape=[jax.ShapeDtypeStruct(x.shape, f32)] * 2 + [jax.ShapeDtypeStruct(x.shape, bf16)],
        compiler_params=_params("parallel", "parallel"),
    )(x, x8, x8, mod, norm_g, p["w_in"], p["b_in"], p["conv_w"], p["conv_b"])

    table = _mxu_table(_time_dft(seq, 2 * seq, 1.0))
    table_spec = _resident((seq, 2 * seq))
    tn = 256
    nblk = D_MODEL // tn
    ks = pl.pallas_call(
        functools.partial(_hy_filter_kernel, seq=seq),
        grid=(nblk,),
        in_specs=[_resident((seq, HY_EMB_PAD)), _resident((HY_EMB_PAD, HY_FILT)), _resident((1, HY_FILT)),
                  _resident((2, HY_FILT)), _resident((HY_FILT, HY_FILT)), _resident((1, HY_FILT)),
                  pl.BlockSpec((HY_FILT, tn), lambda j: (0, j)),
                  pl.BlockSpec((HY_FILT, tn), lambda j: (0, nblk + j)),
                  pl.BlockSpec((1, tn), lambda j: (0, j)),
                  table_spec],
        out_specs=pl.BlockSpec((2, seq, tn), lambda j: (0, 0, j)),
        out_shape=jax.ShapeDtypeStruct((2, seq, D_MODEL), f32),
        compiler_params=_params("parallel"),
    )(jnp.asarray(_hyena_features(seq)), p["f_w1"], p["f_b1"], p["f_freq"], p["f_w2"], p["f_b2"],
      p["f_w3"], p["f_w3"], jnp.asarray(_hyena_decay_rates()), table)

    tf = _tile_rows(trunk, 512)
    whole_seq = lambda rows: pl.BlockSpec((None, rows, D_MODEL), lambda b, i: (b, 0, 0),
                                          pipeline_mode=pl.Buffered(1))
    ys = pl.pallas_call(
        functools.partial(_hy_fwd_kernel, seq=seq),
        grid=(nb, seq // tf),
        in_specs=[table_spec, whole_seq(seq), pl.BlockSpec((2, tf, D_MODEL), lambda b, i: (0, i, 0))],
        out_specs=pl.BlockSpec((None, 2, tf, D_MODEL), lambda b, i: (b, 0, i, 0)),
        out_shape=jax.ShapeDtypeStruct((nb, 2, seq, D_MODEL), bf16),
        scratch_shapes=[pltpu.VMEM((1, D_MODEL), f32)],
        compiler_params=_params("parallel", "arbitrary"),
    )(table, zb, ks)

    return pl.pallas_call(
        functools.partial(_hy_inv_kernel, seq=seq),
        grid=(nb, seq // tf),
        in_specs=[table_spec, whole_seq(2 * seq), _x_spec(tf), _x_spec(tf), _x_spec(tf),
                  _mod_spec(trunk, layer), _ng_spec(layer),
                  _resident((1, D_MODEL)), _resident((D_MODEL, D_MODEL)), _resident((1, D_MODEL))],
        out_specs=_x_spec(tf),
        out_shape=jax.ShapeDtypeStruct(x.shape, f32),
        compiler_params=_params("parallel", "parallel"),
    )(table, ys.reshape(nb, 2 * seq, D_MODEL), z, x0, x, mod, norm_g, p["d"], p["w_out"], p["b_out"])


def _softplus(y):
    return jnp.maximum(y, 0.0) + jnp.log(1.0 + jnp.exp(-jnp.abs(y)))


def _head_sum(t, ones_pair):
    hi, lo = _split(t)
    cols = []
    for p in range(D_MODEL // PAIR_W):
        sl = slice(p * PAIR_W, (p + 1) * PAIR_W)
        cols.append(jnp.dot(hi[:, sl], ones_pair, preferred_element_type=f32)
                    + jnp.dot(lo[:, sl], ones_pair, preferred_element_type=f32))
    return jnp.concatenate(cols, axis=1)


def _rw_pre_kernel(x_ref, xp_ref, xn_ref, mod_ref, ng_ref, mix_ref, wr_ref, wk_ref, wv_ref, g1_ref, g2_ref,
                   w0_ref, w1_ref, w2_ref, a0_ref, a1_ref, a2_ref, kkp_ref, kap_ref, rkp_ref, ones_ref,
                   r_ref, k_ref, v_ref, kk_ref, g_ref, lwf_ref, lwb_ref, af_ref, ab_ref, bn_ref):
    j = pl.program_id(1)
    g, sh, sc = ng_ref[0:1, :], mod_ref[0:1, :], mod_ref[1:2, :]
    h = _norm_mod(x_ref[...], g, sh, sc)
    prev_row = jnp.where(j == 0, 0.0, _norm_mod(xp_ref[...], g, sh, sc)[7:8, :])
    next_row = jnp.where(j == pl.num_programs(1) - 1, 0.0, _norm_mod(xn_ref[...], g, sh, sc)[0:1, :])
    h_prev, h_next = _shifted(h, prev_row, next_row)
    xx = 0.5 * (h_prev + h_next) - h

    def mixed(m):
        return (h + xx * mix_ref[m:m + 1, :]).astype(bf16)

    r = jnp.dot(mixed(0), wr_ref[...], preferred_element_type=f32)
    xw = mixed(1)
    k = jnp.dot(mixed(2), wk_ref[...], preferred_element_type=f32)
    v = jnp.dot(mixed(3), wv_ref[...], preferred_element_type=f32)
    xa = mixed(4)
    r_ref[...] = r
    k_ref[...] = k
    v_ref[...] = v
    gate = _sigmoid(jnp.dot(mixed(5), g1_ref[...], preferred_element_type=f32))
    g_ref[...] = jnp.dot(gate.astype(bf16), g2_ref[...], preferred_element_type=f32)
    ones = ones_ref[...]
    kk = k * kkp_ref[...]
    kk_ref[...] = kk * lax.rsqrt(jnp.maximum(_head_sum(kk * kk, ones), 1e-24))
    a_sum = None
    for dd, (lw_ref, a_ref) in enumerate(((lwf_ref, af_ref), (lwb_ref, ab_ref))):
        lora = jnp.tanh(jnp.dot(xw, w1_ref[dd], preferred_element_type=f32))
        wl = w0_ref[dd:dd + 1, :] + jnp.dot(lora.astype(bf16), w2_ref[dd], preferred_element_type=f32)
        w_log = -_softplus(-wl) - 0.5
        lw_ref[...] = -jnp.exp(w_log)
        al = jnp.dot(xa, a1_ref[dd], preferred_element_type=f32)
        a = _sigmoid(a0_ref[dd:dd + 1, :] + jnp.dot(al.astype(bf16), a2_ref[dd], preferred_element_type=f32))
        a_ref[...] = a
        a_sum = a if a_sum is None else a_sum + a
    kd_sum = k * (2.0 + (a_sum - 2.0) * kap_ref[...])
    bn_ref[...] = _head_sum(r * kd_sum * rkp_ref[...], ones) * v


def _stack_heads(x, first_head):
    return jnp.concatenate([jnp.where(first_head, x, 0.0), jnp.where(first_head, 0.0, x)], axis=0).astype(bf16)


def _dot_nt(a, b):
    return lax.dot_general(a, b, (((1,), (1,)), ((), ())), preferred_element_type=f32)


def _dot_tn(a, b):
    return lax.dot_general(a, b, (((0,), (0,)), ((), ())), preferred_element_type=f32)


def _rw_scan_kernel(r_ref, k_ref, v_ref, kk_ref, lw_ref, a_ref, kap_ref, s0_ref, o_ref, s_ref, *, reverse):
    c = pl.program_id(1)
    C, W, G = SCAN_CHUNK, PAIR_W, SCAN_GROUP

    @pl.when(c == 0)
    def _():
        s_ref[...] = s0_ref[...]

    row = lax.broadcasted_iota(jnp.int32, (C, W), 0)
    lane = lax.broadcasted_iota(jnp.int32, (C, W), 1)
    col = lane & (C - 1)
    first_head = lane < RW_HEAD
    before = (col > row) if reverse else (col < row)
    before_eq = before | (col == row)
    eye = (col == row).astype(f32)
    row_w = lax.broadcasted_iota(jnp.int32, (W, W), 0)
    lane_w = lax.broadcasted_iota(jnp.int32, (W, W), 1)
    same_head = (row_w >= RW_HEAD) == (lane_w >= RW_HEAD)
    diag_w = row_w == lane_w
    merge_masks = []
    s = 1
    while s < C:
        sh = s.bit_length() - 1
        same_pair = (row >> (sh + 1)) == (col >> (sh + 1))
        later, earlier = ((row >> sh) & 1, (col >> sh) & 1)
        merge_masks.append(same_pair & (((later == 0) & (earlier == 1)) if reverse
                                        else ((later == 1) & (earlier == 0))))
        s *= 2

    row_c = lax.broadcasted_iota(jnp.int32, (G * C, G * C), 0)
    col_c = lax.broadcasted_iota(jnp.int32, (G * C, G * C), 1)
    chunk_bits = C.bit_length() - 1
    same_chunk = (row_c >> chunk_bits) == (col_c >> chunk_bits)
    tri = (same_chunk & ((col_c >= row_c) if reverse else (col_c <= row_c))).astype(bf16)
    lw_all = lw_ref[...]
    cum_all = _dot_exact_rhs_left(tri, lw_all)
    n_pairs = D_MODEL // W

    def pairs(x):
        return [x[g * C:(g + 1) * C, p * W:(p + 1) * W] for g in range(G) for p in range(n_pairs)]

    def per_chunk_rows(rows):
        return [rows[g][:, p * W:(p + 1) * W] for g in range(G) for p in range(n_pairs)]

    def each(fn, *lists):
        return [fn(*args) for args in zip(*lists)]

    def stack(x):
        return _stack_heads(x, first_head)

    def mm(x, y):
        return jnp.dot(x.astype(bf16), stack(y), preferred_element_type=f32)

    def mm2(x, y0, y1):
        return jnp.dot(x.astype(bf16), jnp.concatenate([stack(y0), stack(y1)], axis=1),
                       preferred_element_type=f32)

    r, k, v, kk, a = (pairs(ref[...]) for ref in (r_ref, k_ref, v_ref, kk_ref, a_ref))
    lw, cum = pairs(lw_all), pairs(cum_all)
    tot = per_chunk_rows([jnp.sum(lw_all[g * C:(g + 1) * C, :], axis=0, keepdims=True) for g in range(G)])
    ka = per_chunk_rows([kap_ref[...]] * G)
    kd = each(lambda k_, a_, ka_: k_ * (1.0 + (a_ - 1.0) * ka_), k, a, ka)
    b = each(lambda kk_, a_: kk_ * a_, kk, a)
    e_in = each(jnp.exp, cum)
    e_out = each(lambda c_: jnp.exp(-c_), cum)
    e_end = each(lambda t_, c_: jnp.exp(t_ - c_), tot, cum)
    kap = each(lambda kk_, c_, l_: kk_ * jnp.exp(c_ - l_), kk, cum, lw)
    mul = lambda x_, y_: x_ * y_
    kt, bt, rt, kh, bh = each(mul, kd, e_out), each(mul, b, e_out), each(mul, r, e_in), each(mul, kd, e_end), \
        each(mul, b, e_end)

    lhs = each(lambda x_, y_: jnp.concatenate([x_, y_], axis=0).astype(bf16), kap, rt)
    gram = each(lambda l_, y0_, y1_: _dot_nt(l_, jnp.concatenate([stack(y0_), stack(y1_)], axis=0)), lhs, kt, bt)
    m_kk = each(lambda g_: jnp.where(before, g_[:C, :W], 0.0), gram)
    n_kb = each(lambda g_: jnp.where(before, g_[:C, W:], 0.0), gram)
    a_qk = each(lambda g_: jnp.where(before_eq, g_[C:, :W], 0.0), gram)
    a_qb = each(lambda g_: jnp.where(before_eq, g_[C:, W:], 0.0), gram)

    tinv = each(lambda n_: eye - jnp.where(merge_masks[0], n_, 0.0), n_kb)
    for off in merge_masks[1:]:
        tn = each(lambda t_, n_: mm(t_, jnp.where(off, n_, 0.0)), tinv, n_kb)
        tinv = each(lambda t_, tn_: t_ - mm(tn_, t_), tinv, tn)

    mv = each(mm, m_kk, v)
    tk = each(mm2, tinv, kap, mv)
    kt_c, w1 = each(lambda t_: t_[:, :W], tk), each(lambda t_: t_[:, W:], tk)
    qa = each(mm2, a_qb, kt_c, w1)
    av = each(mm, a_qk, v)
    q_eff = each(lambda rt_, qa_: rt_ - qa_[:, :W], rt, qa)
    o_intra = each(lambda av_, qa_: av_ - qa_[:, W:], av, qa)
    kb = each(lambda x_, b_: _dot_tn(x_.astype(bf16), b_.astype(bf16)), kt_c, bh)
    phi = each(lambda t_, kb_: jnp.where(same_head, jnp.where(diag_w, jnp.exp(t_), 0.0) - kb_, 0.0), tot, kb)
    ds_full = each(lambda v_, w_, kh_, bh_: _dot_tn(jnp.concatenate([v_, -w_], axis=0).astype(bf16),
                                                    jnp.concatenate([kh_, bh_], axis=0).astype(bf16)),
                   v, w1, kh, bh)
    d_s = each(lambda d_: jnp.where(first_head, d_[:RW_HEAD, :], d_[RW_HEAD:, :]), ds_full)
    state = [s_ref[:, p * W:(p + 1) * W] for p in range(n_pairs)]
    for g in (range(G - 1, -1, -1) if reverse else range(G)):
        sel = slice(g * n_pairs, (g + 1) * n_pairs)
        o = each(lambda q_, s_, oi_: _dot_nt(q_.astype(bf16), stack(s_)) + oi_, q_eff[sel], state, o_intra[sel])
        state = each(lambda s_, ph_, ds_: jnp.dot(s_.astype(bf16), ph_.astype(bf16),
                                                  preferred_element_type=f32) + ds_, state, phi[sel], d_s[sel])
        o_ref[g * C:(g + 1) * C, :] = jnp.concatenate(o, axis=1)
    s_ref[...] = jnp.concatenate(state, axis=1)


def _rw_post_kernel(of_ref, ob_ref, bn_ref, g_ref, x_ref, mod_ref, ng_ref, lw_ref, lb_ref, ones_ref,
                    wo_ref, o_ref):
    o = of_ref[...] + ob_ref[...]
    ones = ones_ref[...]
    inv_n = 1.0 / RW_HEAD
    dev = o - _head_sum(o, ones) * inv_n
    var = _head_sum(dev * dev, ones) * inv_n
    on = dev * lax.rsqrt(var + RW_LN_EPS) * lw_ref[...] + lb_ref[...] + bn_ref[...]
    y = jnp.dot((on * g_ref[...]).astype(bf16), wo_ref[...], preferred_element_type=f32)
    o_ref[...] = _residual(x_ref[...], y, ng_ref[1:2, :], mod_ref[2:3, :])


@functools.lru_cache(maxsize=None)
def _pair_ones():
    idx = np.arange(PAIR_W) // RW_HEAD
    return np.asarray(idx[:, None] == idx[None, :], dtype=bf16)


def _rwkv_layer(x, trunk, layer, mod, norm_g, p, s0_fwd, s0_bwd):
    nb, seq = trunk.batch, trunk.seq
    tl = _tile_rows(trunk, 256)
    x8 = x.reshape(nb, seq // 8, 8, D_MODEL)
    prev_spec, next_spec = _halo_specs(trunk, tl)
    lora_w, lora_a, lora_g = p["w1"].shape[-1], p["a1"].shape[-1], p["g1"].shape[-1]
    ones = jnp.asarray(_pair_ones())
    outs = pl.pallas_call(
        _rw_pre_kernel,
        grid=(nb, seq // tl),
        in_specs=[_x_spec(tl), prev_spec, next_spec, _mod_spec(trunk, layer), _ng_spec(layer),
                  _resident((6, D_MODEL)),
                  _resident((D_MODEL, D_MODEL)), _resident((D_MODEL, D_MODEL)), _resident((D_MODEL, D_MODEL)),
                  _resident((D_MODEL, lora_g)), _resident((lora_g, D_MODEL)),
                  _resident((2, D_MODEL)), _resident((2, D_MODEL, lora_w)), _resident((2, lora_w, D_MODEL)),
                  _resident((2, D_MODEL)), _resident((2, D_MODEL, lora_a)), _resident((2, lora_a, D_MODEL)),
                  _resident((1, D_MODEL)), _resident((1, D_MODEL)), _resident((1, D_MODEL)),
                  _resident((PAIR_W, PAIR_W))],
        out_specs=[_x_spec(tl)] * 10,
        out_shape=[jax.ShapeDtypeStruct(x.shape, f32)] * 10,
        compiler_params=_params("parallel", "parallel"),
    )(x, x8, x8, mod, norm_g, p["mix"], p["wr"], p["wk"], p["wv"], p["g1"], p["g2"],
      p["w0"], p["w1"], p["w2"], p["a0"], p["a1"], p["a2"], p["kk"], p["ka"], p["rk"], ones)
    r, k, v, kk, gate, lw_f, lw_b, a_f, a_b, bonus = outs

    def state_in(s):
        return s.transpose(0, 2, 1, 3).reshape(nb, RW_HEAD, D_MODEL)

    def state_out(s):
        return s.reshape(nb, RW_HEAD, RW_HEADS, RW_HEAD).transpose(0, 2, 1, 3)

    rows = SCAN_CHUNK * SCAN_GROUP
    n_chunks = seq // rows
    state_spec = pl.BlockSpec((None, RW_HEAD, D_MODEL), lambda b, c: (b, 0, 0))
    results = []
    for reverse, lw, a, s0 in ((False, lw_f, a_f, s0_fwd), (True, lw_b, a_b, s0_bwd)):
        if reverse:
            chunk_spec = pl.BlockSpec((None, rows, D_MODEL), lambda b, c: (b, n_chunks - 1 - c, 0))
        else:
            chunk_spec = pl.BlockSpec((None, rows, D_MODEL), lambda b, c: (b, c, 0))
        results.append(pl.pallas_call(
            functools.partial(_rw_scan_kernel, reverse=reverse),
            grid=(nb, n_chunks),
            in_specs=[chunk_spec] * 6 + [_resident((1, D_MODEL)), state_spec],
            out_specs=[chunk_spec, state_spec],
            out_shape=[jax.ShapeDtypeStruct(x.shape, f32),
                       jax.ShapeDtypeStruct((nb, RW_HEAD, D_MODEL), f32)],
            compiler_params=_params("parallel", "arbitrary"),
        )(r, k, v, kk, lw, a, p["ka"], state_in(s0)))
    (o_f, s_f), (o_b, s_b) = results

    y = pl.pallas_call(
        _rw_post_kernel,
        grid=(nb, seq // tl),
        in_specs=[_x_spec(tl)] * 5 + [_mod_spec(trunk, layer), _ng_spec(layer),
                                      _resident((1, D_MODEL)), _resident((1, D_MODEL)),
                                      _resident((PAIR_W, PAIR_W)), _resident((D_MODEL, D_MODEL))],
        out_specs=_x_spec(tl),
        out_shape=jax.ShapeDtypeStruct(x.shape, f32),
        compiler_params=_params("parallel", "parallel"),
    )(o_f, o_b, bonus, gate, x, mod, norm_g, p["lnx_w"], p["lnx_b"], ones, p["wo"])
    return y, state_out(s_f), state_out(s_b)


def kernel(x_prompt, x_sample, state_wkv_fwd, state_wkv_bwd, c, c_ctx, ada_w, ada_b, norm_g, ffn_w1, ffn_w3, ffn_w2, fn_w_out, fn_b_out, hy_w_in, hy_b_in, hy_conv_w, hy_conv_b, hy_f_w1, hy_f_b1, hy_f_freq, hy_f_w2, hy_f_b2, hy_f_w3, hy_d, hy_w_out, hy_b_out, rw_mix, rw_wr, rw_wk, rw_wv, rw_wo, rw_w0, rw_w1, rw_w2, rw_a0, rw_a1, rw_a2, rw_g1, rw_g2, rw_kk, rw_ka, rw_rk, rw_lnx_w, rw_lnx_b):
    n_ctx, n_dec = x_prompt.shape[0], x_sample.shape[0]
    assert 1 + n_dec <= MOD_ROWS
    trunks = (Trunk(n_ctx, x_prompt.shape[1], 0, False), Trunk(n_dec, x_sample.shape[1], 1, True))
    cond = jnp.zeros((MOD_ROWS, D_MODEL), f32).at[0].set(c_ctx).at[1:1 + n_dec].set(c)
    mod = _adaln(cond, ada_w, ada_b)

    n_rwkv = rw_w0.shape[0]
    zero_state = jnp.zeros((n_ctx, n_rwkv, RW_HEADS, RW_HEAD, RW_HEAD), f32)
    states = ((zero_state, zero_state), (state_wkv_fwd, state_wkv_bwd))
    xs = [x_prompt, x_sample]
    new_fwd, new_bwd = [], []
    for i in range(DEPTH):
        kind, j = i % N_MIXERS, i // N_MIXERS
        if kind == 0:
            w_out, b_out = fn_w_out[j].astype(bf16), fn_b_out[j][None]
            xs = [_fourier_layer(x, t, i, mod, norm_g, w_out, b_out) for x, t in zip(xs, trunks)]
        elif kind == 1:
            w1_pad = jnp.zeros((HY_EMB_PAD, HY_FILT), f32).at[:HY_EMB].set(hy_f_w1[j])
            p = dict(w_in=hy_w_in[j].astype(bf16), b_in=hy_b_in[j][None], conv_w=hy_conv_w[j],
                     conv_b=hy_conv_b[j][None], f_w1=w1_pad, f_b1=hy_f_b1[j][None], f_freq=hy_f_freq[j],
                     f_w2=hy_f_w2[j], f_b2=hy_f_b2[j][None], f_w3=hy_f_w3[j], d=hy_d[j][None],
                     w_out=hy_w_out[j].astype(bf16), b_out=hy_b_out[j][None])
            xs = [_hyena_layer(x, t, i, mod, norm_g, p) for x, t in zip(xs, trunks)]
        else:
            per_head = lambda t: t.reshape(1, D_MODEL)
            p = dict(mix=rw_mix[j], wr=rw_wr[j].astype(bf16), wk=rw_wk[j].astype(bf16),
                     wv=rw_wv[j].astype(bf16), wo=rw_wo[j].astype(bf16), w0=rw_w0[j],
                     w1=rw_w1[j].astype(bf16), w2=rw_w2[j].astype(bf16), a0=rw_a0[j],
                     a1=rw_a1[j].astype(bf16), a2=rw_a2[j].astype(bf16), g1=rw_g1[j].astype(bf16),
                     g2=rw_g2[j].astype(bf16), kk=per_head(rw_kk[j]), ka=per_head(rw_ka[j]),
                     rk=per_head(rw_rk[j]), lnx_w=rw_lnx_w[j][None], lnx_b=rw_lnx_b[j][None])
            outs = [_rwkv_layer(x, t, i, mod, norm_g, p, sf[:, j], sb[:, j])
                    for x, t, (sf, sb) in zip(xs, trunks, states)]
            xs = [o[0] for o in outs]
            new_fwd.append(outs[0][1])
            new_bwd.append(outs[0][2])
        w1, w3, w2 = ffn_w1[i].astype(bf16), ffn_w3[i].astype(bf16), ffn_w2[i].astype(bf16)
        xs = [_ffn(x, t, i, mod, norm_g, w1, w3, w2) for x, t in zip(xs, trunks)]
    return xs[0], xs[1], jnp.stack(new_fwd, axis=1), jnp.stack(new_bwd, axis=1)
```

```python
import functools
import math
from typing import NamedTuple

import numpy as np
import jax
import jax.numpy as jnp
from jax import lax
from jax.experimental import pallas as pl
from jax.experimental.pallas import tpu as pltpu

f32 = jnp.float32
bf16 = jnp.bfloat16

D_MODEL = 1024
DEPTH = 4
N_MIXERS = 3
D_FF = 2816
NORM_EPS = 1e-6
FN_GROUP_W = 256
FN_HALO = 16
HY_EMB = 33
HY_EMB_PAD = 128
HY_BANDS = 16
HY_FILT = 64
HY_TARGET = 1e-2
HY_FAST = 0.3
HY_SLOW = 1.5
RW_HEAD = 64
RW_HEADS = 16
RW_LN_EPS = 64e-5
SCAN_CHUNK = 64
SCAN_GROUP = 2
PAIR_W = 2 * RW_HEAD
assert SCAN_CHUNK == RW_HEAD, "the scan keeps (chunk, chunk) and (chunk, head) tiles in one lane layout"
MOD_ROWS = 8
VMEM_LIMIT = 56 * 1024 * 1024


class Trunk(NamedTuple):
    batch: int
    seq: int
    mod_base: int
    per_batch_mod: bool


def _params(*sem):
    return pltpu.CompilerParams(dimension_semantics=sem, vmem_limit_bytes=VMEM_LIMIT)


def _resident(shape):
    nd = len(shape)
    return pl.BlockSpec(shape, lambda *_: (0,) * nd, pipeline_mode=pl.Buffered(1))


def _dot(a, b):
    return jnp.dot(a.astype(bf16), b.astype(bf16), preferred_element_type=f32)


def _split(x):
    hi = x.astype(bf16)
    lo = (x - hi.astype(f32)).astype(bf16)
    return hi, lo


def _dot3(a, b):
    ah, al = _split(a)
    bh, bl = _split(b)
    d = functools.partial(jnp.dot, preferred_element_type=f32)
    return d(ah, bh) + (d(ah, bl) + d(al, bh))


def _dot_exact_rhs_left(a_exact, b):
    b0 = b.astype(bf16)
    r1 = b - b0.astype(f32)
    b1 = r1.astype(bf16)
    b2 = (r1 - b1.astype(f32)).astype(bf16)
    d = functools.partial(jnp.dot, preferred_element_type=f32)
    return d(a_exact, b0) + (d(a_exact, b1) + d(a_exact, b2))


def _sigmoid(x):
    return 1.0 / (1.0 + jnp.exp(-x))


def _norm_mod(x, g, shift, scale):
    ms = jnp.mean(x * x, axis=-1, keepdims=True)
    return (x * lax.rsqrt(ms + NORM_EPS) * g) * (1.0 + scale) + shift


def _residual(x, y, g, gate):
    ms = jnp.mean(y * y, axis=-1, keepdims=True)
    return x + gate * (y * lax.rsqrt(ms + NORM_EPS) * g)


def _shifted(h, prev_row, next_row):
    rows = h.shape[0]
    ridx = lax.broadcasted_iota(jnp.int32, h.shape, 0)
    h_prev = jnp.where(ridx == 0, prev_row, pltpu.roll(h, 1, 0))
    h_next = jnp.where(ridx == rows - 1, next_row, pltpu.roll(h, rows - 1, 0))
    return h_prev, h_next


def _tile_rows(trunk, cap):
    return min(trunk.seq, cap)


def _x_spec(tl):
    return pl.BlockSpec((None, tl, D_MODEL), lambda b, j: (b, j, 0))


def _halo_specs(trunk, tl):
    g = tl // 8
    last = trunk.seq // 8 - 1
    prev = pl.BlockSpec((None, None, 8, D_MODEL), lambda b, j: (b, jnp.maximum(j * g - 1, 0), 0, 0))
    nxt = pl.BlockSpec((None, None, 8, D_MODEL), lambda b, j: (b, jnp.minimum((j + 1) * g, last), 0, 0))
    return prev, nxt


def _whole_seq_spec(rows, tiles_per_seq):
    mode = pl.Buffered(1) if tiles_per_seq > 2 else None
    return pl.BlockSpec((None, rows, D_MODEL), lambda b, i: (b, 0, 0), pipeline_mode=mode)


def _mod_spec(trunk, layer):
    if trunk.per_batch_mod:
        return pl.BlockSpec((None, None, 6, D_MODEL), lambda b, *_: (layer, trunk.mod_base + b, 0, 0))
    return pl.BlockSpec((None, None, 6, D_MODEL), lambda b, *_: (layer, trunk.mod_base, 0, 0))


def _ng_spec(layer):
    return pl.BlockSpec((None, 4, D_MODEL), lambda *_: (layer, 0, 0))


def _mod_kernel(c_ref, w_ref, b_ref, o_ref):
    c = c_ref[...]
    o_ref[...] = _dot3(c * _sigmoid(c), w_ref[...]) + b_ref[...]


def _adaln(cond, ada_w, ada_b):
    tn = 1536
    out = pl.pallas_call(
        _mod_kernel,
        grid=(DEPTH, 6 * D_MODEL // tn),
        in_specs=[pl.BlockSpec((MOD_ROWS, D_MODEL), lambda l, j: (0, 0)),
                  pl.BlockSpec((None, D_MODEL, tn), lambda l, j: (l, 0, j)),
                  pl.BlockSpec((None, 1, tn), lambda l, j: (l, 0, j))],
        out_specs=pl.BlockSpec((None, MOD_ROWS, tn), lambda l, j: (l, 0, j)),
        out_shape=jax.ShapeDtypeStruct((DEPTH, MOD_ROWS, 6 * D_MODEL), f32),
        compiler_params=_params("arbitrary", "arbitrary"),
    )(cond, ada_w, ada_b.reshape(DEPTH, 1, 6 * D_MODEL))
    return out.reshape(DEPTH, MOD_ROWS, 6, D_MODEL)


def _ffn_kernel(x_ref, mod_ref, ng_ref, w1_ref, w3_ref, w2_ref, o_ref):
    x = x_ref[...]
    h = _norm_mod(x, ng_ref[2:3, :], mod_ref[3:4, :], mod_ref[4:5, :]).astype(bf16)
    a = jnp.dot(h, w1_ref[...], preferred_element_type=f32)
    b = jnp.dot(h, w3_ref[...], preferred_element_type=f32)
    gated = (a * _sigmoid(a) * b).astype(bf16)
    y = jnp.dot(gated, w2_ref[...], preferred_element_type=f32)
    o_ref[...] = _residual(x, y, ng_ref[3:4, :], mod_ref[5:6, :])


def _ffn(x, trunk, layer, mod, norm_g, w1, w3, w2):
    tl = _tile_rows(trunk, 512)
    return pl.pallas_call(
        _ffn_kernel,
        grid=(trunk.batch, trunk.seq // tl),
        in_specs=[_x_spec(tl), _mod_spec(trunk, layer), _ng_spec(layer),
                  _resident((D_MODEL, D_FF)), _resident((D_MODEL, D_FF)), _resident((D_FF, D_MODEL))],
        out_specs=_x_spec(tl),
        out_shape=jax.ShapeDtypeStruct(x.shape, f32),
        compiler_params=_params("parallel", "parallel"),
    )(x, mod, norm_g, w1, w3, w2)


@functools.lru_cache(maxsize=None)
def _channel_dft():
    w = FN_GROUP_W
    idx = np.arange(w)
    ang = 2.0 * np.pi * ((idx[:, None] * idx[None, :]) % w) / w
    return (np.concatenate([np.cos(ang), np.sin(ang)], axis=1) / math.sqrt(w)).astype(np.float32)


@functools.lru_cache(maxsize=None)
def _time_dft(n_rows, period, scale):
    idx = np.arange(n_rows)
    ang = 2.0 * np.pi * ((idx[:, None] * idx[None, :]) % period) / period
    return (np.concatenate([np.cos(ang), -np.sin(ang)], axis=1) * scale).astype(np.float32)


def _table_rows(cs_ref, tile, rows):
    return cs_ref[pl.ds(pl.multiple_of(tile * rows, rows), rows), :]


def _mxu_table(table):
    return jnp.asarray(table).astype(bf16)


def _fn_a_kernel(x_ref, mod_ref, ng_ref, w_ref, o_ref):
    h = _norm_mod(x_ref[...], ng_ref[0:1, :], mod_ref[0:1, :], mod_ref[1:2, :]).astype(bf16)
    w = FN_GROUP_W
    for g in range(D_MODEL // w):
        ab = jnp.dot(h[:, g * w:(g + 1) * w], w_ref[...], preferred_element_type=f32).astype(bf16)
        o_ref[0, :, g * w:(g + 1) * w] = ab[:, :w]
        o_ref[1, :, g * w:(g + 1) * w] = ab[:, w:]


@functools.lru_cache(maxsize=None)
def _fourier_half_table(seq):
    k = np.arange(seq // 2 + FN_HALO)
    t = np.arange(seq)
    ang = 2.0 * np.pi * ((k[:, None] * t[None, :]) % seq) / seq
    return (np.concatenate([np.cos(ang), np.sin(ang)], axis=1) / math.sqrt(seq)).astype(np.float32)


@functools.lru_cache(maxsize=None)
def _row_reversal(tm):
    out = np.zeros((tm, tm + FN_HALO), np.float32)
    out[np.arange(tm), tm - np.arange(tm)] = 1.0
    return np.asarray(out, dtype=bf16)


def _fn_b_kernel(cs_ref, rev_ref, ab_ref, x_ref, mod_ref, ng_ref, wo_ref, bo_ref, o_ref, mirror_ref, *, seq):
    s = pl.program_id(1)
    half_tiles = pl.num_programs(1) // 2
    tm = x_ref.shape[0]
    src_rows = tm + FN_HALO

    def finish(f_bf16):
        y = jnp.dot(f_bf16, wo_ref[...], preferred_element_type=f32) + bo_ref[...]
        o_ref[...] = _residual(x_ref[...], y, ng_ref[1:2, :], mod_ref[2:3, :])

    @pl.when(s < half_tiles)
    def _():
        first = pl.multiple_of(s * tm, tm)
        rows = cs_ref[pl.ds(first, src_rows), :]
        p = jnp.dot(rows[:, :seq], ab_ref[:seq, :], preferred_element_type=f32)
        q = jnp.dot(rows[:, seq:], ab_ref[seq:, :], preferred_element_type=f32)
        mirror_ref[pl.ds(first, src_rows), :] = (p + q).astype(bf16)
        finish((p - q)[:tm, :].astype(bf16))

    @pl.when(s >= half_tiles)
    def _():
        first = pl.multiple_of((2 * half_tiles - 1 - s) * tm, tm)
        src = mirror_ref[pl.ds(first, src_rows), :]
        finish(jnp.dot(rev_ref[...], src, preferred_element_type=f32).astype(bf16))


def _fourier_layer(x, trunk, layer, mod, norm_g, w_out, b_out):
    nb, seq = trunk.batch, trunk.seq
    tl = _tile_rows(trunk, 512)
    ab = pl.pallas_call(
        _fn_a_kernel,
        grid=(nb, seq // tl),
        in_specs=[_x_spec(tl), _mod_spec(trunk, layer), _ng_spec(layer), _resident((FN_GROUP_W, 2 * FN_GROUP_W))],
        out_specs=pl.BlockSpec((None, 2, tl, D_MODEL), lambda b, j: (b, 0, j, 0)),
        out_shape=jax.ShapeDtypeStruct((nb, 2, seq, D_MODEL), bf16),
        compiler_params=_params("parallel", "parallel"),
    )(x, mod, norm_g, _mxu_table(_channel_dft()))

    tm = min(seq // 2, 512)
    half_rows = seq // 2 + FN_HALO
    return pl.pallas_call(
        functools.partial(_fn_b_kernel, seq=seq),
        grid=(nb, seq // tm),
        in_specs=[_resident((half_rows, 2 * seq)), _resident((tm, tm + FN_HALO)),
                  _whole_seq_spec(2 * seq, seq // tm),
                  _x_spec(tm), _mod_spec(trunk, layer), _ng_spec(layer),
                  _resident((D_MODEL, D_MODEL)), _resident((1, D_MODEL))],
        out_specs=_x_spec(tm),
        out_shape=jax.ShapeDtypeStruct(x.shape, f32),
        scratch_shapes=[pltpu.VMEM((half_rows, D_MODEL), bf16)],
        compiler_params=_params("parallel", "arbitrary"),
    )(_mxu_table(_fourier_half_table(seq)), jnp.asarray(_row_reversal(tm)), ab.reshape(nb, 2 * seq, D_MODEL),
      x, mod, norm_g, w_out, b_out)


@functools.lru_cache(maxsize=None)
def _hyena_features(seq):
    t = np.linspace(0.0, 1.0, seq)[:, None]
    ang = 2.0 * np.pi * np.arange(seq)[:, None] / seq
    bands = np.linspace(1e-4, HY_BANDS - 1, HY_BANDS)[None]
    z = np.concatenate([t, np.cos(bands * ang), -np.sin(bands * ang)], axis=-1)
    out = np.zeros((seq, HY_EMB_PAD), np.float32)
    out[:, :HY_EMB] = z
    return out


@functools.lru_cache(maxsize=None)
def _hyena_decay_rates():
    d = np.linspace(math.log(HY_TARGET) / HY_FAST, math.log(HY_TARGET) / HY_SLOW, D_MODEL)
    return np.abs(d)[None].astype(np.float32)


def _hy_in_kernel(x_ref, xp_ref, xn_ref, mod_ref, ng_ref, w_ref, b_ref, cw_ref, cb_ref, x0_ref, z_ref, zb_ref):
    j = pl.program_id(1)
    g, sh, sc = ng_ref[0:1, :], mod_ref[0:1, :], mod_ref[1:2, :]
    tl = x_ref.shape[0]
    rows = jnp.concatenate([x_ref[...], xp_ref[...], xn_ref[...]], axis=0)
    u_all = jnp.dot(_norm_mod(rows, g, sh, sc).astype(bf16), w_ref[...], preferred_element_type=f32) + b_ref[...]
    u = u_all[:tl, :]
    prev_row = jnp.where(j == 0, 0.0, u_all[tl + 7:tl + 8, :])
    next_row = jnp.where(j == pl.num_programs(1) - 1, 0.0, u_all[tl + 8:tl + 9, :])
    u_prev, u_next = _shifted(u, prev_row, next_row)
    uc = u_prev * cw_ref[0:1, :] + u * cw_ref[1:2, :] + u_next * cw_ref[2:3, :] + cb_ref[...]
    x0_ref[...] = uc[:, :D_MODEL]
    z = uc[:, 2 * D_MODEL:] * uc[:, D_MODEL:2 * D_MODEL]
    z_ref[...] = z
    zb_ref[...] = z.astype(bf16)


def _hy_filter_kernel(feat_ref, w1_ref, b1_ref, fq_ref, w2_ref, b2_ref, w3f_ref, w3b_ref, rate_ref,
                      cs_ref, o_ref, hdn_ref, *, seq):
    @pl.when(pl.program_id(0) == 0)
    def _():
        hdn = jnp.sin(fq_ref[0:1, :] * (_dot3(feat_ref[...], w1_ref[...]) + b1_ref[...]))
        hdn_ref[...] = jnp.sin(fq_ref[1:2, :] * (_dot3(hdn, w2_ref[...]) + b2_ref[...]))

    hdn = hdn_ref[...]
    rows = lax.broadcasted_iota(jnp.int32, (seq, 1), 0)
    t = rows.astype(f32) * (1.0 / (seq - 1))
    win = jnp.exp(-t * rate_ref[...])
    k_fwd = _dot3(hdn, w3f_ref[...]) * win
    k_bwd = _dot3(hdn, w3b_ref[...]) * win
    both = k_fwd + k_bwd
    diff = jnp.where(rows == 0, both, k_fwd - k_bwd)
    k_re = jnp.dot(cs_ref[:, :seq], both.astype(bf16), preferred_element_type=f32)
    k_im = jnp.dot(cs_ref[:, seq:], diff.astype(bf16), preferred_element_type=f32)
    sign = (1 - 2 * (rows & 1)).astype(f32)
    k_nyq = jnp.sum(both * sign, axis=0, keepdims=True)
    wgt = jnp.where(rows == 0, 0.5 / seq, 1.0 / seq)
    o_ref[0] = k_re * wgt
    o_ref[1] = jnp.where(rows == 0, k_nyq, k_im) * wgt


def _alternating_sign(first, rows):
    t = first + lax.broadcasted_iota(jnp.int32, (rows, 1), 0)
    return (1 - 2 * (t & 1)).astype(f32)


def _hy_fwd_kernel(cs_ref, zb_ref, ks_ref, o_ref, nyq_ref, *, seq):
    i = pl.program_id(1)
    tf = o_ref.shape[1]
    rows = _table_rows(cs_ref, i, tf)
    zb = zb_ref[...]
    zr = jnp.dot(rows[:, :seq], zb, preferred_element_type=f32)
    zi = jnp.dot(rows[:, seq:], zb, preferred_element_type=f32)

    @pl.when(i == 0)
    def _():
        nyq_ref[...] = jnp.sum(zb.astype(f32) * _alternating_sign(0, seq), axis=0, keepdims=True)

    @pl.when(i != 0)
    def _():
        nyq_ref[...] = jnp.zeros_like(nyq_ref)

    kr, ki = ks_ref[0], ks_ref[1]
    packed = (i * tf + lax.broadcasted_iota(jnp.int32, (tf, 1), 0)) == 0
    o_ref[0] = (zr * kr - zi * ki).astype(bf16)
    o_ref[1] = jnp.where(packed, nyq_ref[...] * ki, zr * ki + zi * kr).astype(bf16)


def _hy_inv_kernel(cs_ref, ys_ref, z_ref, x0_ref, x_ref, mod_ref, ng_ref, d_ref, wo_ref, bo_ref, o_ref, *, seq):
    i = pl.program_id(1)
    tm = x_ref.shape[0]
    conv = jnp.dot(_table_rows(cs_ref, i, tm), ys_ref[...], preferred_element_type=f32)
    conv = conv + _alternating_sign(i * tm, tm) * ys_ref[seq:seq + 1, :].astype(f32)
    y = conv + z_ref[...] * d_ref[...]
    out = jnp.dot((y * x0_ref[...]).astype(bf16), wo_ref[...], preferred_element_type=f32) + bo_ref[...]
    o_ref[...] = _residual(x_ref[...], out, ng_ref[1:2, :], mod_ref[2:3, :])


def _hyena_layer(x, trunk, layer, mod, norm_g, p):
    nb, seq = trunk.batch, trunk.seq
    tl = _tile_rows(trunk, 256)
    x8 = x.reshape(nb, seq // 8, 8, D_MODEL)
    prev_spec, next_spec = _halo_specs(trunk, tl)
    x0, z, zb = pl.pallas_call(
        _hy_in_kernel,
        grid=(nb, seq // tl),
        in_specs=[_x_spec(tl), prev_spec, next_spec, _mod_spec(trunk, layer), _ng_spec(layer),
                  _resident((D_MODEL, 3 * D_MODEL)), _resident((1, 3 * D_MODEL)),
                  _resident((3, 3 * D_MODEL)), _resident((1, 3 * D_MODEL))],
        out_specs=[_x_spec(tl)] * 3,
        out_shape=[jax.ShapeDtypeStruct(x.shape, f32)] * 2 + [jax.ShapeDtypeStruct(x.shape, bf16)],
        compiler_params=_params("parallel", "parallel"),
    )(x, x8, x8, mod, norm_g, p["w_in"], p["b_in"], p["conv_w"], p["conv_b"])

    table = _mxu_table(_time_dft(seq, 2 * seq, 1.0))
    table_spec = _resident((seq, 2 * seq))
    tn = 256
    nblk = D_MODEL // tn
    ks = pl.pallas_call(
        functools.partial(_hy_filter_kernel, seq=seq),
        grid=(nblk,),
        in_specs=[_resident((seq, HY_EMB_PAD)), _resident((HY_EMB_PAD, HY_FILT)), _resident((1, HY_FILT)),
                  _resident((2, HY_FILT)), _resident((HY_FILT, HY_FILT)), _resident((1, HY_FILT)),
                  pl.BlockSpec((HY_FILT, tn), lambda j: (0, j)),
                  pl.BlockSpec((HY_FILT, tn), lambda j: (0, nblk + j)),
                  pl.BlockSpec((1, tn), lambda j: (0, j)),
                  table_spec],
        out_specs=pl.BlockSpec((2, seq, tn), lambda j: (0, 0, j)),
        out_shape=jax.ShapeDtypeStruct((2, seq, D_MODEL), f32),
        scratch_shapes=[pltpu.VMEM((seq, HY_FILT), f32)],
        compiler_params=_params("arbitrary"),
    )(jnp.asarray(_hyena_features(seq)), p["f_w1"], p["f_b1"], p["f_freq"], p["f_w2"], p["f_b2"],
      p["f_w3"], p["f_w3"], jnp.asarray(_hyena_decay_rates()), table)

    tf = _tile_rows(trunk, 512)
    whole_seq = lambda rows: _whole_seq_spec(rows, seq // tf)
    ys = pl.pallas_call(
        functools.partial(_hy_fwd_kernel, seq=seq),
        grid=(nb, seq // tf),
        in_specs=[table_spec, whole_seq(seq), pl.BlockSpec((2, tf, D_MODEL), lambda b, i: (0, i, 0))],
        out_specs=pl.BlockSpec((None, 2, tf, D_MODEL), lambda b, i: (b, 0, i, 0)),
        out_shape=jax.ShapeDtypeStruct((nb, 2, seq, D_MODEL), bf16),
        scratch_shapes=[pltpu.VMEM((1, D_MODEL), f32)],
        compiler_params=_params("parallel", "arbitrary"),
    )(table, zb, ks)

    return pl.pallas_call(
        functools.partial(_hy_inv_kernel, seq=seq),
        grid=(nb, seq // tf),
        in_specs=[table_spec, whole_seq(2 * seq), _x_spec(tf), _x_spec(tf), _x_spec(tf),
                  _mod_spec(trunk, layer), _ng_spec(layer),
                  _resident((1, D_MODEL)), _resident((D_MODEL, D_MODEL)), _resident((1, D_MODEL))],
        out_specs=_x_spec(tf),
        out_shape=jax.ShapeDtypeStruct(x.shape, f32),
        compiler_params=_params("parallel", "parallel"),
    )(table, ys.reshape(nb, 2 * seq, D_MODEL), z, x0, x, mod, norm_g, p["d"], p["w_out"], p["b_out"])


def _head_sum(t, ones_pair):
    hi, lo = _split(t)
    cols = []
    for p in range(D_MODEL // PAIR_W):
        sl = slice(p * PAIR_W, (p + 1) * PAIR_W)
        cols.append(jnp.dot(hi[:, sl], ones_pair, preferred_element_type=f32)
                    + jnp.dot(lo[:, sl], ones_pair, preferred_element_type=f32))
    return jnp.concatenate(cols, axis=1)


def _rw_pre_kernel(x_ref, xp_ref, xn_ref, mod_ref, ng_ref, mix_ref, wr_ref, wk_ref, wv_ref, g1_ref, g2_ref,
                   w0_ref, w1_ref, w2_ref, a0_ref, a1_ref, a2_ref, kkp_ref, kap_ref, rkp_ref, ones_ref,
                   r_ref, k_ref, v_ref, kk_ref, g_ref, lwf_ref, lwb_ref, af_ref, ab_ref, bn_ref):
    j = pl.program_id(1)
    g, sh, sc = ng_ref[0:1, :], mod_ref[0:1, :], mod_ref[1:2, :]
    h = _norm_mod(x_ref[...], g, sh, sc)
    prev_row = jnp.where(j == 0, 0.0, _norm_mod(xp_ref[...], g, sh, sc)[7:8, :])
    next_row = jnp.where(j == pl.num_programs(1) - 1, 0.0, _norm_mod(xn_ref[...], g, sh, sc)[0:1, :])
    h_prev, h_next = _shifted(h, prev_row, next_row)
    xx = 0.5 * (h_prev + h_next) - h

    def mixed(m):
        return (h + xx * mix_ref[m:m + 1, :]).astype(bf16)

    r = jnp.dot(mixed(0), wr_ref[...], preferred_element_type=f32)
    xw = mixed(1)
    k = jnp.dot(mixed(2), wk_ref[...], preferred_element_type=f32)
    v = jnp.dot(mixed(3), wv_ref[...], preferred_element_type=f32)
    xa = mixed(4)
    r_ref[...] = r
    k_ref[...] = k
    v_ref[...] = v
    gate = _sigmoid(jnp.dot(mixed(5), g1_ref[...], preferred_element_type=f32))
    g_ref[...] = jnp.dot(gate.astype(bf16), g2_ref[...], preferred_element_type=f32)
    ones = ones_ref[...]
    kk = k * kkp_ref[...]
    kk_ref[...] = kk * lax.rsqrt(jnp.maximum(_head_sum(kk * kk, ones), 1e-24))
    a_sum = None
    for dd, (lw_ref, a_ref) in enumerate(((lwf_ref, af_ref), (lwb_ref, ab_ref))):
        lora = jnp.tanh(jnp.dot(xw, w1_ref[dd], preferred_element_type=f32))
        wl = w0_ref[dd:dd + 1, :] + jnp.dot(lora.astype(bf16), w2_ref[dd], preferred_element_type=f32)
        lw_ref[...] = -math.exp(-0.5) * _sigmoid(wl)
        al = jnp.dot(xa, a1_ref[dd], preferred_element_type=f32)
        a = _sigmoid(a0_ref[dd:dd + 1, :] + jnp.dot(al.astype(bf16), a2_ref[dd], preferred_element_type=f32))
        a_ref[...] = a
        a_sum = a if a_sum is None else a_sum + a
    kd_sum = k * (2.0 + (a_sum - 2.0) * kap_ref[...])
    bn_ref[...] = _head_sum(r * kd_sum * rkp_ref[...], ones) * v


def _stack_heads(x, first_head):
    return jnp.concatenate([jnp.where(first_head, x, 0.0), jnp.where(first_head, 0.0, x)], axis=0).astype(bf16)


def _dot_nt(a, b):
    return lax.dot_general(a, b, (((1,), (1,)), ((), ())), preferred_element_type=f32)


def _dot_tn(a, b):
    return lax.dot_general(a, b, (((0,), (0,)), ((), ())), preferred_element_type=f32)


def _rw_scan_kernel(r_ref, k_ref, v_ref, kk_ref, lw_ref, a_ref, kap_ref, s0_ref, o_ref, s_ref, *, reverse):
    c = pl.program_id(1)
    C, W, G = SCAN_CHUNK, PAIR_W, SCAN_GROUP

    @pl.when(c == 0)
    def _():
        s_ref[...] = s0_ref[...]

    row = lax.broadcasted_iota(jnp.int32, (C, W), 0)
    lane = lax.broadcasted_iota(jnp.int32, (C, W), 1)
    col = lane & (C - 1)
    first_head = lane < RW_HEAD
    before = (col > row) if reverse else (col < row)
    before_eq = before | (col == row)
    eye = (col == row).astype(f32)
    row_w = lax.broadcasted_iota(jnp.int32, (W, W), 0)
    lane_w = lax.broadcasted_iota(jnp.int32, (W, W), 1)
    same_head = (row_w >= RW_HEAD) == (lane_w >= RW_HEAD)
    diag_w = row_w == lane_w
    merge_masks = []
    s = 1
    while s < C:
        sh = s.bit_length() - 1
        same_pair = (row >> (sh + 1)) == (col >> (sh + 1))
        later, earlier = ((row >> sh) & 1, (col >> sh) & 1)
        merge_masks.append(same_pair & (((later == 0) & (earlier == 1)) if reverse
                                        else ((later == 1) & (earlier == 0))))
        s *= 2

    row_c = lax.broadcasted_iota(jnp.int32, (G * C, G * C), 0)
    col_c = lax.broadcasted_iota(jnp.int32, (G * C, G * C), 1)
    chunk_bits = C.bit_length() - 1
    same_chunk = (row_c >> chunk_bits) == (col_c >> chunk_bits)
    tri = (same_chunk & ((col_c >= row_c) if reverse else (col_c <= row_c))).astype(bf16)
    lw_all = lw_ref[...]
    cum_all = _dot_exact_rhs_left(tri, lw_all)
    n_pairs = D_MODEL // W

    def pairs(x):
        return [x[g * C:(g + 1) * C, p * W:(p + 1) * W] for g in range(G) for p in range(n_pairs)]

    def per_chunk_rows(rows):
        return [rows[g][:, p * W:(p + 1) * W] for g in range(G) for p in range(n_pairs)]

    def each(fn, *lists):
        return [fn(*args) for args in zip(*lists)]

    def stack(x):
        return _stack_heads(x, first_head)

    def mm(x, y):
        return jnp.dot(x.astype(bf16), stack(y), preferred_element_type=f32)

    def mm2(x, y0, y1):
        return jnp.dot(x.astype(bf16), jnp.concatenate([stack(y0), stack(y1)], axis=1),
                       preferred_element_type=f32)

    r, k, v, kk, a = (pairs(ref[...]) for ref in (r_ref, k_ref, v_ref, kk_ref, a_ref))
    lw, cum = pairs(lw_all), pairs(cum_all)
    tot = per_chunk_rows([jnp.sum(lw_all[g * C:(g + 1) * C, :], axis=0, keepdims=True) for g in range(G)])
    ka = per_chunk_rows([kap_ref[...]] * G)
    kd = each(lambda k_, a_, ka_: k_ * (1.0 + (a_ - 1.0) * ka_), k, a, ka)
    b = each(lambda kk_, a_: kk_ * a_, kk, a)
    e_in = each(jnp.exp, cum)
    e_out = each(lambda c_: jnp.exp(-c_), cum)
    e_end = each(lambda t_, c_: jnp.exp(t_ - c_), tot, cum)
    kap = each(lambda kk_, c_, l_: kk_ * jnp.exp(c_ - l_), kk, cum, lw)
    mul = lambda x_, y_: x_ * y_
    kt, bt, rt, kh, bh = each(mul, kd, e_out), each(mul, b, e_out), each(mul, r, e_in), each(mul, kd, e_end), \
        each(mul, b, e_end)

    lhs = each(lambda x_, y_: jnp.concatenate([x_, y_], axis=0).astype(bf16), kap, rt)
    gram = each(lambda l_, y0_, y1_: _dot_nt(l_, jnp.concatenate([stack(y0_), stack(y1_)], axis=0)), lhs, kt, bt)
    m_kk = each(lambda g_: jnp.where(before, g_[:C, :W], 0.0), gram)
    n_kb = each(lambda g_: jnp.where(before, g_[:C, W:], 0.0), gram)
    a_qk = each(lambda g_: jnp.where(before_eq, g_[C:, :W], 0.0), gram)
    a_qb = each(lambda g_: jnp.where(before_eq, g_[C:, W:], 0.0), gram)

    tinv = each(lambda n_: eye - jnp.where(merge_masks[0], n_, 0.0), n_kb)
    for off in merge_masks[1:]:
        tn = each(lambda t_, n_: mm(t_, jnp.where(off, n_, 0.0)), tinv, n_kb)
        tinv = each(lambda t_, tn_: t_ - mm(tn_, t_), tinv, tn)

    mv = each(mm, m_kk, v)
    tk = each(mm2, tinv, kap, mv)
    kt_c, w1 = each(lambda t_: t_[:, :W], tk), each(lambda t_: t_[:, W:], tk)
    qa = each(mm2, a_qb, kt_c, w1)
    av = each(mm, a_qk, v)
    q_eff = each(lambda rt_, qa_: rt_ - qa_[:, :W], rt, qa)
    o_intra = each(lambda av_, qa_: av_ - qa_[:, W:], av, qa)
    kb = each(lambda x_, b_: _dot_tn(x_.astype(bf16), b_.astype(bf16)), kt_c, bh)
    phi = each(lambda t_, kb_: jnp.where(same_head, jnp.where(diag_w, jnp.exp(t_), 0.0) - kb_, 0.0), tot, kb)
    ds_full = each(lambda v_, w_, kh_, bh_: _dot_tn(jnp.concatenate([v_, -w_], axis=0).astype(bf16),
                                                    jnp.concatenate([kh_, bh_], axis=0).astype(bf16)),
                   v, w1, kh, bh)
    d_s = each(lambda d_: jnp.where(first_head, d_[:RW_HEAD, :], d_[RW_HEAD:, :]), ds_full)
    state = [s_ref[:, p * W:(p + 1) * W] for p in range(n_pairs)]
    for g in (range(G - 1, -1, -1) if reverse else range(G)):
        sel = slice(g * n_pairs, (g + 1) * n_pairs)
        o = each(lambda q_, s_, oi_: _dot_nt(q_.astype(bf16), stack(s_)) + oi_, q_eff[sel], state, o_intra[sel])
        state = each(lambda s_, ph_, ds_: jnp.dot(s_.astype(bf16), ph_.astype(bf16),
                                                  preferred_element_type=f32) + ds_, state, phi[sel], d_s[sel])
        o_ref[g * C:(g + 1) * C, :] = jnp.concatenate(o, axis=1)
    s_ref[...] = jnp.concatenate(state, axis=1)


def _rw_post_kernel(of_ref, ob_ref, bn_ref, g_ref, x_ref, mod_ref, ng_ref, lw_ref, lb_ref, ones_ref,
                    wo_ref, o_ref):
    o = of_ref[...] + ob_ref[...]
    ones = ones_ref[...]
    inv_n = 1.0 / RW_HEAD
    dev = o - _head_sum(o, ones) * inv_n
    var = _head_sum(dev * dev, ones) * inv_n
    on = dev * lax.rsqrt(var + RW_LN_EPS) * lw_ref[...] + lb_ref[...] + bn_ref[...]
    y = jnp.dot((on * g_ref[...]).astype(bf16), wo_ref[...], preferred_element_type=f32)
    o_ref[...] = _residual(x_ref[...], y, ng_ref[1:2, :], mod_ref[2:3, :])


@functools.lru_cache(maxsize=None)
def _pair_ones():
    idx = np.arange(PAIR_W) // RW_HEAD
    return np.asarray(idx[:, None] == idx[None, :], dtype=bf16)


def _rwkv_layer(x, trunk, layer, mod, norm_g, p, s0_fwd, s0_bwd):
    nb, seq = trunk.batch, trunk.seq
    tl = _tile_rows(trunk, 256)
    x8 = x.reshape(nb, seq // 8, 8, D_MODEL)
    prev_spec, next_spec = _halo_specs(trunk, tl)
    lora_w, lora_a, lora_g = p["w1"].shape[-1], p["a1"].shape[-1], p["g1"].shape[-1]
    ones = jnp.asarray(_pair_ones())
    outs = pl.pallas_call(
        _rw_pre_kernel,
        grid=(nb, seq // tl),
        in_specs=[_x_spec(tl), prev_spec, next_spec, _mod_spec(trunk, layer), _ng_spec(layer),
                  _resident((6, D_MODEL)),
                  _resident((D_MODEL, D_MODEL)), _resident((D_MODEL, D_MODEL)), _resident((D_MODEL, D_MODEL)),
                  _resident((D_MODEL, lora_g)), _resident((lora_g, D_MODEL)),
                  _resident((2, D_MODEL)), _resident((2, D_MODEL, lora_w)), _resident((2, lora_w, D_MODEL)),
                  _resident((2, D_MODEL)), _resident((2, D_MODEL, lora_a)), _resident((2, lora_a, D_MODEL)),
                  _resident((1, D_MODEL)), _resident((1, D_MODEL)), _resident((1, D_MODEL)),
                  _resident((PAIR_W, PAIR_W))],
        out_specs=[_x_spec(tl)] * 10,
        out_shape=[jax.ShapeDtypeStruct(x.shape, f32)] * 10,
        compiler_params=_params("parallel", "parallel"),
    )(x, x8, x8, mod, norm_g, p["mix"], p["wr"], p["wk"], p["wv"], p["g1"], p["g2"],
      p["w0"], p["w1"], p["w2"], p["a0"], p["a1"], p["a2"], p["kk"], p["ka"], p["rk"], ones)
    r, k, v, kk, gate, lw_f, lw_b, a_f, a_b, bonus = outs

    def state_in(s):
        return s.transpose(0, 2, 1, 3).reshape(nb, RW_HEAD, D_MODEL)

    def state_out(s):
        return s.reshape(nb, RW_HEAD, RW_HEADS, RW_HEAD).transpose(0, 2, 1, 3)

    rows = SCAN_CHUNK * SCAN_GROUP
    n_chunks = seq // rows
    state_spec = pl.BlockSpec((None, RW_HEAD, D_MODEL), lambda b, c: (b, 0, 0))
    results = []
    for reverse, lw, a, s0 in ((False, lw_f, a_f, s0_fwd), (True, lw_b, a_b, s0_bwd)):
        if reverse:
            chunk_spec = pl.BlockSpec((None, rows, D_MODEL), lambda b, c: (b, n_chunks - 1 - c, 0))
        else:
            chunk_spec = pl.BlockSpec((None, rows, D_MODEL), lambda b, c: (b, c, 0))
        results.append(pl.pallas_call(
            functools.partial(_rw_scan_kernel, reverse=reverse),
            grid=(nb, n_chunks),
            in_specs=[chunk_spec] * 6 + [_resident((1, D_MODEL)), state_spec],
            out_specs=[chunk_spec, state_spec],
            out_shape=[jax.ShapeDtypeStruct(x.shape, f32),
                       jax.ShapeDtypeStruct((nb, RW_HEAD, D_MODEL), f32)],
            compiler_params=_params("parallel", "arbitrary"),
        )(r, k, v, kk, lw, a, p["ka"], state_in(s0)))
    (o_f, s_f), (o_b, s_b) = results

    y = pl.pallas_call(
        _rw_post_kernel,
        grid=(nb, seq // tl),
        in_specs=[_x_spec(tl)] * 5 + [_mod_spec(trunk, layer), _ng_spec(layer),
                                      _resident((1, D_MODEL)), _resident((1, D_MODEL)),
                                      _resident((PAIR_W, PAIR_W)), _resident((D_MODEL, D_MODEL))],
        out_specs=_x_spec(tl),
        out_shape=jax.ShapeDtypeStruct(x.shape, f32),
        compiler_params=_params("parallel", "parallel"),
    )(o_f, o_b, bonus, gate, x, mod, norm_g, p["lnx_w"], p["lnx_b"], ones, p["wo"])
    return y, state_out(s_f), state_out(s_b)


def kernel(x_prompt, x_sample, state_wkv_fwd, state_wkv_bwd, c, c_ctx, ada_w, ada_b, norm_g, ffn_w1, ffn_w3, ffn_w2, fn_w_out, fn_b_out, hy_w_in, hy_b_in, hy_conv_w, hy_conv_b, hy_f_w1, hy_f_b1, hy_f_freq, hy_f_w2, hy_f_b2, hy_f_w3, hy_d, hy_w_out, hy_b_out, rw_mix, rw_wr, rw_wk, rw_wv, rw_wo, rw_w0, rw_w1, rw_w2, rw_a0, rw_a1, rw_a2, rw_g1, rw_g2, rw_kk, rw_ka, rw_rk, rw_lnx_w, rw_lnx_b):
    n_ctx, n_dec = x_prompt.shape[0], x_sample.shape[0]
    assert 1 + n_dec <= MOD_ROWS
    trunks = (Trunk(n_ctx, x_prompt.shape[1], 0, False), Trunk(n_dec, x_sample.shape[1], 1, True))
    cond = jnp.zeros((MOD_ROWS, D_MODEL), f32).at[0].set(c_ctx).at[1:1 + n_dec].set(c)
    mod = _adaln(cond, ada_w, ada_b)

    n_rwkv = rw_w0.shape[0]
    zero_state = jnp.zeros((n_ctx, n_rwkv, RW_HEADS, RW_HEAD, RW_HEAD), f32)
    states = ((zero_state, zero_state), (state_wkv_fwd, state_wkv_bwd))
    xs = [x_prompt, x_sample]
    new_fwd, new_bwd = [], []
    for i in range(DEPTH):
        kind, j = i % N_MIXERS, i // N_MIXERS
        if kind == 0:
            w_out, b_out = fn_w_out[j].astype(bf16), fn_b_out[j][None]
            xs = [_fourier_layer(x, t, i, mod, norm_g, w_out, b_out) for x, t in zip(xs, trunks)]
        elif kind == 1:
            w1_pad = jnp.zeros((HY_EMB_PAD, HY_FILT), f32).at[:HY_EMB].set(hy_f_w1[j])
            p = dict(w_in=hy_w_in[j].astype(bf16), b_in=hy_b_in[j][None], conv_w=hy_conv_w[j],
                     conv_b=hy_conv_b[j][None], f_w1=w1_pad, f_b1=hy_f_b1[j][None], f_freq=hy_f_freq[j],
                     f_w2=hy_f_w2[j], f_b2=hy_f_b2[j][None], f_w3=hy_f_w3[j], d=hy_d[j][None],
                     w_out=hy_w_out[j].astype(bf16), b_out=hy_b_out[j][None])
            xs = [_hyena_layer(x, t, i, mod, norm_g, p) for x, t in zip(xs, trunks)]
        else:
            per_head = lambda t: t.reshape(1, D_MODEL)
            p = dict(mix=rw_mix[j], wr=rw_wr[j].astype(bf16), wk=rw_wk[j].astype(bf16),
                     wv=rw_wv[j].astype(bf16), wo=rw_wo[j].astype(bf16), w0=rw_w0[j],
                     w1=rw_w1[j].astype(bf16), w2=rw_w2[j].astype(bf16), a0=rw_a0[j],
                     a1=rw_a1[j].astype(bf16), a2=rw_a2[j].astype(bf16), g1=rw_g1[j].astype(bf16),
                     g2=rw_g2[j].astype(bf16), kk=per_head(rw_kk[j]), ka=per_head(rw_ka[j]),
                     rk=per_head(rw_rk[j]), lnx_w=rw_lnx_w[j][None], lnx_b=rw_lnx_b[j][None])
            outs = [_rwkv_layer(x, t, i, mod, norm_g, p, sf[:, j], sb[:, j])
                    for x, t, (sf, sb) in zip(xs, trunks, states)]
            xs = [o[0] for o in outs]
            new_fwd.append(outs[0][1])
            new_bwd.append(outs[0][2])
        w1, w3, w2 = ffn_w1[i].astype(bf16), ffn_w3[i].astype(bf16), ffn_w2[i].astype(bf16)
        xs = [_ffn(x, t, i, mod, norm_g, w1, w3, w2) for x, t in zip(xs, trunks)]
    return xs[0], xs[1], jnp.stack(new_fwd, axis=1), jnp.stack(new_bwd, axis=1)
```

```python
import functools
import math
from typing import NamedTuple

import numpy as np
import jax
import jax.numpy as jnp
from jax import lax
from jax.experimental import pallas as pl
from jax.experimental.pallas import tpu as pltpu

f32 = jnp.float32
bf16 = jnp.bfloat16

D_MODEL = 1024
DEPTH = 4
N_MIXERS = 3
D_FF = 2816
NORM_EPS = 1e-6
FN_GROUP_W = 256
FN_HALO = 16
HY_EMB = 33
HY_EMB_PAD = 128
HY_BANDS = 16
HY_FILT = 64
HY_TARGET = 1e-2
HY_FAST = 0.3
HY_SLOW = 1.5
RW_HEAD = 64
RW_HEADS = 16
RW_LN_EPS = 64e-5
SCAN_CHUNK = 64
SCAN_GROUP = 2
PAIR_W = 2 * RW_HEAD
assert SCAN_CHUNK == RW_HEAD, "the scan keeps (chunk, chunk) and (chunk, head) tiles in one lane layout"
MOD_ROWS = 8
VMEM_LIMIT = 56 * 1024 * 1024


class Trunk(NamedTuple):
    batch: int
    seq: int
    mod_base: int
    per_batch_mod: bool


def _params(*sem):
    return pltpu.CompilerParams(dimension_semantics=sem, vmem_limit_bytes=VMEM_LIMIT)


def _resident(shape):
    nd = len(shape)
    return pl.BlockSpec(shape, lambda *_: (0,) * nd, pipeline_mode=pl.Buffered(1))


def _dot(a, b):
    return jnp.dot(a.astype(bf16), b.astype(bf16), preferred_element_type=f32)


def _split(x):
    hi = x.astype(bf16)
    lo = (x - hi.astype(f32)).astype(bf16)
    return hi, lo


def _dot3(a, b):
    ah, al = _split(a)
    bh, bl = _split(b)
    d = functools.partial(jnp.dot, preferred_element_type=f32)
    return d(ah, bh) + (d(ah, bl) + d(al, bh))


def _dot_exact_rhs_left(a_exact, b):
    b0 = b.astype(bf16)
    r1 = b - b0.astype(f32)
    b1 = r1.astype(bf16)
    b2 = (r1 - b1.astype(f32)).astype(bf16)
    d = functools.partial(jnp.dot, preferred_element_type=f32)
    return d(a_exact, b0) + (d(a_exact, b1) + d(a_exact, b2))


def _sigmoid(x):
    return 0.5 * jnp.tanh(0.5 * x) + 0.5


def _norm_mod(x, g, shift, scale):
    ms = jnp.mean(x * x, axis=-1, keepdims=True)
    return (x * lax.rsqrt(ms + NORM_EPS)) * (g * (1.0 + scale)) + shift


def _residual(x, y, g, gate):
    ms = jnp.mean(y * y, axis=-1, keepdims=True)
    return x + gate * (y * lax.rsqrt(ms + NORM_EPS) * g)


def _shifted(h, prev_row, next_row):
    rows = h.shape[0]
    ridx = lax.broadcasted_iota(jnp.int32, h.shape, 0)
    h_prev = jnp.where(ridx == 0, prev_row, pltpu.roll(h, 1, 0))
    h_next = jnp.where(ridx == rows - 1, next_row, pltpu.roll(h, rows - 1, 0))
    return h_prev, h_next


def _tile_rows(trunk, cap):
    return min(trunk.seq, cap)


def _x_spec(tl):
    return pl.BlockSpec((None, tl, D_MODEL), lambda b, j: (b, j, 0))


def _halo_specs(trunk, tl):
    g = tl // 8
    last = trunk.seq // 8 - 1
    prev = pl.BlockSpec((None, None, 8, D_MODEL), lambda b, j: (b, jnp.maximum(j * g - 1, 0), 0, 0))
    nxt = pl.BlockSpec((None, None, 8, D_MODEL), lambda b, j: (b, jnp.minimum((j + 1) * g, last), 0, 0))
    return prev, nxt


def _whole_seq_spec(rows, tiles_per_seq):
    mode = pl.Buffered(1) if tiles_per_seq > 2 else None
    return pl.BlockSpec((None, rows, D_MODEL), lambda b, i: (b, 0, 0), pipeline_mode=mode)


def _mod_spec(trunk, layer):
    if trunk.per_batch_mod:
        return pl.BlockSpec((None, None, 6, D_MODEL), lambda b, *_: (layer, trunk.mod_base + b, 0, 0))
    return pl.BlockSpec((None, None, 6, D_MODEL), lambda b, *_: (layer, trunk.mod_base, 0, 0))


def _ng_spec(layer):
    return pl.BlockSpec((None, 4, D_MODEL), lambda *_: (layer, 0, 0))


def _mod_kernel(c_ref, w_ref, b_ref, o_ref):
    c = c_ref[...]
    o_ref[...] = _dot3(c * _sigmoid(c), w_ref[...]) + b_ref[...]


def _adaln(cond, ada_w, ada_b):
    tn = 1536
    out = pl.pallas_call(
        _mod_kernel,
        grid=(DEPTH, 6 * D_MODEL // tn),
        in_specs=[pl.BlockSpec((MOD_ROWS, D_MODEL), lambda l, j: (0, 0)),
                  pl.BlockSpec((None, D_MODEL, tn), lambda l, j: (l, 0, j)),
                  pl.BlockSpec((None, 1, tn), lambda l, j: (l, 0, j))],
        out_specs=pl.BlockSpec((None, MOD_ROWS, tn), lambda l, j: (l, 0, j)),
        out_shape=jax.ShapeDtypeStruct((DEPTH, MOD_ROWS, 6 * D_MODEL), f32),
        compiler_params=_params("arbitrary", "arbitrary"),
    )(cond, ada_w, ada_b.reshape(DEPTH, 1, 6 * D_MODEL))
    return out.reshape(DEPTH, MOD_ROWS, 6, D_MODEL)


def _ffn_kernel(x_ref, mod_ref, ng_ref, w1_ref, w3_ref, w2_ref, o_ref):
    x = x_ref[...]
    h = _norm_mod(x, ng_ref[2:3, :], mod_ref[3:4, :], mod_ref[4:5, :]).astype(bf16)
    a = jnp.dot(h, w1_ref[...], preferred_element_type=f32)
    b = jnp.dot(h, w3_ref[...], preferred_element_type=f32)
    gated = (a * _sigmoid(a) * b).astype(bf16)
    y = jnp.dot(gated, w2_ref[...], preferred_element_type=f32)
    o_ref[...] = _residual(x, y, ng_ref[3:4, :], mod_ref[5:6, :])


def _merge_sequences(x, trunk):
    if trunk.per_batch_mod:
        return x, trunk
    return x.reshape(1, -1, D_MODEL), Trunk(1, trunk.batch * trunk.seq, trunk.mod_base, False)


def _ffn(x, trunk, layer, mod, norm_g, w1, w3, w2):
    shape = x.shape
    x, trunk = _merge_sequences(x, trunk)
    tl = _tile_rows(trunk, 512)
    return pl.pallas_call(
        _ffn_kernel,
        grid=(trunk.batch, trunk.seq // tl),
        in_specs=[_x_spec(tl), _mod_spec(trunk, layer), _ng_spec(layer),
                  _resident((D_MODEL, D_FF)), _resident((D_MODEL, D_FF)), _resident((D_FF, D_MODEL))],
        out_specs=_x_spec(tl),
        out_shape=jax.ShapeDtypeStruct(x.shape, f32),
        compiler_params=_params("parallel", "parallel"),
    )(x, mod, norm_g, w1, w3, w2).reshape(shape)


@functools.lru_cache(maxsize=None)
def _channel_dft():
    w = FN_GROUP_W
    idx = np.arange(w)
    ang = 2.0 * np.pi * ((idx[:, None] * idx[None, :]) % w) / w
    return (np.concatenate([np.cos(ang), np.sin(ang)], axis=1) / math.sqrt(w)).astype(np.float32)


@functools.lru_cache(maxsize=None)
def _time_dft(n_rows, period, scale):
    idx = np.arange(n_rows)
    ang = 2.0 * np.pi * ((idx[:, None] * idx[None, :]) % period) / period
    return (np.concatenate([np.cos(ang), -np.sin(ang)], axis=1) * scale).astype(np.float32)


def _table_rows(cs_ref, tile, rows):
    return cs_ref[pl.ds(pl.multiple_of(tile * rows, rows), rows), :]


def _mxu_table(table):
    return jnp.asarray(table).astype(bf16)


def _fn_a_kernel(x_ref, mod_ref, ng_ref, w_ref, a_ref, b_ref):
    h = _norm_mod(x_ref[...], ng_ref[0:1, :], mod_ref[0:1, :], mod_ref[1:2, :]).astype(bf16)
    w = FN_GROUP_W
    for g in range(D_MODEL // w):
        ab = jnp.dot(h[:, g * w:(g + 1) * w], w_ref[...], preferred_element_type=f32).astype(bf16)
        a_ref[:, g * w:(g + 1) * w] = ab[:, :w]
        b_ref[:, g * w:(g + 1) * w] = ab[:, w:]


@functools.lru_cache(maxsize=None)
def _fourier_half_table(seq):
    k = np.arange(seq // 2 + FN_HALO)
    t = np.arange(seq)
    ang = 2.0 * np.pi * ((k[:, None] * t[None, :]) % seq) / seq
    return (np.concatenate([np.cos(ang), np.sin(ang)], axis=1) / math.sqrt(seq)).astype(np.float32)


@functools.lru_cache(maxsize=None)
def _row_reversal(tm):
    out = np.zeros((tm, tm + FN_HALO), np.float32)
    out[np.arange(tm), tm - np.arange(tm)] = 1.0
    return np.asarray(out, dtype=bf16)


def _fn_b_kernel(cs_ref, rev_ref, a_ref, b_ref, x_ref, mod_ref, ng_ref, wo_ref, bo_ref, o_ref, mirror_ref, *,
                 seq):
    s = pl.program_id(1)
    half_tiles = pl.num_programs(1) // 2
    n_seq, tm = x_ref.shape[0], x_ref.shape[1]
    src_rows = tm + FN_HALO

    def finish(f_bf16):
        f = jnp.concatenate(f_bf16, axis=0)
        y = jnp.dot(f, wo_ref[...], preferred_element_type=f32) + bo_ref[...]
        x = x_ref[...].reshape(n_seq * tm, D_MODEL)
        o_ref[...] = _residual(x, y, ng_ref[1:2, :], mod_ref[2:3, :]).reshape(n_seq, tm, D_MODEL)

    @pl.when(s < half_tiles)
    def _():
        first = pl.multiple_of(s * tm, tm)
        rows = cs_ref[pl.ds(first, src_rows), :]
        out = []
        for i in range(n_seq):
            p = jnp.dot(rows[:, :seq], a_ref[i], preferred_element_type=f32)
            q = jnp.dot(rows[:, seq:], b_ref[i], preferred_element_type=f32)
            mirror_ref[i, pl.ds(first, src_rows), :] = (p + q).astype(bf16)
            out.append((p - q)[:tm, :].astype(bf16))
        finish(out)

    @pl.when(s >= half_tiles)
    def _():
        first = pl.multiple_of((2 * half_tiles - 1 - s) * tm, tm)
        finish([jnp.dot(rev_ref[...], mirror_ref[i, pl.ds(first, src_rows), :],
                        preferred_element_type=f32).astype(bf16) for i in range(n_seq)])


def _fourier_layer(x, trunk, layer, mod, norm_g, w_out, b_out):
    nb, seq = trunk.batch, trunk.seq
    xm, merged = _merge_sequences(x, trunk)
    tl = _tile_rows(merged, 512)
    a, b = pl.pallas_call(
        _fn_a_kernel,
        grid=(merged.batch, merged.seq // tl),
        in_specs=[_x_spec(tl), _mod_spec(merged, layer), _ng_spec(layer),
                  _resident((FN_GROUP_W, 2 * FN_GROUP_W))],
        out_specs=[_x_spec(tl)] * 2,
        out_shape=[jax.ShapeDtypeStruct(xm.shape, bf16)] * 2,
        compiler_params=_params("parallel", "parallel"),
    )(xm, mod, norm_g, _mxu_table(_channel_dft()))

    tm = min(seq // 2, 512)
    half_rows = seq // 2 + FN_HALO
    n_seq = 1 if trunk.per_batch_mod else math.gcd(nb, max(1, 512 // tm))
    tile = pl.BlockSpec((n_seq, tm, D_MODEL), lambda b, i: (b, i, 0))
    mode = pl.Buffered(1) if seq // tm > 2 else None
    whole = pl.BlockSpec((n_seq, seq, D_MODEL), lambda b, i: (b, 0, 0), pipeline_mode=mode)
    return pl.pallas_call(
        functools.partial(_fn_b_kernel, seq=seq),
        grid=(nb // n_seq, seq // tm),
        in_specs=[_resident((half_rows, 2 * seq)), _resident((tm, tm + FN_HALO)), whole, whole,
                  tile, _mod_spec(trunk, layer), _ng_spec(layer),
                  _resident((D_MODEL, D_MODEL)), _resident((1, D_MODEL))],
        out_specs=tile,
        out_shape=jax.ShapeDtypeStruct(x.shape, f32),
        scratch_shapes=[pltpu.VMEM((n_seq, half_rows, D_MODEL), bf16)],
        compiler_params=_params("parallel", "arbitrary"),
    )(_mxu_table(_fourier_half_table(seq)), jnp.asarray(_row_reversal(tm)), a.reshape(x.shape), b.reshape(x.shape),
      x, mod, norm_g, w_out, b_out)


@functools.lru_cache(maxsize=None)
def _hyena_features(seq):
    t = np.linspace(0.0, 1.0, seq)[:, None]
    ang = 2.0 * np.pi * np.arange(seq)[:, None] / seq
    bands = np.linspace(1e-4, HY_BANDS - 1, HY_BANDS)[None]
    z = np.concatenate([t, np.cos(bands * ang), -np.sin(bands * ang)], axis=-1)
    out = np.zeros((seq, HY_EMB_PAD), np.float32)
    out[:, :HY_EMB] = z
    return out


@functools.lru_cache(maxsize=None)
def _hyena_decay_rates():
    d = np.linspace(math.log(HY_TARGET) / HY_FAST, math.log(HY_TARGET) / HY_SLOW, D_MODEL)
    return np.abs(d)[None].astype(np.float32)


def _hy_in_kernel(x_ref, xp_ref, xn_ref, mod_ref, ng_ref, w_ref, b_ref, cw_ref, cb_ref, x0_ref, z_ref, zb_ref):
    j = pl.program_id(1)
    g, sh, sc = ng_ref[0:1, :], mod_ref[0:1, :], mod_ref[1:2, :]
    tl = x_ref.shape[0]
    rows = jnp.concatenate([x_ref[...], xp_ref[...], xn_ref[...]], axis=0)
    u_all = jnp.dot(_norm_mod(rows, g, sh, sc).astype(bf16), w_ref[...], preferred_element_type=f32) + b_ref[...]
    u = u_all[:tl, :]
    prev_row = jnp.where(j == 0, 0.0, u_all[tl + 7:tl + 8, :])
    next_row = jnp.where(j == pl.num_programs(1) - 1, 0.0, u_all[tl + 8:tl + 9, :])
    u_prev, u_next = _shifted(u, prev_row, next_row)
    uc = u_prev * cw_ref[0:1, :] + u * cw_ref[1:2, :] + u_next * cw_ref[2:3, :] + cb_ref[...]
    x0_ref[...] = uc[:, :D_MODEL]
    z = uc[:, 2 * D_MODEL:] * uc[:, D_MODEL:2 * D_MODEL]
    z_ref[...] = z
    zb_ref[...] = z.astype(bf16)


def _hy_filter_kernel(feat_ref, w1_ref, b1_ref, fq_ref, w2_ref, b2_ref, w3f_ref, w3b_ref, rate_ref,
                      cs_ref, o_ref, hdn_ref, *, seq):
    @pl.when(pl.program_id(0) == 0)
    def _():
        hdn = jnp.sin(fq_ref[0:1, :] * (_dot3(feat_ref[...], w1_ref[...]) + b1_ref[...]))
        hdn_ref[...] = jnp.sin(fq_ref[1:2, :] * (_dot3(hdn, w2_ref[...]) + b2_ref[...]))

    hdn = hdn_ref[...]
    rows = lax.broadcasted_iota(jnp.int32, (seq, 1), 0)
    t = rows.astype(f32) * (1.0 / (seq - 1))
    win = jnp.exp(-t * rate_ref[...])
    k_fwd = _dot3(hdn, w3f_ref[...]) * win
    k_bwd = _dot3(hdn, w3b_ref[...]) * win
    both = k_fwd + k_bwd
    diff = jnp.where(rows == 0, both, k_fwd - k_bwd)
    k_re = jnp.dot(cs_ref[:, :seq], both.astype(bf16), preferred_element_type=f32)
    k_im = jnp.dot(cs_ref[:, seq:], diff.astype(bf16), preferred_element_type=f32)
    sign = (1 - 2 * (rows & 1)).astype(f32)
    k_nyq = jnp.sum(both * sign, axis=0, keepdims=True)
    wgt = jnp.where(rows == 0, 0.5 / seq, 1.0 / seq)
    o_ref[0] = k_re * wgt
    o_ref[1] = jnp.where(rows == 0, k_nyq, k_im) * wgt


def _alternating_sign(first, rows):
    t = first + lax.broadcasted_iota(jnp.int32, (rows, 1), 0)
    return (1 - 2 * (t & 1)).astype(f32)


def _hy_fwd_kernel(cs_ref, zb_ref, ks_ref, o_ref, nyq_ref, *, seq):
    i = pl.program_id(1)
    tf = o_ref.shape[1]
    rows = _table_rows(cs_ref, i, tf)
    zb = zb_ref[...]
    zr = jnp.dot(rows[:, :seq], zb, preferred_element_type=f32)
    zi = jnp.dot(rows[:, seq:], zb, preferred_element_type=f32)

    @pl.when(i == 0)
    def _():
        nyq_ref[...] = jnp.sum(zb.astype(f32) * _alternating_sign(0, seq), axis=0, keepdims=True)

    @pl.when(i != 0)
    def _():
        nyq_ref[...] = jnp.zeros_like(nyq_ref)

    kr, ki = ks_ref[0], ks_ref[1]
    packed = (i * tf + lax.broadcasted_iota(jnp.int32, (tf, 1), 0)) == 0
    o_ref[0] = (zr * kr - zi * ki).astype(bf16)
    o_ref[1] = jnp.where(packed, nyq_ref[...] * ki, zr * ki + zi * kr).astype(bf16)


def _hy_inv_kernel(cs_ref, ys_ref, z_ref, x0_ref, x_ref, mod_ref, ng_ref, d_ref, wo_ref, bo_ref, o_ref, *, seq):
    i = pl.program_id(1)
    tm = x_ref.shape[0]
    conv = jnp.dot(_table_rows(cs_ref, i, tm), ys_ref[...], preferred_element_type=f32)
    conv = conv + _alternating_sign(i * tm, tm) * ys_ref[seq:seq + 1, :].astype(f32)
    y = conv + z_ref[...] * d_ref[...]
    out = jnp.dot((y * x0_ref[...]).astype(bf16), wo_ref[...], preferred_element_type=f32) + bo_ref[...]
    o_ref[...] = _residual(x_ref[...], out, ng_ref[1:2, :], mod_ref[2:3, :])


def _hyena_layer(x, trunk, layer, mod, norm_g, p):
    nb, seq = trunk.batch, trunk.seq
    tl = _tile_rows(trunk, 256)
    x8 = x.reshape(nb, seq // 8, 8, D_MODEL)
    prev_spec, next_spec = _halo_specs(trunk, tl)
    x0, z, zb = pl.pallas_call(
        _hy_in_kernel,
        grid=(nb, seq // tl),
        in_specs=[_x_spec(tl), prev_spec, next_spec, _mod_spec(trunk, layer), _ng_spec(layer),
                  _resident((D_MODEL, 3 * D_MODEL)), _resident((1, 3 * D_MODEL)),
                  _resident((3, 3 * D_MODEL)), _resident((1, 3 * D_MODEL))],
        out_specs=[_x_spec(tl)] * 3,
        out_shape=[jax.ShapeDtypeStruct(x.shape, f32)] * 2 + [jax.ShapeDtypeStruct(x.shape, bf16)],
        compiler_params=_params("parallel", "parallel"),
    )(x, x8, x8, mod, norm_g, p["w_in"], p["b_in"], p["conv_w"], p["conv_b"])

    table = _mxu_table(_time_dft(seq, 2 * seq, 1.0))
    table_spec = _resident((seq, 2 * seq))
    tn = 256
    nblk = D_MODEL // tn
    ks = pl.pallas_call(
        functools.partial(_hy_filter_kernel, seq=seq),
        grid=(nblk,),
        in_specs=[_resident((seq, HY_EMB_PAD)), _resident((HY_EMB_PAD, HY_FILT)), _resident((1, HY_FILT)),
                  _resident((2, HY_FILT)), _resident((HY_FILT, HY_FILT)), _resident((1, HY_FILT)),
                  pl.BlockSpec((HY_FILT, tn), lambda j: (0, j)),
                  pl.BlockSpec((HY_FILT, tn), lambda j: (0, nblk + j)),
                  pl.BlockSpec((1, tn), lambda j: (0, j)),
                  table_spec],
        out_specs=pl.BlockSpec((2, seq, tn), lambda j: (0, 0, j)),
        out_shape=jax.ShapeDtypeStruct((2, seq, D_MODEL), f32),
        scratch_shapes=[pltpu.VMEM((seq, HY_FILT), f32)],
        compiler_params=_params("arbitrary"),
    )(jnp.asarray(_hyena_features(seq)), p["f_w1"], p["f_b1"], p["f_freq"], p["f_w2"], p["f_b2"],
      p["f_w3"], p["f_w3"], jnp.asarray(_hyena_decay_rates()), table)

    tf = _tile_rows(trunk, 512)
    whole_seq = lambda rows: _whole_seq_spec(rows, seq // tf)
    ys = pl.pallas_call(
        functools.partial(_hy_fwd_kernel, seq=seq),
        grid=(nb, seq // tf),
        in_specs=[table_spec, whole_seq(seq), pl.BlockSpec((2, tf, D_MODEL), lambda b, i: (0, i, 0))],
        out_specs=pl.BlockSpec((None, 2, tf, D_MODEL), lambda b, i: (b, 0, i, 0)),
        out_shape=jax.ShapeDtypeStruct((nb, 2, seq, D_MODEL), bf16),
        scratch_shapes=[pltpu.VMEM((1, D_MODEL), f32)],
        compiler_params=_params("parallel", "arbitrary"),
    )(table, zb, ks)

    return pl.pallas_call(
        functools.partial(_hy_inv_kernel, seq=seq),
        grid=(nb, seq // tf),
        in_specs=[table_spec, whole_seq(2 * seq), _x_spec(tf), _x_spec(tf), _x_spec(tf),
                  _mod_spec(trunk, layer), _ng_spec(layer),
                  _resident((1, D_MODEL)), _resident((D_MODEL, D_MODEL)), _resident((1, D_MODEL))],
        out_specs=_x_spec(tf),
        out_shape=jax.ShapeDtypeStruct(x.shape, f32),
        compiler_params=_params("parallel", "parallel"),
    )(table, ys.reshape(nb, 2 * seq, D_MODEL), z, x0, x, mod, norm_g, p["d"], p["w_out"], p["b_out"])


def _head_sum(t, ones_pair):
    hi, lo = _split(t)
    cols = []
    for p in range(D_MODEL // PAIR_W):
        sl = slice(p * PAIR_W, (p + 1) * PAIR_W)
        cols.append(jnp.dot(hi[:, sl], ones_pair, preferred_element_type=f32)
                    + jnp.dot(lo[:, sl], ones_pair, preferred_element_type=f32))
    return jnp.concatenate(cols, axis=1)


def _rw_pre_kernel(x_ref, xp_ref, xn_ref, mod_ref, ng_ref, mix_ref, wr_ref, wk_ref, wv_ref, g1_ref, g2_ref,
                   w0_ref, w1_ref, w2_ref, a0_ref, a1_ref, a2_ref, kkp_ref, kap_ref, rkp_ref, ones_ref,
                   r_ref, k_ref, v_ref, kk_ref, g_ref, lwf_ref, lwb_ref, af_ref, ab_ref, bn_ref):
    j = pl.program_id(1)
    g, sh, sc = ng_ref[0:1, :], mod_ref[0:1, :], mod_ref[1:2, :]
    h = _norm_mod(x_ref[...], g, sh, sc)
    prev_row = jnp.where(j == 0, 0.0, _norm_mod(xp_ref[...], g, sh, sc)[7:8, :])
    next_row = jnp.where(j == pl.num_programs(1) - 1, 0.0, _norm_mod(xn_ref[...], g, sh, sc)[0:1, :])
    h_prev, h_next = _shifted(h, prev_row, next_row)
    xx = 0.5 * (h_prev + h_next) - h

    def mixed(m):
        return (h + xx * mix_ref[m:m + 1, :]).astype(bf16)

    r = jnp.dot(mixed(0), wr_ref[...], preferred_element_type=f32)
    xw = mixed(1)
    k = jnp.dot(mixed(2), wk_ref[...], preferred_element_type=f32)
    v = jnp.dot(mixed(3), wv_ref[...], preferred_element_type=f32)
    xa = mixed(4)
    r_ref[...] = r
    k_ref[...] = k
    v_ref[...] = v
    gate = _sigmoid(jnp.dot(mixed(5), g1_ref[...], preferred_element_type=f32))
    g_ref[...] = jnp.dot(gate.astype(bf16), g2_ref[...], preferred_element_type=f32)
    ones = ones_ref[...]
    kk = k * kkp_ref[...]
    kk_ref[...] = kk * lax.rsqrt(jnp.maximum(_head_sum(kk * kk, ones), 1e-24))
    a_sum = None
    for dd, (lw_ref, a_ref) in enumerate(((lwf_ref, af_ref), (lwb_ref, ab_ref))):
        lora = jnp.tanh(jnp.dot(xw, w1_ref[dd], preferred_element_type=f32))
        wl = w0_ref[dd:dd + 1, :] + jnp.dot(lora.astype(bf16), w2_ref[dd], preferred_element_type=f32)
        lw_ref[...] = -math.exp(-0.5) * _sigmoid(wl)
        al = jnp.dot(xa, a1_ref[dd], preferred_element_type=f32)
        a = _sigmoid(a0_ref[dd:dd + 1, :] + jnp.dot(al.astype(bf16), a2_ref[dd], preferred_element_type=f32))
        a_ref[...] = a
        a_sum = a if a_sum is None else a_sum + a
    kd_sum = k * (2.0 + (a_sum - 2.0) * kap_ref[...])
    bn_ref[...] = _head_sum(r * kd_sum * rkp_ref[...], ones) * v


def _stack_heads(x, first_head):
    return jnp.concatenate([jnp.where(first_head, x, 0.0), jnp.where(first_head, 0.0, x)], axis=0).astype(bf16)


def _dot_nt(a, b):
    return lax.dot_general(a, b, (((1,), (1,)), ((), ())), preferred_element_type=f32)


def _dot_tn(a, b):
    return lax.dot_general(a, b, (((0,), (0,)), ((), ())), preferred_element_type=f32)


def _rw_scan_kernel(r_ref, k_ref, v_ref, kk_ref, lw_ref, a_ref, kap_ref, s0_ref, o_ref, s_ref, *, reverse):
    c = pl.program_id(1)
    C, W, G = SCAN_CHUNK, PAIR_W, SCAN_GROUP

    @pl.when(c == 0)
    def _():
        s_ref[...] = s0_ref[...]

    row = lax.broadcasted_iota(jnp.int32, (C, W), 0)
    lane = lax.broadcasted_iota(jnp.int32, (C, W), 1)
    col = lane & (C - 1)
    first_head = lane < RW_HEAD
    before = (col > row) if reverse else (col < row)
    before_eq = before | (col == row)
    eye = (col == row).astype(f32)
    row_w = lax.broadcasted_iota(jnp.int32, (W, W), 0)
    lane_w = lax.broadcasted_iota(jnp.int32, (W, W), 1)
    same_head = (row_w >= RW_HEAD) == (lane_w >= RW_HEAD)
    diag_w = row_w == lane_w
    merge_masks = []
    s = 1
    while s < C:
        sh = s.bit_length() - 1
        same_pair = (row >> (sh + 1)) == (col >> (sh + 1))
        later, earlier = ((row >> sh) & 1, (col >> sh) & 1)
        merge_masks.append(same_pair & (((later == 0) & (earlier == 1)) if reverse
                                        else ((later == 1) & (earlier == 0))))
        s *= 2

    row_c = lax.broadcasted_iota(jnp.int32, (G * C, G * C), 0)
    col_c = lax.broadcasted_iota(jnp.int32, (G * C, G * C), 1)
    chunk_bits = C.bit_length() - 1
    same_chunk = (row_c >> chunk_bits) == (col_c >> chunk_bits)
    tri = (same_chunk & ((col_c >= row_c) if reverse else (col_c <= row_c))).astype(bf16)
    lw_all = lw_ref[...]
    cum_all = _dot_exact_rhs_left(tri, lw_all)
    n_pairs = D_MODEL // W

    def pairs(x):
        return [x[g * C:(g + 1) * C, p * W:(p + 1) * W] for g in range(G) for p in range(n_pairs)]

    def per_chunk_rows(rows):
        return [rows[g][:, p * W:(p + 1) * W] for g in range(G) for p in range(n_pairs)]

    def each(fn, *lists):
        return [fn(*args) for args in zip(*lists)]

    def stack(x):
        return _stack_heads(x, first_head)

    def mm(x, y):
        return jnp.dot(x.astype(bf16), stack(y), preferred_element_type=f32)

    def mm2(x, y0, y1):
        return jnp.dot(x.astype(bf16), jnp.concatenate([stack(y0), stack(y1)], axis=1),
                       preferred_element_type=f32)

    r, k, v, kk, a = (pairs(ref[...]) for ref in (r_ref, k_ref, v_ref, kk_ref, a_ref))
    lw, cum = pairs(lw_all), pairs(cum_all)
    tot = per_chunk_rows([jnp.sum(lw_all[g * C:(g + 1) * C, :], axis=0, keepdims=True) for g in range(G)])
    ka = per_chunk_rows([kap_ref[...]] * G)
    kd = each(lambda k_, a_, ka_: k_ * (1.0 + (a_ - 1.0) * ka_), k, a, ka)
    b = each(lambda kk_, a_: kk_ * a_, kk, a)
    e_in = each(jnp.exp, cum)
    e_out = each(lambda c_: jnp.exp(-c_), cum)
    e_end = each(lambda t_, c_: jnp.exp(t_ - c_), tot, cum)
    kap = each(lambda kk_, c_, l_: kk_ * jnp.exp(c_ - l_), kk, cum, lw)
    mul = lambda x_, y_: x_ * y_
    kt, bt, rt, kh, bh = each(mul, kd, e_out), each(mul, b, e_out), each(mul, r, e_in), each(mul, kd, e_end), \
        each(mul, b, e_end)

    lhs = each(lambda x_, y_: jnp.concatenate([x_, y_], axis=0).astype(bf16), kap, rt)
    gram = each(lambda l_, y0_, y1_: _dot_nt(l_, jnp.concatenate([stack(y0_), stack(y1_)], axis=0)), lhs, kt, bt)
    m_kk = each(lambda g_: jnp.where(before, g_[:C, :W], 0.0), gram)
    n_kb = each(lambda g_: jnp.where(before, g_[:C, W:], 0.0), gram)
    a_qk = each(lambda g_: jnp.where(before_eq, g_[C:, :W], 0.0), gram)
    a_qb = each(lambda g_: jnp.where(before_eq, g_[C:, W:], 0.0), gram)

    tinv = each(lambda n_: eye - jnp.where(merge_masks[0], n_, 0.0), n_kb)
    for off in merge_masks[1:]:
        tn = each(lambda t_, n_: mm(t_, jnp.where(off, n_, 0.0)), tinv, n_kb)
        tinv = each(lambda t_, tn_: t_ - mm(tn_, t_), tinv, tn)

    mv = each(mm, m_kk, v)
    tk = each(mm2, tinv, kap, mv)
    kt_c, w1 = each(lambda t_: t_[:, :W], tk), each(lambda t_: t_[:, W:], tk)
    qa = each(mm2, a_qb, kt_c, w1)
    av = each(mm, a_qk, v)
    q_eff = each(lambda rt_, qa_: rt_ - qa_[:, :W], rt, qa)
    o_intra = each(lambda av_, qa_: av_ - qa_[:, W:], av, qa)
    kb = each(lambda x_, b_: _dot_tn(x_.astype(bf16), b_.astype(bf16)), kt_c, bh)
    phi = each(lambda t_, kb_: jnp.where(same_head, jnp.where(diag_w, jnp.exp(t_), 0.0) - kb_, 0.0), tot, kb)
    ds_full = each(lambda v_, w_, kh_, bh_: _dot_tn(jnp.concatenate([v_, -w_], axis=0).astype(bf16),
                                                    jnp.concatenate([kh_, bh_], axis=0).astype(bf16)),
                   v, w1, kh, bh)
    d_s = each(lambda d_: jnp.where(first_head, d_[:RW_HEAD, :], d_[RW_HEAD:, :]), ds_full)
    state = [s_ref[:, p * W:(p + 1) * W] for p in range(n_pairs)]
    for g in (range(G - 1, -1, -1) if reverse else range(G)):
        sel = slice(g * n_pairs, (g + 1) * n_pairs)
        o = each(lambda q_, s_, oi_: _dot_nt(q_.astype(bf16), stack(s_)) + oi_, q_eff[sel], state, o_intra[sel])
        state = each(lambda s_, ph_, ds_: jnp.dot(s_.astype(bf16), ph_.astype(bf16),
                                                  preferred_element_type=f32) + ds_, state, phi[sel], d_s[sel])
        o_ref[g * C:(g + 1) * C, :] = jnp.concatenate(o, axis=1)
    s_ref[...] = jnp.concatenate(state, axis=1)


def _rw_post_kernel(of_ref, ob_ref, bn_ref, g_ref, x_ref, mod_ref, ng_ref, lw_ref, lb_ref, ones_ref,
                    wo_ref, o_ref):
    o = of_ref[...] + ob_ref[...]
    ones = ones_ref[...]
    inv_n = 1.0 / RW_HEAD
    dev = o - _head_sum(o, ones) * inv_n
    var = _head_sum(dev * dev, ones) * inv_n
    on = dev * lax.rsqrt(var + RW_LN_EPS) * lw_ref[...] + lb_ref[...] + bn_ref[...]
    y = jnp.dot((on * g_ref[...]).astype(bf16), wo_ref[...], preferred_element_type=f32)
    o_ref[...] = _residual(x_ref[...], y, ng_ref[1:2, :], mod_ref[2:3, :])


@functools.lru_cache(maxsize=None)
def _pair_ones():
    idx = np.arange(PAIR_W) // RW_HEAD
    return np.asarray(idx[:, None] == idx[None, :], dtype=bf16)


def _rwkv_layer(x, trunk, layer, mod, norm_g, p, s0_fwd, s0_bwd):
    nb, seq = trunk.batch, trunk.seq
    tl = _tile_rows(trunk, 256)
    x8 = x.reshape(nb, seq // 8, 8, D_MODEL)
    prev_spec, next_spec = _halo_specs(trunk, tl)
    lora_w, lora_a, lora_g = p["w1"].shape[-1], p["a1"].shape[-1], p["g1"].shape[-1]
    ones = jnp.asarray(_pair_ones())
    outs = pl.pallas_call(
        _rw_pre_kernel,
        grid=(nb, seq // tl),
        in_specs=[_x_spec(tl), prev_spec, next_spec, _mod_spec(trunk, layer), _ng_spec(layer),
                  _resident((6, D_MODEL)),
                  _resident((D_MODEL, D_MODEL)), _resident((D_MODEL, D_MODEL)), _resident((D_MODEL, D_MODEL)),
                  _resident((D_MODEL, lora_g)), _resident((lora_g, D_MODEL)),
                  _resident((2, D_MODEL)), _resident((2, D_MODEL, lora_w)), _resident((2, lora_w, D_MODEL)),
                  _resident((2, D_MODEL)), _resident((2, D_MODEL, lora_a)), _resident((2, lora_a, D_MODEL)),
                  _resident((1, D_MODEL)), _resident((1, D_MODEL)), _resident((1, D_MODEL)),
                  _resident((PAIR_W, PAIR_W))],
        out_specs=[_x_spec(tl)] * 10,
        out_shape=[jax.ShapeDtypeStruct(x.shape, f32)] * 10,
        compiler_params=_params("parallel", "parallel"),
    )(x, x8, x8, mod, norm_g, p["mix"], p["wr"], p["wk"], p["wv"], p["g1"], p["g2"],
      p["w0"], p["w1"], p["w2"], p["a0"], p["a1"], p["a2"], p["kk"], p["ka"], p["rk"], ones)
    r, k, v, kk, gate, lw_f, lw_b, a_f, a_b, bonus = outs

    def state_in(s):
        return s.transpose(0, 2, 1, 3).reshape(nb, RW_HEAD, D_MODEL)

    def state_out(s):
        return s.reshape(nb, RW_HEAD, RW_HEADS, RW_HEAD).transpose(0, 2, 1, 3)

    rows = SCAN_CHUNK * SCAN_GROUP
    n_chunks = seq // rows
    state_spec = pl.BlockSpec((None, RW_HEAD, D_MODEL), lambda b, c: (b, 0, 0))
    results = []
    for reverse, lw, a, s0 in ((False, lw_f, a_f, s0_fwd), (True, lw_b, a_b, s0_bwd)):
        if reverse:
            chunk_spec = pl.BlockSpec((None, rows, D_MODEL), lambda b, c: (b, n_chunks - 1 - c, 0))
        else:
            chunk_spec = pl.BlockSpec((None, rows, D_MODEL), lambda b, c: (b, c, 0))
        results.append(pl.pallas_call(
            functools.partial(_rw_scan_kernel, reverse=reverse),
            grid=(nb, n_chunks),
            in_specs=[chunk_spec] * 6 + [_resident((1, D_MODEL)), state_spec],
            out_specs=[chunk_spec, state_spec],
            out_shape=[jax.ShapeDtypeStruct(x.shape, f32),
                       jax.ShapeDtypeStruct((nb, RW_HEAD, D_MODEL), f32)],
            compiler_params=_params("parallel", "arbitrary"),
        )(r, k, v, kk, lw, a, p["ka"], state_in(s0)))
    (o_f, s_f), (o_b, s_b) = results

    y = pl.pallas_call(
        _rw_post_kernel,
        grid=(nb, seq // tl),
        in_specs=[_x_spec(tl)] * 5 + [_mod_spec(trunk, layer), _ng_spec(layer),
                                      _resident((1, D_MODEL)), _resident((1, D_MODEL)),
                                      _resident((PAIR_W, PAIR_W)), _resident((D_MODEL, D_MODEL))],
        out_specs=_x_spec(tl),
        out_shape=jax.ShapeDtypeStruct(x.shape, f32),
        compiler_params=_params("parallel", "parallel"),
    )(o_f, o_b, bonus, gate, x, mod, norm_g, p["lnx_w"], p["lnx_b"], ones, p["wo"])
    return y, state_out(s_f), state_out(s_b)


def kernel(x_prompt, x_sample, state_wkv_fwd, state_wkv_bwd, c, c_ctx, ada_w, ada_b, norm_g, ffn_w1, ffn_w3, ffn_w2, fn_w_out, fn_b_out, hy_w_in, hy_b_in, hy_conv_w, hy_conv_b, hy_f_w1, hy_f_b1, hy_f_freq, hy_f_w2, hy_f_b2, hy_f_w3, hy_d, hy_w_out, hy_b_out, rw_mix, rw_wr, rw_wk, rw_wv, rw_wo, rw_w0, rw_w1, rw_w2, rw_a0, rw_a1, rw_a2, rw_g1, rw_g2, rw_kk, rw_ka, rw_rk, rw_lnx_w, rw_lnx_b):
    n_ctx, n_dec = x_prompt.shape[0], x_sample.shape[0]
    assert 1 + n_dec <= MOD_ROWS
    trunks = (Trunk(n_ctx, x_prompt.shape[1], 0, False), Trunk(n_dec, x_sample.shape[1], 1, True))
    cond = jnp.zeros((MOD_ROWS, D_MODEL), f32).at[0].set(c_ctx).at[1:1 + n_dec].set(c)
    mod = _adaln(cond, ada_w, ada_b)

    n_rwkv = rw_w0.shape[0]
    zero_state = jnp.zeros((n_ctx, n_rwkv, RW_HEADS, RW_HEAD, RW_HEAD), f32)
    states = ((zero_state, zero_state), (state_wkv_fwd, state_wkv_bwd))
    xs = [x_prompt, x_sample]
    new_fwd, new_bwd = [], []
    for i in range(DEPTH):
        kind, j = i % N_MIXERS, i // N_MIXERS
        if kind == 0:
            w_out, b_out = fn_w_out[j].astype(bf16), fn_b_out[j][None]
            xs = [_fourier_layer(x, t, i, mod, norm_g, w_out, b_out) for x, t in zip(xs, trunks)]
        elif kind == 1:
            w1_pad = jnp.zeros((HY_EMB_PAD, HY_FILT), f32).at[:HY_EMB].set(hy_f_w1[j])
            p = dict(w_in=hy_w_in[j].astype(bf16), b_in=hy_b_in[j][None], conv_w=hy_conv_w[j],
                     conv_b=hy_conv_b[j][None], f_w1=w1_pad, f_b1=hy_f_b1[j][None], f_freq=hy_f_freq[j],
                     f_w2=hy_f_w2[j], f_b2=hy_f_b2[j][None], f_w3=hy_f_w3[j], d=hy_d[j][None],
                     w_out=hy_w_out[j].astype(bf16), b_out=hy_b_out[j][None])
            xs = [_hyena_layer(x, t, i, mod, norm_g, p) for x, t in zip(xs, trunks)]
        else:
            per_head = lambda t: t.reshape(1, D_MODEL)
            p = dict(mix=rw_mix[j], wr=rw_wr[j].astype(bf16), wk=rw_wk[j].astype(bf16),
                     wv=rw_wv[j].astype(bf16), wo=rw_wo[j].astype(bf16), w0=rw_w0[j],
                     w1=rw_w1[j].astype(bf16), w2=rw_w2[j].astype(bf16), a0=rw_a0[j],
                     a1=rw_a1[j].astype(bf16), a2=rw_a2[j].astype(bf16), g1=rw_g1[j].astype(bf16),
                     g2=rw_g2[j].astype(bf16), kk=per_head(rw_kk[j]), ka=per_head(rw_ka[j]),
                     rk=per_head(rw_rk[j]), lnx_w=rw_lnx_w[j][None], lnx_b=rw_lnx_b[j][None])
            outs = [_rwkv_layer(x, t, i, mod, norm_g, p, sf[:, j], sb[:, j])
                    for x, t, (sf, sb) in zip(xs, trunks, states)]
            xs = [o[0] for o in outs]
            new_fwd.append(outs[0][1])
            new_bwd.append(outs[0][2])
        w1, w3, w2 = ffn_w1[i].astype(bf16), ffn_w3[i].astype(bf16), ffn_w2[i].astype(bf16)
        xs = [_ffn(x, t, i, mod, norm_g, w1, w3, w2) for x, t in zip(xs, trunks)]
    return xs[0], xs[1], jnp.stack(new_fwd, axis=1), jnp.stack(new_bwd, axis=1)
```

```python
import functools
import math
from typing import NamedTuple

import numpy as np
import jax
import jax.numpy as jnp
from jax import lax
from jax.experimental import pallas as pl
from jax.experimental.pallas import tpu as pltpu

f32 = jnp.float32
bf16 = jnp.bfloat16

D_MODEL = 1024
DEPTH = 4
N_MIXERS = 3
D_FF = 2816
NORM_EPS = 1e-6
FN_GROUP_W = 256
FN_HALO = 16
HY_EMB = 33
HY_EMB_PAD = 128
HY_BANDS = 16
HY_FILT = 64
HY_TARGET = 1e-2
HY_FAST = 0.3
HY_SLOW = 1.5
RW_HEAD = 64
RW_HEADS = 16
RW_LN_EPS = 64e-5
SCAN_CHUNK = 64
SCAN_GROUP = 2
PAIR_W = 2 * RW_HEAD
assert SCAN_CHUNK == RW_HEAD, "the scan keeps (chunk, chunk) and (chunk, head) tiles in one lane layout"
MOD_ROWS = 8
VMEM_LIMIT = 56 * 1024 * 1024


class Trunk(NamedTuple):
    batch: int
    seq: int
    mod_base: int
    per_batch_mod: bool


def _params(*sem):
    return pltpu.CompilerParams(dimension_semantics=sem, vmem_limit_bytes=VMEM_LIMIT)


def _resident(shape):
    nd = len(shape)
    return pl.BlockSpec(shape, lambda *_: (0,) * nd, pipeline_mode=pl.Buffered(1))


def _dot(a, b):
    return jnp.dot(a.astype(bf16), b.astype(bf16), preferred_element_type=f32)


def _split(x):
    hi = x.astype(bf16)
    lo = (x - hi.astype(f32)).astype(bf16)
    return hi, lo


def _dot3(a, b):
    ah, al = _split(a)
    bh, bl = _split(b)
    d = functools.partial(jnp.dot, preferred_element_type=f32)
    return d(ah, bh) + (d(ah, bl) + d(al, bh))


def _dot_exact_rhs_left(a_exact, b):
    b0 = b.astype(bf16)
    r1 = b - b0.astype(f32)
    b1 = r1.astype(bf16)
    b2 = (r1 - b1.astype(f32)).astype(bf16)
    d = functools.partial(jnp.dot, preferred_element_type=f32)
    return d(a_exact, b0) + (d(a_exact, b1) + d(a_exact, b2))


def _sigmoid(x):
    return 0.5 * jnp.tanh(0.5 * x) + 0.5


def _norm_mod(x, g, shift, scale):
    ms = jnp.mean(x * x, axis=-1, keepdims=True)
    return (x * lax.rsqrt(ms + NORM_EPS)) * (g * (1.0 + scale)) + shift


def _residual(x, y, g, gate):
    ms = jnp.mean(y * y, axis=-1, keepdims=True)
    return x + gate * (y * lax.rsqrt(ms + NORM_EPS) * g)


def _shifted(h, prev_row, next_row):
    rows = h.shape[0]
    ridx = lax.broadcasted_iota(jnp.int32, h.shape, 0)
    h_prev = jnp.where(ridx == 0, prev_row, pltpu.roll(h, 1, 0))
    h_next = jnp.where(ridx == rows - 1, next_row, pltpu.roll(h, rows - 1, 0))
    return h_prev, h_next


def _tile_rows(trunk, cap):
    return min(trunk.seq, cap)


def _x_spec(tl):
    return pl.BlockSpec((None, tl, D_MODEL), lambda b, j: (b, j, 0))


def _halo_specs(trunk, tl):
    g = tl // 8
    last = trunk.seq // 8 - 1
    prev = pl.BlockSpec((None, None, 8, D_MODEL), lambda b, j: (b, jnp.maximum(j * g - 1, 0), 0, 0))
    nxt = pl.BlockSpec((None, None, 8, D_MODEL), lambda b, j: (b, jnp.minimum((j + 1) * g, last), 0, 0))
    return prev, nxt


def _mod_spec(trunk, layer):
    if trunk.per_batch_mod:
        return pl.BlockSpec((None, None, 6, D_MODEL), lambda b, *_: (layer, trunk.mod_base + b, 0, 0))
    return pl.BlockSpec((None, None, 6, D_MODEL), lambda b, *_: (layer, trunk.mod_base, 0, 0))


def _ng_spec(layer):
    return pl.BlockSpec((None, 4, D_MODEL), lambda *_: (layer, 0, 0))


def _mod_kernel(c_ref, w_ref, b_ref, o_ref):
    c = c_ref[...]
    o_ref[...] = _dot3(c * _sigmoid(c), w_ref[...]) + b_ref[...]


def _adaln(cond, ada_w, ada_b):
    tn = 1536
    out = pl.pallas_call(
        _mod_kernel,
        grid=(DEPTH, 6 * D_MODEL // tn),
        in_specs=[pl.BlockSpec((MOD_ROWS, D_MODEL), lambda l, j: (0, 0)),
                  pl.BlockSpec((None, D_MODEL, tn), lambda l, j: (l, 0, j)),
                  pl.BlockSpec((None, 1, tn), lambda l, j: (l, 0, j))],
        out_specs=pl.BlockSpec((None, MOD_ROWS, tn), lambda l, j: (l, 0, j)),
        out_shape=jax.ShapeDtypeStruct((DEPTH, MOD_ROWS, 6 * D_MODEL), f32),
        compiler_params=_params("arbitrary", "arbitrary"),
    )(cond, ada_w, ada_b.reshape(DEPTH, 1, 6 * D_MODEL))
    return out.reshape(DEPTH, MOD_ROWS, 6, D_MODEL)


def _ffn_kernel(x_ref, mod_ref, ng_ref, w1_ref, w3_ref, w2_ref, o_ref):
    x = x_ref[...]
    h = _norm_mod(x, ng_ref[2:3, :], mod_ref[3:4, :], mod_ref[4:5, :]).astype(bf16)
    a = jnp.dot(h, w1_ref[...], preferred_element_type=f32)
    b = jnp.dot(h, w3_ref[...], preferred_element_type=f32)
    gated = (a * _sigmoid(a) * b).astype(bf16)
    y = jnp.dot(gated, w2_ref[...], preferred_element_type=f32)
    o_ref[...] = _residual(x, y, ng_ref[3:4, :], mod_ref[5:6, :])


def _merge_sequences(x, trunk):
    if trunk.per_batch_mod:
        return x, trunk
    return x.reshape(1, -1, D_MODEL), Trunk(1, trunk.batch * trunk.seq, trunk.mod_base, False)


def _ffn(x, trunk, layer, mod, norm_g, w1, w3, w2):
    shape = x.shape
    x, trunk = _merge_sequences(x, trunk)
    tl = _tile_rows(trunk, 512)
    return pl.pallas_call(
        _ffn_kernel,
        grid=(trunk.batch, trunk.seq // tl),
        in_specs=[_x_spec(tl), _mod_spec(trunk, layer), _ng_spec(layer),
                  _resident((D_MODEL, D_FF)), _resident((D_MODEL, D_FF)), _resident((D_FF, D_MODEL))],
        out_specs=_x_spec(tl),
        out_shape=jax.ShapeDtypeStruct(x.shape, f32),
        compiler_params=_params("parallel", "parallel"),
    )(x, mod, norm_g, w1, w3, w2).reshape(shape)


@functools.lru_cache(maxsize=None)
def _channel_dft():
    w = FN_GROUP_W
    idx = np.arange(w)
    ang = 2.0 * np.pi * ((idx[:, None] * idx[None, :]) % w) / w
    return (np.concatenate([np.cos(ang), np.sin(ang)], axis=1) / math.sqrt(w)).astype(np.float32)


def _table_rows(cs_ref, tile, rows):
    return cs_ref[pl.ds(pl.multiple_of(tile * rows, rows), rows), :]


def _mxu_table(table):
    return jnp.asarray(table).astype(bf16)


def _fn_a_kernel(x_ref, mod_ref, ng_ref, w_ref, a_ref, b_ref):
    h = _norm_mod(x_ref[...], ng_ref[0:1, :], mod_ref[0:1, :], mod_ref[1:2, :]).astype(bf16)
    w = FN_GROUP_W
    for g in range(D_MODEL // w):
        ab = jnp.dot(h[:, g * w:(g + 1) * w], w_ref[...], preferred_element_type=f32).astype(bf16)
        a_ref[:, g * w:(g + 1) * w] = ab[:, :w]
        b_ref[:, g * w:(g + 1) * w] = ab[:, w:]


@functools.lru_cache(maxsize=None)
def _fourier_half_table(seq):
    k = np.arange(seq // 2 + FN_HALO)
    t = np.arange(seq)
    ang = 2.0 * np.pi * ((k[:, None] * t[None, :]) % seq) / seq
    return (np.concatenate([np.cos(ang), np.sin(ang)], axis=1) / math.sqrt(seq)).astype(np.float32)


@functools.lru_cache(maxsize=None)
def _row_reversal(tm):
    out = np.zeros((tm, tm + FN_HALO), np.float32)
    out[np.arange(tm), tm - np.arange(tm)] = 1.0
    return np.asarray(out, dtype=bf16)


def _fn_b_kernel(cs_ref, rev_ref, a_ref, b_ref, x_ref, mod_ref, ng_ref, wo_ref, bo_ref, o_ref, mirror_ref, *,
                 seq):
    s = pl.program_id(1)
    half_tiles = pl.num_programs(1) // 2
    n_seq, tm = x_ref.shape[0], x_ref.shape[1]
    src_rows = tm + FN_HALO

    def finish(f_bf16):
        f = jnp.concatenate(f_bf16, axis=0)
        y = jnp.dot(f, wo_ref[...], preferred_element_type=f32) + bo_ref[...]
        x = x_ref[...].reshape(n_seq * tm, D_MODEL)
        o_ref[...] = _residual(x, y, ng_ref[1:2, :], mod_ref[2:3, :]).reshape(n_seq, tm, D_MODEL)

    @pl.when(s < half_tiles)
    def _():
        first = pl.multiple_of(s * tm, tm)
        rows = cs_ref[pl.ds(first, src_rows), :]
        out = []
        for i in range(n_seq):
            p = jnp.dot(rows[:, :seq], a_ref[i], preferred_element_type=f32)
            q = jnp.dot(rows[:, seq:], b_ref[i], preferred_element_type=f32)
            mirror_ref[i, pl.ds(first, src_rows), :] = (p + q).astype(bf16)
            out.append((p - q)[:tm, :].astype(bf16))
        finish(out)

    @pl.when(s >= half_tiles)
    def _():
        first = pl.multiple_of((2 * half_tiles - 1 - s) * tm, tm)
        finish([jnp.dot(rev_ref[...], mirror_ref[i, pl.ds(first, src_rows), :],
                        preferred_element_type=f32).astype(bf16) for i in range(n_seq)])


def _fourier_layer(x, trunk, layer, mod, norm_g, w_out, b_out):
    nb, seq = trunk.batch, trunk.seq
    xm, merged = _merge_sequences(x, trunk)
    tl = _tile_rows(merged, 512)
    a, b = pl.pallas_call(
        _fn_a_kernel,
        grid=(merged.batch, merged.seq // tl),
        in_specs=[_x_spec(tl), _mod_spec(merged, layer), _ng_spec(layer),
                  _resident((FN_GROUP_W, 2 * FN_GROUP_W))],
        out_specs=[_x_spec(tl)] * 2,
        out_shape=[jax.ShapeDtypeStruct(xm.shape, bf16)] * 2,
        compiler_params=_params("parallel", "parallel"),
    )(xm, mod, norm_g, _mxu_table(_channel_dft()))

    tm = min(seq // 2, 512)
    half_rows = seq // 2 + FN_HALO
    n_seq = 1 if trunk.per_batch_mod else math.gcd(nb, max(1, 512 // tm))
    tile = pl.BlockSpec((n_seq, tm, D_MODEL), lambda b, i: (b, i, 0))
    mode = pl.Buffered(1) if seq // tm > 2 else None
    whole = pl.BlockSpec((n_seq, seq, D_MODEL), lambda b, i: (b, 0, 0), pipeline_mode=mode)
    return pl.pallas_call(
        functools.partial(_fn_b_kernel, seq=seq),
        grid=(nb // n_seq, seq // tm),
        in_specs=[_resident((half_rows, 2 * seq)), _resident((tm, tm + FN_HALO)), whole, whole,
                  tile, _mod_spec(trunk, layer), _ng_spec(layer),
                  _resident((D_MODEL, D_MODEL)), _resident((1, D_MODEL))],
        out_specs=tile,
        out_shape=jax.ShapeDtypeStruct(x.shape, f32),
        scratch_shapes=[pltpu.VMEM((n_seq, half_rows, D_MODEL), bf16)],
        compiler_params=_params("parallel", "arbitrary"),
    )(_mxu_table(_fourier_half_table(seq)), jnp.asarray(_row_reversal(tm)), a.reshape(x.shape), b.reshape(x.shape),
      x, mod, norm_g, w_out, b_out)


@functools.lru_cache(maxsize=None)
def _hyena_features(seq):
    t = np.linspace(0.0, 1.0, seq)[:, None]
    ang = 2.0 * np.pi * np.arange(seq)[:, None] / seq
    bands = np.linspace(1e-4, HY_BANDS - 1, HY_BANDS)[None]
    z = np.concatenate([t, np.cos(bands * ang), -np.sin(bands * ang)], axis=-1)
    out = np.zeros((seq, HY_EMB_PAD), np.float32)
    out[:, :HY_EMB] = np.concatenate([z[0::2], z[1::2]], axis=0)
    return out


@functools.lru_cache(maxsize=None)
def _hyena_decay_rates():
    d = np.linspace(math.log(HY_TARGET) / HY_FAST, math.log(HY_TARGET) / HY_SLOW, D_MODEL)
    return np.abs(d)[None].astype(np.float32)


@functools.lru_cache(maxsize=None)
def _hyena_tables(seq):
    half = seq // 2
    p = np.arange(half)[:, None]
    m = np.arange(half)[None, :]
    ang_e = np.pi * ((p * 2 * m) % (2 * seq)) / seq
    ang_o = np.pi * ((p * (2 * m + 1)) % (2 * seq)) / seq
    fwd = np.concatenate([np.cos(ang_e), np.cos(ang_o), np.sin(ang_e), np.sin(ang_o)], axis=1)
    inv = np.concatenate([np.cos(ang_e).T, -np.sin(ang_e).T, np.cos(ang_o).T, -np.sin(ang_o).T], axis=1)
    return fwd.astype(np.float32), inv.astype(np.float32)


def _hy_in_kernel(x_ref, xp_ref, xn_ref, mod_ref, ng_ref, w_ref, b_ref, cw_ref, cb_ref, x0_ref, z_ref, zb_ref):
    j = pl.program_id(1)
    g, sh, sc = ng_ref[0:1, :], mod_ref[0:1, :], mod_ref[1:2, :]
    tl = x_ref.shape[0]
    rows = jnp.concatenate([x_ref[...], xp_ref[...], xn_ref[...]], axis=0)
    u_all = jnp.dot(_norm_mod(rows, g, sh, sc).astype(bf16), w_ref[...], preferred_element_type=f32) + b_ref[...]
    u = u_all[:tl, :]
    prev_row = jnp.where(j == 0, 0.0, u_all[tl + 7:tl + 8, :])
    next_row = jnp.where(j == pl.num_programs(1) - 1, 0.0, u_all[tl + 8:tl + 9, :])
    u_prev, u_next = _shifted(u, prev_row, next_row)
    uc = u_prev * cw_ref[0:1, :] + u * cw_ref[1:2, :] + u_next * cw_ref[2:3, :] + cb_ref[...]
    x0_ref[...] = uc[:, :D_MODEL]
    z = uc[:, 2 * D_MODEL:] * uc[:, D_MODEL:2 * D_MODEL]
    z_ref[...] = z
    zb_ref[...] = z.astype(bf16)


def _alternating_sign(first, rows):
    t = first + lax.broadcasted_iota(jnp.int32, (rows, 1), 0)
    return (1 - 2 * (t & 1)).astype(f32)


def _parity_dft(rows, even, odd, half):
    d = functools.partial(jnp.dot, preferred_element_type=f32)
    return (d(rows[:, :half], even), d(rows[:, half:2 * half], odd),
            d(rows[:, 2 * half:3 * half], even), d(rows[:, 3 * half:], odd))


def _hy_filter_kernel(feat_ref, w1_ref, b1_ref, fq_ref, w2_ref, b2_ref, w3f_ref, w3b_ref, rate_ref,
                      tab_ref, o_ref, hdn_ref, *, seq):
    half = seq // 2

    @pl.when(pl.program_id(0) == 0)
    def _():
        hdn = jnp.sin(fq_ref[0:1, :] * (_dot3(feat_ref[...], w1_ref[...]) + b1_ref[...]))
        hdn_ref[...] = jnp.sin(fq_ref[1:2, :] * (_dot3(hdn, w2_ref[...]) + b2_ref[...]))

    hdn = hdn_ref[...]
    rows = lax.broadcasted_iota(jnp.int32, (seq, 1), 0)
    lag = jnp.where(rows < half, 2 * rows, 2 * (rows - half) + 1)
    win = jnp.exp(-(lag.astype(f32) * (1.0 / (seq - 1))) * rate_ref[...])
    k_fwd = _dot3(hdn, w3f_ref[...]) * win
    k_bwd = _dot3(hdn, w3b_ref[...]) * win
    both = k_fwd + k_bwd
    diff = k_fwd - k_bwd
    ec, oc, _, _ = _parity_dft(tab_ref[...], both[:half].astype(bf16), both[half:].astype(bf16), half)
    _, _, es, os_ = _parity_dft(tab_ref[...], diff[:half].astype(bf16), diff[half:].astype(bf16), half)
    sign = _alternating_sign(0, half)
    mid_re = jnp.sum(both[:half] * sign, axis=0, keepdims=True)
    mid_im = -jnp.sum(diff[half:] * sign, axis=0, keepdims=True)
    first = lax.broadcasted_iota(jnp.int32, (half, 1), 0) == 0
    wgt = jnp.where(first, 0.5 / seq, 1.0 / seq)
    o_ref[0] = (ec + oc) * wgt
    o_ref[1] = jnp.where(first, mid_re * (1.0 / seq), -(es + os_) * wgt)
    o_ref[2] = (ec - oc) * wgt
    o_ref[3] = jnp.where(first, mid_im * (1.0 / seq), (es - os_) * wgt)


def _hy_fwd_kernel(tab_ref, z_ref, ks_ref, o_ref, mid_ref, *, seq):
    i = pl.program_id(1)
    half = seq // 2
    tf = o_ref.shape[1]
    z_even, z_odd = z_ref[:, :D_MODEL], z_ref[:, D_MODEL:]
    ec, oc, es, os_ = _parity_dft(_table_rows(tab_ref, i, tf), z_even, z_odd, half)
    zre_lo, zim_lo, zre_hi, zim_hi = ec + oc, -(es + os_), ec - oc, es - os_
    kre_lo, kim_lo, kre_hi, kim_hi = ks_ref[0], ks_ref[1], ks_ref[2], ks_ref[3]
    yre_lo = zre_lo * kre_lo - zim_lo * kim_lo
    yim_lo = zre_lo * kim_lo + zim_lo * kre_lo
    yre_hi = zre_hi * kre_hi - zim_hi * kim_hi
    yim_hi = zre_hi * kim_hi + zim_hi * kre_hi

    @pl.when(i == 0)
    def _():
        sign = _alternating_sign(0, half)
        mid_re = jnp.sum(z_even.astype(f32) * sign, axis=0, keepdims=True)
        mid_im = -jnp.sum(z_odd.astype(f32) * sign, axis=0, keepdims=True)
        k_re, k_im = ks_ref[1, 0:1, :], ks_ref[3, 0:1, :]
        mid_ref[0:1, :] = mid_re * k_re - mid_im * k_im
        mid_ref[1:2, :] = mid_re * k_im + mid_im * k_re

    @pl.when(i != 0)
    def _():
        mid_ref[...] = jnp.zeros_like(mid_ref)

    packed = (i * tf + lax.broadcasted_iota(jnp.int32, (tf, 1), 0)) == 0
    o_ref[0] = (yre_lo + yre_hi).astype(bf16)
    o_ref[1] = jnp.where(packed, mid_ref[0:1, :], yim_lo - yim_hi).astype(bf16)
    o_ref[2] = (yre_lo - yre_hi).astype(bf16)
    o_ref[3] = jnp.where(packed, mid_ref[1:2, :], yim_lo + yim_hi).astype(bf16)


def _hy_inv_kernel(tab_ref, ys_ref, z_ref, x0_ref, x_ref, mod_ref, ng_ref, d_ref, wo_ref, bo_ref, o_ref, *, seq):
    i = pl.program_id(1)
    tm = x_ref.shape[0]
    half = seq // 2
    D = D_MODEL
    rows = _table_rows(tab_ref, i, tm)
    sign = _alternating_sign(i * tm, tm)
    conv_even = (jnp.dot(rows[:, :seq], ys_ref[:seq, :], preferred_element_type=f32)
                 + sign * ys_ref[half:half + 1, :].astype(f32))
    conv_odd = (jnp.dot(rows[:, seq:], ys_ref[seq:, :], preferred_element_type=f32)
                - sign * ys_ref[3 * half:3 * half + 1, :].astype(f32))
    d = d_ref[...]
    v_even = ((conv_even + z_ref[:, :D] * d) * x0_ref[:, :D]).astype(bf16)
    v_odd = ((conv_odd + z_ref[:, D:] * d) * x0_ref[:, D:]).astype(bf16)
    out = jnp.dot(jnp.concatenate([v_even, v_odd], axis=0), wo_ref[...], preferred_element_type=f32) + bo_ref[...]
    g, gate = ng_ref[1:2, :], mod_ref[2:3, :]
    o_ref[:, :D] = _residual(x_ref[:, :D], out[:tm, :], g, gate)
    o_ref[:, D:] = _residual(x_ref[:, D:], out[tm:, :], g, gate)


def _hyena_layer(x, trunk, layer, mod, norm_g, p):
    nb, seq = trunk.batch, trunk.seq
    tl = _tile_rows(trunk, 256)
    x8 = x.reshape(nb, seq // 8, 8, D_MODEL)
    prev_spec, next_spec = _halo_specs(trunk, tl)
    x0, z, zb = pl.pallas_call(
        _hy_in_kernel,
        grid=(nb, seq // tl),
        in_specs=[_x_spec(tl), prev_spec, next_spec, _mod_spec(trunk, layer), _ng_spec(layer),
                  _resident((D_MODEL, 3 * D_MODEL)), _resident((1, 3 * D_MODEL)),
                  _resident((3, 3 * D_MODEL)), _resident((1, 3 * D_MODEL))],
        out_specs=[_x_spec(tl)] * 3,
        out_shape=[jax.ShapeDtypeStruct(x.shape, f32)] * 2 + [jax.ShapeDtypeStruct(x.shape, bf16)],
        compiler_params=_params("parallel", "parallel"),
    )(x, x8, x8, mod, norm_g, p["w_in"], p["b_in"], p["conv_w"], p["conv_b"])

    half = seq // 2
    fwd_table, inv_table = (_mxu_table(t) for t in _hyena_tables(seq))
    table_spec = _resident((half, 2 * seq))
    tn = 256
    nblk = D_MODEL // tn
    ks = pl.pallas_call(
        functools.partial(_hy_filter_kernel, seq=seq),
        grid=(nblk,),
        in_specs=[_resident((seq, HY_EMB_PAD)), _resident((HY_EMB_PAD, HY_FILT)), _resident((1, HY_FILT)),
                  _resident((2, HY_FILT)), _resident((HY_FILT, HY_FILT)), _resident((1, HY_FILT)),
                  pl.BlockSpec((HY_FILT, tn), lambda j: (0, j)),
                  pl.BlockSpec((HY_FILT, tn), lambda j: (0, nblk + j)),
                  pl.BlockSpec((1, tn), lambda j: (0, j)),
                  table_spec],
        out_specs=pl.BlockSpec((4, half, tn), lambda j: (0, 0, j)),
        out_shape=jax.ShapeDtypeStruct((4, half, D_MODEL), f32),
        scratch_shapes=[pltpu.VMEM((seq, HY_FILT), f32)],
        compiler_params=_params("arbitrary"),
    )(jnp.asarray(_hyena_features(seq)), p["f_w1"], p["f_b1"], p["f_freq"], p["f_w2"], p["f_b2"],
      p["f_w3"], p["f_w3"], jnp.asarray(_hyena_decay_rates()), fwd_table)

    paired = lambda t: t.reshape(nb, half, 2 * D_MODEL)
    tf = min(half, 512)
    mode = pl.Buffered(1) if half // tf > 1 else None
    ys = pl.pallas_call(
        functools.partial(_hy_fwd_kernel, seq=seq),
        grid=(nb, half // tf),
        in_specs=[table_spec,
                  pl.BlockSpec((None, half, 2 * D_MODEL), lambda b, i: (b, 0, 0), pipeline_mode=mode),
                  pl.BlockSpec((4, tf, D_MODEL), lambda b, i: (0, i, 0))],
        out_specs=pl.BlockSpec((None, 4, tf, D_MODEL), lambda b, i: (b, 0, i, 0)),
        out_shape=jax.ShapeDtypeStruct((nb, 4, half, D_MODEL), bf16),
        scratch_shapes=[pltpu.VMEM((2, D_MODEL), f32)],
        compiler_params=_params("parallel", "arbitrary"),
    )(fwd_table, paired(zb), ks)

    tm = min(half, 256)
    mode = pl.Buffered(1) if half // tm > 2 else None
    pair_tile = pl.BlockSpec((None, tm, 2 * D_MODEL), lambda b, i: (b, i, 0))
    out = pl.pallas_call(
        functools.partial(_hy_inv_kernel, seq=seq),
        grid=(nb, half // tm),
        in_specs=[table_spec,
                  pl.BlockSpec((None, 2 * seq, D_MODEL), lambda b, i: (b, 0, 0), pipeline_mode=mode),
                  pair_tile, pair_tile, pair_tile, _mod_spec(trunk, layer), _ng_spec(layer),
                  _resident((1, D_MODEL)), _resident((D_MODEL, D_MODEL)), _resident((1, D_MODEL))],
        out_specs=pair_tile,
        out_shape=jax.ShapeDtypeStruct((nb, half, 2 * D_MODEL), f32),
        compiler_params=_params("parallel", "parallel"),
    )(inv_table, ys.reshape(nb, 2 * seq, D_MODEL), paired(z), paired(x0), paired(x), mod, norm_g,
      p["d"], p["w_out"], p["b_out"])
    return out.reshape(x.shape)


def _head_sum(t, ones_pair):
    hi, lo = _split(t)
    cols = []
    for p in range(D_MODEL // PAIR_W):
        sl = slice(p * PAIR_W, (p + 1) * PAIR_W)
        cols.append(jnp.dot(hi[:, sl], ones_pair, preferred_element_type=f32)
                    + jnp.dot(lo[:, sl], ones_pair, preferred_element_type=f32))
    return jnp.concatenate(cols, axis=1)


def _rw_pre_kernel(x_ref, xp_ref, xn_ref, mod_ref, ng_ref, mix_ref, wr_ref, wk_ref, wv_ref, g1_ref, g2_ref,
                   w0_ref, w1_ref, w2_ref, a0_ref, a1_ref, a2_ref, kkp_ref, kap_ref, rkp_ref, ones_ref,
                   r_ref, k_ref, v_ref, kk_ref, g_ref, lwf_ref, lwb_ref, af_ref, ab_ref, bn_ref):
    j = pl.program_id(1)
    g, sh, sc = ng_ref[0:1, :], mod_ref[0:1, :], mod_ref[1:2, :]
    h = _norm_mod(x_ref[...], g, sh, sc)
    prev_row = jnp.where(j == 0, 0.0, _norm_mod(xp_ref[...], g, sh, sc)[7:8, :])
    next_row = jnp.where(j == pl.num_programs(1) - 1, 0.0, _norm_mod(xn_ref[...], g, sh, sc)[0:1, :])
    h_prev, h_next = _shifted(h, prev_row, next_row)
    xx = 0.5 * (h_prev + h_next) - h

    def mixed(m):
        return (h + xx * mix_ref[m:m + 1, :]).astype(bf16)

    r = jnp.dot(mixed(0), wr_ref[...], preferred_element_type=f32)
    xw = mixed(1)
    k = jnp.dot(mixed(2), wk_ref[...], preferred_element_type=f32)
    v = jnp.dot(mixed(3), wv_ref[...], preferred_element_type=f32)
    xa = mixed(4)
    r_ref[...] = r
    k_ref[...] = k
    v_ref[...] = v
    gate = _sigmoid(jnp.dot(mixed(5), g1_ref[...], preferred_element_type=f32))
    g_ref[...] = jnp.dot(gate.astype(bf16), g2_ref[...], preferred_element_type=f32)
    ones = ones_ref[...]
    kk = k * kkp_ref[...]
    kk_ref[...] = kk * lax.rsqrt(jnp.maximum(_head_sum(kk * kk, ones), 1e-24))
    a_sum = None
    for dd, (lw_ref, a_ref) in enumerate(((lwf_ref, af_ref), (lwb_ref, ab_ref))):
        lora = jnp.tanh(jnp.dot(xw, w1_ref[dd], preferred_element_type=f32))
        wl = w0_ref[dd:dd + 1, :] + jnp.dot(lora.astype(bf16), w2_ref[dd], preferred_element_type=f32)
        lw_ref[...] = -math.exp(-0.5) * _sigmoid(wl)
        al = jnp.dot(xa, a1_ref[dd], preferred_element_type=f32)
        a = _sigmoid(a0_ref[dd:dd + 1, :] + jnp.dot(al.astype(bf16), a2_ref[dd], preferred_element_type=f32))
        a_ref[...] = a
        a_sum = a if a_sum is None else a_sum + a
    kd_sum = k * (2.0 + (a_sum - 2.0) * kap_ref[...])
    bn_ref[...] = _head_sum(r * kd_sum * rkp_ref[...], ones) * v


def _stack_heads(x, first_head):
    return jnp.concatenate([jnp.where(first_head, x, 0.0), jnp.where(first_head, 0.0, x)], axis=0).astype(bf16)


def _dot_nt(a, b):
    return lax.dot_general(a, b, (((1,), (1,)), ((), ())), preferred_element_type=f32)


def _dot_tn(a, b):
    return lax.dot_general(a, b, (((0,), (0,)), ((), ())), preferred_element_type=f32)


def _rw_scan_kernel(r_ref, k_ref, v_ref, kk_ref, lw_ref, a_ref, kap_ref, s0_ref, o_ref, s_ref, *, reverse):
    c = pl.program_id(1)
    C, W, G = SCAN_CHUNK, PAIR_W, SCAN_GROUP

    @pl.when(c == 0)
    def _():
        s_ref[...] = s0_ref[...]

    row = lax.broadcasted_iota(jnp.int32, (C, W), 0)
    lane = lax.broadcasted_iota(jnp.int32, (C, W), 1)
    col = lane & (C - 1)
    first_head = lane < RW_HEAD
    before = (col > row) if reverse else (col < row)
    before_eq = before | (col == row)
    eye = (col == row).astype(f32)
    row_w = lax.broadcasted_iota(jnp.int32, (W, W), 0)
    lane_w = lax.broadcasted_iota(jnp.int32, (W, W), 1)
    same_head = (row_w >= RW_HEAD) == (lane_w >= RW_HEAD)
    diag_w = row_w == lane_w
    merge_masks = []
    s = 1
    while s < C:
        sh = s.bit_length() - 1
        same_pair = (row >> (sh + 1)) == (col >> (sh + 1))
        later, earlier = ((row >> sh) & 1, (col >> sh) & 1)
        merge_masks.append(same_pair & (((later == 0) & (earlier == 1)) if reverse
                                        else ((later == 1) & (earlier == 0))))
        s *= 2

    row_c = lax.broadcasted_iota(jnp.int32, (G * C, G * C), 0)
    col_c = lax.broadcasted_iota(jnp.int32, (G * C, G * C), 1)
    chunk_bits = C.bit_length() - 1
    same_chunk = (row_c >> chunk_bits) == (col_c >> chunk_bits)
    tri = (same_chunk & ((col_c >= row_c) if reverse else (col_c <= row_c))).astype(bf16)
    lw_all = lw_ref[...]
    cum_all = _dot_exact_rhs_left(tri, lw_all)
    n_pairs = D_MODEL // W

    def pairs(x):
        return [x[g * C:(g + 1) * C, p * W:(p + 1) * W] for g in range(G) for p in range(n_pairs)]

    def per_chunk_rows(rows):
        return [rows[g][:, p * W:(p + 1) * W] for g in range(G) for p in range(n_pairs)]

    def each(fn, *lists):
        return [fn(*args) for args in zip(*lists)]

    def stack(x):
        return _stack_heads(x, first_head)

    def mm(x, y):
        return jnp.dot(x.astype(bf16), stack(y), preferred_element_type=f32)

    def mm2(x, y0, y1):
        return jnp.dot(x.astype(bf16), jnp.concatenate([stack(y0), stack(y1)], axis=1),
                       preferred_element_type=f32)

    r, k, v, kk, a = (pairs(ref[...]) for ref in (r_ref, k_ref, v_ref, kk_ref, a_ref))
    lw, cum = pairs(lw_all), pairs(cum_all)
    tot = per_chunk_rows([jnp.sum(lw_all[g * C:(g + 1) * C, :], axis=0, keepdims=True) for g in range(G)])
    ka = per_chunk_rows([kap_ref[...]] * G)
    kd = each(lambda k_, a_, ka_: k_ * (1.0 + (a_ - 1.0) * ka_), k, a, ka)
    b = each(lambda kk_, a_: kk_ * a_, kk, a)
    e_in = each(jnp.exp, cum)
    e_out = each(lambda c_: jnp.exp(-c_), cum)
    e_end = each(lambda t_, c_: jnp.exp(t_ - c_), tot, cum)
    kap = each(lambda kk_, c_, l_: kk_ * jnp.exp(c_ - l_), kk, cum, lw)
    mul = lambda x_, y_: x_ * y_
    kt, bt, rt, kh, bh = each(mul, kd, e_out), each(mul, b, e_out), each(mul, r, e_in), each(mul, kd, e_end), \
        each(mul, b, e_end)

    lhs = each(lambda x_, y_: jnp.concatenate([x_, y_], axis=0).astype(bf16), kap, rt)
    gram = each(lambda l_, y0_, y1_: _dot_nt(l_, jnp.concatenate([stack(y0_), stack(y1_)], axis=0)), lhs, kt, bt)
    m_kk = each(lambda g_: jnp.where(before, g_[:C, :W], 0.0), gram)
    n_kb = each(lambda g_: jnp.where(before, g_[:C, W:], 0.0), gram)
    a_qk = each(lambda g_: jnp.where(before_eq, g_[C:, :W], 0.0), gram)
    a_qb = each(lambda g_: jnp.where(before_eq, g_[C:, W:], 0.0), gram)

    tinv = each(lambda n_: eye - jnp.where(merge_masks[0], n_, 0.0), n_kb)
    for off in merge_masks[1:]:
        tn = each(lambda t_, n_: mm(t_, jnp.where(off, n_, 0.0)), tinv, n_kb)
        tinv = each(lambda t_, tn_: t_ - mm(tn_, t_), tinv, tn)

    mv = each(mm, m_kk, v)
    tk = each(mm2, tinv, kap, mv)
    kt_c, w1 = each(lambda t_: t_[:, :W], tk), each(lambda t_: t_[:, W:], tk)
    qa = each(mm2, a_qb, kt_c, w1)
    av = each(mm, a_qk, v)
    q_eff = each(lambda rt_, qa_: rt_ - qa_[:, :W], rt, qa)
    o_intra = each(lambda av_, qa_: av_ - qa_[:, W:], av, qa)
    kb = each(lambda x_, b_: _dot_tn(x_.astype(bf16), b_.astype(bf16)), kt_c, bh)
    phi = each(lambda t_, kb_: jnp.where(same_head, jnp.where(diag_w, jnp.exp(t_), 0.0) - kb_, 0.0), tot, kb)
    ds_full = each(lambda v_, w_, kh_, bh_: _dot_tn(jnp.concatenate([v_, -w_], axis=0).astype(bf16),
                                                    jnp.concatenate([kh_, bh_], axis=0).astype(bf16)),
                   v, w1, kh, bh)
    d_s = each(lambda d_: jnp.where(first_head, d_[:RW_HEAD, :], d_[RW_HEAD:, :]), ds_full)
    state = [s_ref[:, p * W:(p + 1) * W] for p in range(n_pairs)]
    for g in (range(G - 1, -1, -1) if reverse else range(G)):
        sel = slice(g * n_pairs, (g + 1) * n_pairs)
        o = each(lambda q_, s_, oi_: _dot_nt(q_.astype(bf16), stack(s_)) + oi_, q_eff[sel], state, o_intra[sel])
        state = each(lambda s_, ph_, ds_: jnp.dot(s_.astype(bf16), ph_.astype(bf16),
                                                  preferred_element_type=f32) + ds_, state, phi[sel], d_s[sel])
        o_ref[g * C:(g + 1) * C, :] = jnp.concatenate(o, axis=1)
    s_ref[...] = jnp.concatenate(state, axis=1)


def _rw_post_kernel(of_ref, ob_ref, bn_ref, g_ref, x_ref, mod_ref, ng_ref, lw_ref, lb_ref, ones_ref,
                    wo_ref, o_ref):
    o = of_ref[...] + ob_ref[...]
    ones = ones_ref[...]
    inv_n = 1.0 / RW_HEAD
    dev = o - _head_sum(o, ones) * inv_n
    var = _head_sum(dev * dev, ones) * inv_n
    on = dev * lax.rsqrt(var + RW_LN_EPS) * lw_ref[...] + lb_ref[...] + bn_ref[...]
    y = jnp.dot((on * g_ref[...]).astype(bf16), wo_ref[...], preferred_element_type=f32)
    o_ref[...] = _residual(x_ref[...], y, ng_ref[1:2, :], mod_ref[2:3, :])


@functools.lru_cache(maxsize=None)
def _pair_ones():
    idx = np.arange(PAIR_W) // RW_HEAD
    return np.asarray(idx[:, None] == idx[None, :], dtype=bf16)


def _rwkv_layer(x, trunk, layer, mod, norm_g, p, s0_fwd, s0_bwd):
    nb, seq = trunk.batch, trunk.seq
    tl = _tile_rows(trunk, 256)
    x8 = x.reshape(nb, seq // 8, 8, D_MODEL)
    prev_spec, next_spec = _halo_specs(trunk, tl)
    lora_w, lora_a, lora_g = p["w1"].shape[-1], p["a1"].shape[-1], p["g1"].shape[-1]
    ones = jnp.asarray(_pair_ones())
    outs = pl.pallas_call(
        _rw_pre_kernel,
        grid=(nb, seq // tl),
        in_specs=[_x_spec(tl), prev_spec, next_spec, _mod_spec(trunk, layer), _ng_spec(layer),
                  _resident((6, D_MODEL)),
                  _resident((D_MODEL, D_MODEL)), _resident((D_MODEL, D_MODEL)), _resident((D_MODEL, D_MODEL)),
                  _resident((D_MODEL, lora_g)), _resident((lora_g, D_MODEL)),
                  _resident((2, D_MODEL)), _resident((2, D_MODEL, lora_w)), _resident((2, lora_w, D_MODEL)),
                  _resident((2, D_MODEL)), _resident((2, D_MODEL, lora_a)), _resident((2, lora_a, D_MODEL)),
                  _resident((1, D_MODEL)), _resident((1, D_MODEL)), _resident((1, D_MODEL)),
                  _resident((PAIR_W, PAIR_W))],
        out_specs=[_x_spec(tl)] * 10,
        out_shape=[jax.ShapeDtypeStruct(x.shape, f32)] * 10,
        compiler_params=_params("parallel", "parallel"),
    )(x, x8, x8, mod, norm_g, p["mix"], p["wr"], p["wk"], p["wv"], p["g1"], p["g2"],
      p["w0"], p["w1"], p["w2"], p["a0"], p["a1"], p["a2"], p["kk"], p["ka"], p["rk"], ones)
    r, k, v, kk, gate, lw_f, lw_b, a_f, a_b, bonus = outs

    def state_in(s):
        return s.transpose(0, 2, 1, 3).reshape(nb, RW_HEAD, D_MODEL)

    def state_out(s):
        return s.reshape(nb, RW_HEAD, RW_HEADS, RW_HEAD).transpose(0, 2, 1, 3)

    rows = SCAN_CHUNK * SCAN_GROUP
    n_chunks = seq // rows
    state_spec = pl.BlockSpec((None, RW_HEAD, D_MODEL), lambda b, c: (b, 0, 0))
    results = []
    for reverse, lw, a, s0 in ((False, lw_f, a_f, s0_fwd), (True, lw_b, a_b, s0_bwd)):
        if reverse:
            chunk_spec = pl.BlockSpec((None, rows, D_MODEL), lambda b, c: (b, n_chunks - 1 - c, 0))
        else:
            chunk_spec = pl.BlockSpec((None, rows, D_MODEL), lambda b, c: (b, c, 0))
        results.append(pl.pallas_call(
            functools.partial(_rw_scan_kernel, reverse=reverse),
            grid=(nb, n_chunks),
            in_specs=[chunk_spec] * 6 + [_resident((1, D_MODEL)), state_spec],
            out_specs=[chunk_spec, state_spec],
            out_shape=[jax.ShapeDtypeStruct(x.shape, f32),
                       jax.ShapeDtypeStruct((nb, RW_HEAD, D_MODEL), f32)],
            compiler_params=_params("parallel", "arbitrary"),
        )(r, k, v, kk, lw, a, p["ka"], state_in(s0)))
    (o_f, s_f), (o_b, s_b) = results

    y = pl.pallas_call(
        _rw_post_kernel,
        grid=(nb, seq // tl),
        in_specs=[_x_spec(tl)] * 5 + [_mod_spec(trunk, layer), _ng_spec(layer),
                                      _resident((1, D_MODEL)), _resident((1, D_MODEL)),
                                      _resident((PAIR_W, PAIR_W)), _resident((D_MODEL, D_MODEL))],
        out_specs=_x_spec(tl),
        out_shape=jax.ShapeDtypeStruct(x.shape, f32),
        compiler_params=_params("parallel", "parallel"),
    )(o_f, o_b, bonus, gate, x, mod, norm_g, p["lnx_w"], p["lnx_b"], ones, p["wo"])
    return y, state_out(s_f), state_out(s_b)


def kernel(x_prompt, x_sample, state_wkv_fwd, state_wkv_bwd, c, c_ctx, ada_w, ada_b, norm_g, ffn_w1, ffn_w3, ffn_w2, fn_w_out, fn_b_out, hy_w_in, hy_b_in, hy_conv_w, hy_conv_b, hy_f_w1, hy_f_b1, hy_f_freq, hy_f_w2, hy_f_b2, hy_f_w3, hy_d, hy_w_out, hy_b_out, rw_mix, rw_wr, rw_wk, rw_wv, rw_wo, rw_w0, rw_w1, rw_w2, rw_a0, rw_a1, rw_a2, rw_g1, rw_g2, rw_kk, rw_ka, rw_rk, rw_lnx_w, rw_lnx_b):
    n_ctx, n_dec = x_prompt.shape[0], x_sample.shape[0]
    assert 1 + n_dec <= MOD_ROWS
    trunks = (Trunk(n_ctx, x_prompt.shape[1], 0, False), Trunk(n_dec, x_sample.shape[1], 1, True))
    cond = jnp.zeros((MOD_ROWS, D_MODEL), f32).at[0].set(c_ctx).at[1:1 + n_dec].set(c)
    mod = _adaln(cond, ada_w, ada_b)

    n_rwkv = rw_w0.shape[0]
    zero_state = jnp.zeros((n_ctx, n_rwkv, RW_HEADS, RW_HEAD, RW_HEAD), f32)
    states = ((zero_state, zero_state), (state_wkv_fwd, state_wkv_bwd))
    xs = [x_prompt, x_sample]
    new_fwd, new_bwd = [], []
    for i in range(DEPTH):
        kind, j = i % N_MIXERS, i // N_MIXERS
        if kind == 0:
            w_out, b_out = fn_w_out[j].astype(bf16), fn_b_out[j][None]
            xs = [_fourier_layer(x, t, i, mod, norm_g, w_out, b_out) for x, t in zip(xs, trunks)]
        elif kind == 1:
            w1_pad = jnp.zeros((HY_EMB_PAD, HY_FILT), f32).at[:HY_EMB].set(hy_f_w1[j])
            p = dict(w_in=hy_w_in[j].astype(bf16), b_in=hy_b_in[j][None], conv_w=hy_conv_w[j],
                     conv_b=hy_conv_b[j][None], f_w1=w1_pad, f_b1=hy_f_b1[j][None], f_freq=hy_f_freq[j],
                     f_w2=hy_f_w2[j], f_b2=hy_f_b2[j][None], f_w3=hy_f_w3[j], d=hy_d[j][None],
                     w_out=hy_w_out[j].astype(bf16), b_out=hy_b_out[j][None])
            xs = [_hyena_layer(x, t, i, mod, norm_g, p) for x, t in zip(xs, trunks)]
        else:
            per_head = lambda t: t.reshape(1, D_MODEL)
            p = dict(mix=rw_mix[j], wr=rw_wr[j].astype(bf16), wk=rw_wk[j].astype(bf16),
                     wv=rw_wv[j].astype(bf16), wo=rw_wo[j].astype(bf16), w0=rw_w0[j],
                     w1=rw_w1[j].astype(bf16), w2=rw_w2[j].astype(bf16), a0=rw_a0[j],
                     a1=rw_a1[j].astype(bf16), a2=rw_a2[j].astype(bf16), g1=rw_g1[j].astype(bf16),
                     g2=rw_g2[j].astype(bf16), kk=per_head(rw_kk[j]), ka=per_head(rw_ka[j]),
                     rk=per_head(rw_rk[j]), lnx_w=rw_lnx_w[j][None], lnx_b=rw_lnx_b[j][None])
            outs = [_rwkv_layer(x, t, i, mod, norm_g, p, sf[:, j], sb[:, j])
                    for x, t, (sf, sb) in zip(xs, trunks, states)]
            xs = [o[0] for o in outs]
            new_fwd.append(outs[0][1])
            new_bwd.append(outs[0][2])
        w1, w3, w2 = ffn_w1[i].astype(bf16), ffn_w3[i].astype(bf16), ffn_w2[i].astype(bf16)
        xs = [_ffn(x, t, i, mod, norm_g, w1, w3, w2) for x, t in zip(xs, trunks)]
    return xs[0], xs[1], jnp.stack(new_fwd, axis=1), jnp.stack(new_bwd, axis=1)
```

```python
import functools
import math
from typing import NamedTuple

import numpy as np
import jax
import jax.numpy as jnp
from jax import lax
from jax.experimental import pallas as pl
from jax.experimental.pallas import tpu as pltpu

f32 = jnp.float32
bf16 = jnp.bfloat16

D_MODEL = 1024
DEPTH = 4
N_MIXERS = 3
D_FF = 2816
NORM_EPS = 1e-6
FN_GROUP_W = 256
FN_HALO = 16
HY_EMB = 33
HY_EMB_PAD = 128
HY_BANDS = 16
HY_FILT = 64
HY_TARGET = 1e-2
HY_FAST = 0.3
HY_SLOW = 1.5
RW_HEAD = 64
RW_HEADS = 16
RW_LN_EPS = 64e-5
SCAN_CHUNK = 64
SCAN_GROUP = 2
PAIR_W = 2 * RW_HEAD
assert SCAN_CHUNK == RW_HEAD, "the scan keeps (chunk, chunk) and (chunk, head) tiles in one lane layout"
MOD_ROWS = 8
VMEM_LIMIT = 56 * 1024 * 1024


class Trunk(NamedTuple):
    batch: int
    seq: int
    mod_base: int
    per_batch_mod: bool


def _params(*sem):
    return pltpu.CompilerParams(dimension_semantics=sem, vmem_limit_bytes=VMEM_LIMIT)


def _resident(shape):
    nd = len(shape)
    return pl.BlockSpec(shape, lambda *_: (0,) * nd, pipeline_mode=pl.Buffered(1))


def _dot(a, b):
    return jnp.dot(a.astype(bf16), b.astype(bf16), preferred_element_type=f32)


def _split(x):
    hi = x.astype(bf16)
    lo = (x - hi.astype(f32)).astype(bf16)
    return hi, lo


def _dot3(a, b):
    ah, al = _split(a)
    bh, bl = _split(b)
    d = functools.partial(jnp.dot, preferred_element_type=f32)
    return d(ah, bh) + (d(ah, bl) + d(al, bh))


def _dot_exact_rhs_left(a_exact, b):
    b0 = b.astype(bf16)
    r1 = b - b0.astype(f32)
    b1 = r1.astype(bf16)
    b2 = (r1 - b1.astype(f32)).astype(bf16)
    d = functools.partial(jnp.dot, preferred_element_type=f32)
    return d(a_exact, b0) + (d(a_exact, b1) + d(a_exact, b2))


def _sigmoid(x):
    return 0.5 * jnp.tanh(0.5 * x) + 0.5


def _norm_mod(x, g, shift, scale):
    ms = jnp.mean(x * x, axis=-1, keepdims=True)
    return (x * lax.rsqrt(ms + NORM_EPS)) * (g * (1.0 + scale)) + shift


def _residual(x, y, g, gate):
    ms = jnp.mean(y * y, axis=-1, keepdims=True)
    return x + gate * (y * lax.rsqrt(ms + NORM_EPS) * g)


def _shifted(h, prev_row, next_row):
    rows = h.shape[0]
    ridx = lax.broadcasted_iota(jnp.int32, h.shape, 0)
    h_prev = jnp.where(ridx == 0, prev_row, pltpu.roll(h, 1, 0))
    h_next = jnp.where(ridx == rows - 1, next_row, pltpu.roll(h, rows - 1, 0))
    return h_prev, h_next


def _tile_rows(trunk, cap):
    return min(trunk.seq, cap)


def _x_spec(tl):
    return pl.BlockSpec((None, tl, D_MODEL), lambda b, j: (b, j, 0))


def _halo_specs(trunk, tl):
    g = tl // 8
    last = trunk.seq // 8 - 1
    prev = pl.BlockSpec((None, None, 8, D_MODEL), lambda b, j: (b, jnp.maximum(j * g - 1, 0), 0, 0))
    nxt = pl.BlockSpec((None, None, 8, D_MODEL), lambda b, j: (b, jnp.minimum((j + 1) * g, last), 0, 0))
    return prev, nxt


def _mod_spec(trunk, layer):
    if trunk.per_batch_mod:
        return pl.BlockSpec((None, None, 6, D_MODEL), lambda b, *_: (layer, trunk.mod_base + b, 0, 0))
    return pl.BlockSpec((None, None, 6, D_MODEL), lambda b, *_: (layer, trunk.mod_base, 0, 0))


def _ng_spec(layer):
    return pl.BlockSpec((None, 4, D_MODEL), lambda *_: (layer, 0, 0))


def _mod_kernel(c_ref, w_ref, b_ref, o_ref):
    c = c_ref[...]
    o_ref[...] = _dot3(c * _sigmoid(c), w_ref[...]) + b_ref[...]


def _adaln(cond, ada_w, ada_b):
    tn = 1536
    out = pl.pallas_call(
        _mod_kernel,
        grid=(DEPTH, 6 * D_MODEL // tn),
        in_specs=[pl.BlockSpec((MOD_ROWS, D_MODEL), lambda l, j: (0, 0)),
                  pl.BlockSpec((None, D_MODEL, tn), lambda l, j: (l, 0, j)),
                  pl.BlockSpec((None, 1, tn), lambda l, j: (l, 0, j))],
        out_specs=pl.BlockSpec((None, MOD_ROWS, tn), lambda l, j: (l, 0, j)),
        out_shape=jax.ShapeDtypeStruct((DEPTH, MOD_ROWS, 6 * D_MODEL), f32),
        compiler_params=_params("arbitrary", "arbitrary"),
    )(cond, ada_w, ada_b.reshape(DEPTH, 1, 6 * D_MODEL))
    return out.reshape(DEPTH, MOD_ROWS, 6, D_MODEL)


def _ffn_kernel(x_ref, mod_ref, ng_ref, w1_ref, w3_ref, w2_ref, o_ref):
    x = x_ref[...]
    h = _norm_mod(x, ng_ref[2:3, :], mod_ref[3:4, :], mod_ref[4:5, :]).astype(bf16)
    a = jnp.dot(h, w1_ref[...], preferred_element_type=f32)
    b = jnp.dot(h, w3_ref[...], preferred_element_type=f32)
    gated = (a * _sigmoid(a) * b).astype(bf16)
    y = jnp.dot(gated, w2_ref[...], preferred_element_type=f32)
    o_ref[...] = _residual(x, y, ng_ref[3:4, :], mod_ref[5:6, :])


def _merge_sequences(x, trunk):
    if trunk.per_batch_mod:
        return x, trunk
    return x.reshape(1, -1, D_MODEL), Trunk(1, trunk.batch * trunk.seq, trunk.mod_base, False)


def _ffn(x, trunk, layer, mod, norm_g, w1, w3, w2):
    shape = x.shape
    x, trunk = _merge_sequences(x, trunk)
    tl = _tile_rows(trunk, 512)
    return pl.pallas_call(
        _ffn_kernel,
        grid=(trunk.batch, trunk.seq // tl),
        in_specs=[_x_spec(tl), _mod_spec(trunk, layer), _ng_spec(layer),
                  _resident((D_MODEL, D_FF)), _resident((D_MODEL, D_FF)), _resident((D_FF, D_MODEL))],
        out_specs=_x_spec(tl),
        out_shape=jax.ShapeDtypeStruct(x.shape, f32),
        compiler_params=_params("parallel", "parallel"),
    )(x, mod, norm_g, w1, w3, w2).reshape(shape)


@functools.lru_cache(maxsize=None)
def _channel_dft():
    w = FN_GROUP_W
    idx = np.arange(w)
    ang = 2.0 * np.pi * ((idx[:, None] * idx[None, :]) % w) / w
    return (np.concatenate([np.cos(ang), np.sin(ang)], axis=1) / math.sqrt(w)).astype(np.float32)


def _table_rows(cs_ref, tile, rows):
    return cs_ref[pl.ds(pl.multiple_of(tile * rows, rows), rows), :]


def _mxu_table(table):
    return jnp.asarray(table).astype(bf16)


def _fn_a_kernel(x_ref, mod_ref, ng_ref, w_ref, a_ref, b_ref):
    h = _norm_mod(x_ref[...], ng_ref[0:1, :], mod_ref[0:1, :], mod_ref[1:2, :]).astype(bf16)
    w = FN_GROUP_W
    for g in range(D_MODEL // w):
        ab = jnp.dot(h[:, g * w:(g + 1) * w], w_ref[...], preferred_element_type=f32).astype(bf16)
        a_ref[:, g * w:(g + 1) * w] = ab[:, :w]
        b_ref[:, g * w:(g + 1) * w] = ab[:, w:]


@functools.lru_cache(maxsize=None)
def _fourier_half_table(seq):
    k = np.arange(seq // 2 + FN_HALO)
    t = np.arange(seq)
    ang = 2.0 * np.pi * ((k[:, None] * t[None, :]) % seq) / seq
    return (np.concatenate([np.cos(ang), np.sin(ang)], axis=1) / math.sqrt(seq)).astype(np.float32)


@functools.lru_cache(maxsize=None)
def _row_reversal(tm):
    out = np.zeros((tm, tm + FN_HALO), np.float32)
    out[np.arange(tm), tm - np.arange(tm)] = 1.0
    return np.asarray(out, dtype=bf16)


def _fn_b_kernel(cs_ref, rev_ref, a_ref, b_ref, x_ref, mod_ref, ng_ref, wo_ref, bo_ref, o_ref, mirror_ref, *,
                 seq):
    s = pl.program_id(1)
    half_tiles = pl.num_programs(1) // 2
    n_seq, tm = x_ref.shape[0], x_ref.shape[1]
    src_rows = tm + FN_HALO

    def finish(f_bf16):
        f = jnp.concatenate(f_bf16, axis=0)
        y = jnp.dot(f, wo_ref[...], preferred_element_type=f32) + bo_ref[...]
        x = x_ref[...].reshape(n_seq * tm, D_MODEL)
        o_ref[...] = _residual(x, y, ng_ref[1:2, :], mod_ref[2:3, :]).reshape(n_seq, tm, D_MODEL)

    @pl.when(s < half_tiles)
    def _():
        first = pl.multiple_of(s * tm, tm)
        rows = cs_ref[pl.ds(first, src_rows), :]
        out = []
        for i in range(n_seq):
            p = jnp.dot(rows[:, :seq], a_ref[i], preferred_element_type=f32)
            q = jnp.dot(rows[:, seq:], b_ref[i], preferred_element_type=f32)
            mirror_ref[i, pl.ds(first, src_rows), :] = (p + q).astype(bf16)
            out.append((p - q)[:tm, :].astype(bf16))
        finish(out)

    @pl.when(s >= half_tiles)
    def _():
        first = pl.multiple_of((2 * half_tiles - 1 - s) * tm, tm)
        finish([jnp.dot(rev_ref[...], mirror_ref[i, pl.ds(first, src_rows), :],
                        preferred_element_type=f32).astype(bf16) for i in range(n_seq)])


def _fourier_layer(x, trunk, layer, mod, norm_g, w_out, b_out):
    nb, seq = trunk.batch, trunk.seq
    xm, merged = _merge_sequences(x, trunk)
    tl = _tile_rows(merged, 512)
    a, b = pl.pallas_call(
        _fn_a_kernel,
        grid=(merged.batch, merged.seq // tl),
        in_specs=[_x_spec(tl), _mod_spec(merged, layer), _ng_spec(layer),
                  _resident((FN_GROUP_W, 2 * FN_GROUP_W))],
        out_specs=[_x_spec(tl)] * 2,
        out_shape=[jax.ShapeDtypeStruct(xm.shape, bf16)] * 2,
        compiler_params=_params("parallel", "parallel"),
    )(xm, mod, norm_g, _mxu_table(_channel_dft()))

    tm = min(seq // 2, 512)
    half_rows = seq // 2 + FN_HALO
    n_seq = 1 if trunk.per_batch_mod else math.gcd(nb, max(1, 512 // tm))
    tile = pl.BlockSpec((n_seq, tm, D_MODEL), lambda b, i: (b, i, 0))
    mode = pl.Buffered(1) if seq // tm > 2 else None
    whole = pl.BlockSpec((n_seq, seq, D_MODEL), lambda b, i: (b, 0, 0), pipeline_mode=mode)
    return pl.pallas_call(
        functools.partial(_fn_b_kernel, seq=seq),
        grid=(nb // n_seq, seq // tm),
        in_specs=[_resident((half_rows, 2 * seq)), _resident((tm, tm + FN_HALO)), whole, whole,
                  tile, _mod_spec(trunk, layer), _ng_spec(layer),
                  _resident((D_MODEL, D_MODEL)), _resident((1, D_MODEL))],
        out_specs=tile,
        out_shape=jax.ShapeDtypeStruct(x.shape, f32),
        scratch_shapes=[pltpu.VMEM((n_seq, half_rows, D_MODEL), bf16)],
        compiler_params=_params("parallel", "arbitrary"),
    )(_mxu_table(_fourier_half_table(seq)), jnp.asarray(_row_reversal(tm)), a.reshape(x.shape), b.reshape(x.shape),
      x, mod, norm_g, w_out, b_out)


@functools.lru_cache(maxsize=None)
def _hyena_features(seq):
    t = np.linspace(0.0, 1.0, seq)[:, None]
    ang = 2.0 * np.pi * np.arange(seq)[:, None] / seq
    bands = np.linspace(1e-4, HY_BANDS - 1, HY_BANDS)[None]
    z = np.concatenate([t, np.cos(bands * ang), -np.sin(bands * ang)], axis=-1)
    out = np.zeros((seq, HY_EMB_PAD), np.float32)
    out[:, :HY_EMB] = np.concatenate([z[0::2], z[1::2]], axis=0)
    return out


@functools.lru_cache(maxsize=None)
def _hyena_decay_rates():
    d = np.linspace(math.log(HY_TARGET) / HY_FAST, math.log(HY_TARGET) / HY_SLOW, D_MODEL)
    return np.abs(d)[None].astype(np.float32)


@functools.lru_cache(maxsize=None)
def _hyena_tables(seq):
    half = seq // 2
    p = np.arange(half)[:, None]
    m = np.arange(half)[None, :]
    ang_e = np.pi * ((p * 2 * m) % (2 * seq)) / seq
    ang_o = np.pi * ((p * (2 * m + 1)) % (2 * seq)) / seq
    fwd = np.concatenate([np.cos(ang_e), np.cos(ang_o), np.sin(ang_e), np.sin(ang_o)], axis=1)
    inv = np.concatenate([np.cos(ang_e).T, -np.sin(ang_e).T, np.cos(ang_o).T, -np.sin(ang_o).T], axis=1)
    return fwd.astype(np.float32), inv.astype(np.float32)


@functools.lru_cache(maxsize=None)
def _parity_permutation(n):
    out = np.zeros((n, n), np.float32)
    half = n // 2
    out[np.arange(half), 2 * np.arange(half)] = 1.0
    out[half + np.arange(half), 2 * np.arange(half) + 1] = 1.0
    return out


def _hy_in_kernel(x_ref, xp_ref, xn_ref, mod_ref, ng_ref, w_ref, b_ref, cw_ref, cb_ref, sel_ref,
                  x0_ref, z_ref, zb_ref):
    j = pl.program_id(1)
    g, sh, sc = ng_ref[0:1, :], mod_ref[0:1, :], mod_ref[1:2, :]
    tl = x_ref.shape[0]
    rows = jnp.concatenate([x_ref[...], xp_ref[...], xn_ref[...]], axis=0)
    u_all = jnp.dot(_norm_mod(rows, g, sh, sc).astype(bf16), w_ref[...], preferred_element_type=f32) + b_ref[...]
    u = u_all[:tl, :]
    prev_row = jnp.where(j == 0, 0.0, u_all[tl + 7:tl + 8, :])
    next_row = jnp.where(j == pl.num_programs(1) - 1, 0.0, u_all[tl + 8:tl + 9, :])
    u_prev, u_next = _shifted(u, prev_row, next_row)
    uc = u_prev * cw_ref[0:1, :] + u * cw_ref[1:2, :] + u_next * cw_ref[2:3, :] + cb_ref[...]
    x0_ref[...] = uc[:, :D_MODEL]
    z = uc[:, 2 * D_MODEL:] * uc[:, D_MODEL:2 * D_MODEL]
    z_ref[...] = z
    split = jnp.dot(sel_ref[...], z.astype(bf16), preferred_element_type=f32).astype(bf16)
    zb_ref[0] = split[:tl // 2, :]
    zb_ref[1] = split[tl // 2:, :]


def _alternating_sign(first, rows):
    t = first + lax.broadcasted_iota(jnp.int32, (rows, 1), 0)
    return (1 - 2 * (t & 1)).astype(f32)


def _parity_dft(rows, even, odd, half):
    d = functools.partial(jnp.dot, preferred_element_type=f32)
    return (d(rows[:, :half], even), d(rows[:, half:2 * half], odd),
            d(rows[:, 2 * half:3 * half], even), d(rows[:, 3 * half:], odd))


def _hy_filter_kernel(feat_ref, w1_ref, b1_ref, fq_ref, w2_ref, b2_ref, w3f_ref, w3b_ref, rate_ref,
                      tab_ref, o_ref, hdn_ref, *, seq):
    half = seq // 2

    @pl.when(pl.program_id(0) == 0)
    def _():
        hdn = jnp.sin(fq_ref[0:1, :] * (_dot3(feat_ref[...], w1_ref[...]) + b1_ref[...]))
        hdn_ref[...] = jnp.sin(fq_ref[1:2, :] * (_dot3(hdn, w2_ref[...]) + b2_ref[...]))

    hdn = hdn_ref[...]
    rows = lax.broadcasted_iota(jnp.int32, (seq, 1), 0)
    lag = jnp.where(rows < half, 2 * rows, 2 * (rows - half) + 1)
    win = jnp.exp(-(lag.astype(f32) * (1.0 / (seq - 1))) * rate_ref[...])
    k_fwd = _dot3(hdn, w3f_ref[...]) * win
    k_bwd = _dot3(hdn, w3b_ref[...]) * win
    both = k_fwd + k_bwd
    diff = k_fwd - k_bwd
    ec, oc, _, _ = _parity_dft(tab_ref[...], both[:half].astype(bf16), both[half:].astype(bf16), half)
    _, _, es, os_ = _parity_dft(tab_ref[...], diff[:half].astype(bf16), diff[half:].astype(bf16), half)
    sign = _alternating_sign(0, half)
    mid_re = jnp.sum(both[:half] * sign, axis=0, keepdims=True)
    mid_im = -jnp.sum(diff[half:] * sign, axis=0, keepdims=True)
    first = lax.broadcasted_iota(jnp.int32, (half, 1), 0) == 0
    wgt = jnp.where(first, 0.5 / seq, 1.0 / seq)
    o_ref[0] = (ec + oc) * wgt
    o_ref[1] = jnp.where(first, mid_re * (1.0 / seq), -(es + os_) * wgt)
    o_ref[2] = (ec - oc) * wgt
    o_ref[3] = jnp.where(first, mid_im * (1.0 / seq), (es - os_) * wgt)


def _hy_fwd_kernel(tab_ref, z_ref, ks_ref, o_ref, mid_ref, *, seq):
    i = pl.program_id(1)
    half = seq // 2
    tf = o_ref.shape[1]
    z_even, z_odd = z_ref[0], z_ref[1]
    ec, oc, es, os_ = _parity_dft(_table_rows(tab_ref, i, tf), z_even, z_odd, half)
    zre_lo, zim_lo, zre_hi, zim_hi = ec + oc, -(es + os_), ec - oc, es - os_
    kre_lo, kim_lo, kre_hi, kim_hi = ks_ref[0], ks_ref[1], ks_ref[2], ks_ref[3]
    yre_lo = zre_lo * kre_lo - zim_lo * kim_lo
    yim_lo = zre_lo * kim_lo + zim_lo * kre_lo
    yre_hi = zre_hi * kre_hi - zim_hi * kim_hi
    yim_hi = zre_hi * kim_hi + zim_hi * kre_hi

    @pl.when(i == 0)
    def _():
        sign = _alternating_sign(0, half)
        mid_re = jnp.sum(z_even.astype(f32) * sign, axis=0, keepdims=True)
        mid_im = -jnp.sum(z_odd.astype(f32) * sign, axis=0, keepdims=True)
        k_re, k_im = ks_ref[1, 0:1, :], ks_ref[3, 0:1, :]
        mid_ref[0:1, :] = mid_re * k_re - mid_im * k_im
        mid_ref[1:2, :] = mid_re * k_im + mid_im * k_re

    @pl.when(i != 0)
    def _():
        mid_ref[...] = jnp.zeros_like(mid_ref)

    packed = (i * tf + lax.broadcasted_iota(jnp.int32, (tf, 1), 0)) == 0
    o_ref[0] = (yre_lo + yre_hi).astype(bf16)
    o_ref[1] = jnp.where(packed, mid_ref[0:1, :], yim_lo - yim_hi).astype(bf16)
    o_ref[2] = (yre_lo - yre_hi).astype(bf16)
    o_ref[3] = jnp.where(packed, mid_ref[1:2, :], yim_lo + yim_hi).astype(bf16)


def _hy_inv_kernel(tab_ref, il_ref, ys_ref, z_ref, x0_ref, x_ref, mod_ref, ng_ref, d_ref, wo_ref, bo_ref, o_ref, *,
                   seq):
    i = pl.program_id(1)
    tm = x_ref.shape[0] // 2
    half = seq // 2
    rows = _table_rows(tab_ref, i, tm)
    sign = _alternating_sign(i * tm, tm)
    conv_even = (jnp.dot(rows[:, :seq], ys_ref[:seq, :], preferred_element_type=f32)
                 + sign * ys_ref[half:half + 1, :].astype(f32))
    conv_odd = (jnp.dot(rows[:, seq:], ys_ref[seq:, :], preferred_element_type=f32)
                - sign * ys_ref[3 * half:3 * half + 1, :].astype(f32))
    conv = jnp.dot(il_ref[...], jnp.concatenate([conv_even, conv_odd], axis=0).astype(bf16),
                   preferred_element_type=f32)
    y = conv + z_ref[...] * d_ref[...]
    out = jnp.dot((y * x0_ref[...]).astype(bf16), wo_ref[...], preferred_element_type=f32) + bo_ref[...]
    o_ref[...] = _residual(x_ref[...], out, ng_ref[1:2, :], mod_ref[2:3, :])


def _hyena_layer(x, trunk, layer, mod, norm_g, p):
    nb, seq = trunk.batch, trunk.seq
    tl = _tile_rows(trunk, 256)
    x8 = x.reshape(nb, seq // 8, 8, D_MODEL)
    prev_spec, next_spec = _halo_specs(trunk, tl)
    x0, z, zb = pl.pallas_call(
        _hy_in_kernel,
        grid=(nb, seq // tl),
        in_specs=[_x_spec(tl), prev_spec, next_spec, _mod_spec(trunk, layer), _ng_spec(layer),
                  _resident((D_MODEL, 3 * D_MODEL)), _resident((1, 3 * D_MODEL)),
                  _resident((3, 3 * D_MODEL)), _resident((1, 3 * D_MODEL)), _resident((tl, tl))],
        out_specs=[_x_spec(tl), _x_spec(tl),
                   pl.BlockSpec((None, 2, tl // 2, D_MODEL), lambda b, j: (b, 0, j, 0))],
        out_shape=[jax.ShapeDtypeStruct(x.shape, f32)] * 2
        + [jax.ShapeDtypeStruct((nb, 2, seq // 2, D_MODEL), bf16)],
        compiler_params=_params("parallel", "parallel"),
    )(x, x8, x8, mod, norm_g, p["w_in"], p["b_in"], p["conv_w"], p["conv_b"],
      jnp.asarray(_parity_permutation(tl), dtype=bf16))

    half = seq // 2
    fwd_table, inv_table = (_mxu_table(t) for t in _hyena_tables(seq))
    table_spec = _resident((half, 2 * seq))
    tn = 256
    nblk = D_MODEL // tn
    ks = pl.pallas_call(
        functools.partial(_hy_filter_kernel, seq=seq),
        grid=(nblk,),
        in_specs=[_resident((seq, HY_EMB_PAD)), _resident((HY_EMB_PAD, HY_FILT)), _resident((1, HY_FILT)),
                  _resident((2, HY_FILT)), _resident((HY_FILT, HY_FILT)), _resident((1, HY_FILT)),
                  pl.BlockSpec((HY_FILT, tn), lambda j: (0, j)),
                  pl.BlockSpec((HY_FILT, tn), lambda j: (0, nblk + j)),
                  pl.BlockSpec((1, tn), lambda j: (0, j)),
                  table_spec],
        out_specs=pl.BlockSpec((4, half, tn), lambda j: (0, 0, j)),
        out_shape=jax.ShapeDtypeStruct((4, half, D_MODEL), f32),
        scratch_shapes=[pltpu.VMEM((seq, HY_FILT), f32)],
        compiler_params=_params("arbitrary"),
    )(jnp.asarray(_hyena_features(seq)), p["f_w1"], p["f_b1"], p["f_freq"], p["f_w2"], p["f_b2"],
      p["f_w3"], p["f_w3"], jnp.asarray(_hyena_decay_rates()), fwd_table)

    tf = min(half, 512)
    mode = pl.Buffered(1) if half // tf > 1 else None
    ys = pl.pallas_call(
        functools.partial(_hy_fwd_kernel, seq=seq),
        grid=(nb, half // tf),
        in_specs=[table_spec,
                  pl.BlockSpec((None, 2, half, D_MODEL), lambda b, i: (b, 0, 0, 0), pipeline_mode=mode),
                  pl.BlockSpec((4, tf, D_MODEL), lambda b, i: (0, i, 0))],
        out_specs=pl.BlockSpec((None, 4, tf, D_MODEL), lambda b, i: (b, 0, i, 0)),
        out_shape=jax.ShapeDtypeStruct((nb, 4, half, D_MODEL), bf16),
        scratch_shapes=[pltpu.VMEM((2, D_MODEL), f32)],
        compiler_params=_params("parallel", "arbitrary"),
    )(fwd_table, zb, ks)

    tm = min(half, 256)
    mode = pl.Buffered(1) if half // tm > 2 else None
    pair_tile = _x_spec(2 * tm)
    return pl.pallas_call(
        functools.partial(_hy_inv_kernel, seq=seq),
        grid=(nb, half // tm),
        in_specs=[table_spec, _resident((2 * tm, 2 * tm)),
                  pl.BlockSpec((None, 2 * seq, D_MODEL), lambda b, i: (b, 0, 0), pipeline_mode=mode),
                  pair_tile, pair_tile, pair_tile, _mod_spec(trunk, layer), _ng_spec(layer),
                  _resident((1, D_MODEL)), _resident((D_MODEL, D_MODEL)), _resident((1, D_MODEL))],
        out_specs=pair_tile,
        out_shape=jax.ShapeDtypeStruct(x.shape, f32),
        compiler_params=_params("parallel", "parallel"),
    )(inv_table, jnp.asarray(_parity_permutation(2 * tm).T, dtype=bf16), ys.reshape(nb, 2 * seq, D_MODEL),
      z, x0, x, mod, norm_g, p["d"], p["w_out"], p["b_out"])


def _head_sum(t, ones_pair):
    hi, lo = _split(t)
    cols = []
    for p in range(D_MODEL // PAIR_W):
        sl = slice(p * PAIR_W, (p + 1) * PAIR_W)
        cols.append(jnp.dot(hi[:, sl], ones_pair, preferred_element_type=f32)
                    + jnp.dot(lo[:, sl], ones_pair, preferred_element_type=f32))
    return jnp.concatenate(cols, axis=1)


def _rw_pre_kernel(x_ref, xp_ref, xn_ref, mod_ref, ng_ref, mix_ref, wr_ref, wk_ref, wv_ref, g1_ref, g2_ref,
                   w0_ref, w1_ref, w2_ref, a0_ref, a1_ref, a2_ref, kkp_ref, kap_ref, rkp_ref, ones_ref,
                   r_ref, k_ref, v_ref, kk_ref, g_ref, lwf_ref, lwb_ref, af_ref, ab_ref, bn_ref):
    j = pl.program_id(1)
    g, sh, sc = ng_ref[0:1, :], mod_ref[0:1, :], mod_ref[1:2, :]
    h = _norm_mod(x_ref[...], g, sh, sc)
    prev_row = jnp.where(j == 0, 0.0, _norm_mod(xp_ref[...], g, sh, sc)[7:8, :])
    next_row = jnp.where(j == pl.num_programs(1) - 1, 0.0, _norm_mod(xn_ref[...], g, sh, sc)[0:1, :])
    h_prev, h_next = _shifted(h, prev_row, next_row)
    xx = 0.5 * (h_prev + h_next) - h

    def mixed(m):
        return (h + xx * mix_ref[m:m + 1, :]).astype(bf16)

    r = jnp.dot(mixed(0), wr_ref[...], preferred_element_type=f32)
    xw = mixed(1)
    k = jnp.dot(mixed(2), wk_ref[...], preferred_element_type=f32)
    v = jnp.dot(mixed(3), wv_ref[...], preferred_element_type=f32)
    xa = mixed(4)
    r_ref[...] = r
    k_ref[...] = k
    v_ref[...] = v
    gate = _sigmoid(jnp.dot(mixed(5), g1_ref[...], preferred_element_type=f32))
    g_ref[...] = jnp.dot(gate.astype(bf16), g2_ref[...], preferred_element_type=f32)
    ones = ones_ref[...]
    kk = k * kkp_ref[...]
    kk_ref[...] = kk * lax.rsqrt(jnp.maximum(_head_sum(kk * kk, ones), 1e-24))
    a_sum = None
    for dd, (lw_ref, a_ref) in enumerate(((lwf_ref, af_ref), (lwb_ref, ab_ref))):
        lora = jnp.tanh(jnp.dot(xw, w1_ref[dd], preferred_element_type=f32))
        wl = w0_ref[dd:dd + 1, :] + jnp.dot(lora.astype(bf16), w2_ref[dd], preferred_element_type=f32)
        lw_ref[...] = -math.exp(-0.5) * _sigmoid(wl)
        al = jnp.dot(xa, a1_ref[dd], preferred_element_type=f32)
        a = _sigmoid(a0_ref[dd:dd + 1, :] + jnp.dot(al.astype(bf16), a2_ref[dd], preferred_element_type=f32))
        a_ref[...] = a
        a_sum = a if a_sum is None else a_sum + a
    kd_sum = k * (2.0 + (a_sum - 2.0) * kap_ref[...])
    bn_ref[...] = _head_sum(r * kd_sum * rkp_ref[...], ones) * v


def _stack_heads(x, first_head):
    return jnp.concatenate([jnp.where(first_head, x, 0.0), jnp.where(first_head, 0.0, x)], axis=0).astype(bf16)


def _dot_nt(a, b):
    return lax.dot_general(a, b, (((1,), (1,)), ((), ())), preferred_element_type=f32)


def _dot_tn(a, b):
    return lax.dot_general(a, b, (((0,), (0,)), ((), ())), preferred_element_type=f32)


def _rw_scan_kernel(r_ref, k_ref, v_ref, kk_ref, lw_ref, a_ref, kap_ref, s0_ref, o_ref, s_ref, *, reverse):
    c = pl.program_id(1)
    C, W, G = SCAN_CHUNK, PAIR_W, SCAN_GROUP

    @pl.when(c == 0)
    def _():
        s_ref[...] = s0_ref[...]

    row = lax.broadcasted_iota(jnp.int32, (C, W), 0)
    lane = lax.broadcasted_iota(jnp.int32, (C, W), 1)
    col = lane & (C - 1)
    first_head = lane < RW_HEAD
    before = (col > row) if reverse else (col < row)
    before_eq = before | (col == row)
    eye = (col == row).astype(f32)
    row_w = lax.broadcasted_iota(jnp.int32, (W, W), 0)
    lane_w = lax.broadcasted_iota(jnp.int32, (W, W), 1)
    same_head = (row_w >= RW_HEAD) == (lane_w >= RW_HEAD)
    diag_w = row_w == lane_w
    merge_masks = []
    s = 1
    while s < C:
        sh = s.bit_length() - 1
        same_pair = (row >> (sh + 1)) == (col >> (sh + 1))
        later, earlier = ((row >> sh) & 1, (col >> sh) & 1)
        merge_masks.append(same_pair & (((later == 0) & (earlier == 1)) if reverse
                                        else ((later == 1) & (earlier == 0))))
        s *= 2

    row_c = lax.broadcasted_iota(jnp.int32, (G * C, G * C), 0)
    col_c = lax.broadcasted_iota(jnp.int32, (G * C, G * C), 1)
    chunk_bits = C.bit_length() - 1
    same_chunk = (row_c >> chunk_bits) == (col_c >> chunk_bits)
    tri = (same_chunk & ((col_c >= row_c) if reverse else (col_c <= row_c))).astype(bf16)
    lw_all = lw_ref[...]
    cum_all = _dot_exact_rhs_left(tri, lw_all)
    n_pairs = D_MODEL // W

    def pairs(x):
        return [x[g * C:(g + 1) * C, p * W:(p + 1) * W] for g in range(G) for p in range(n_pairs)]

    def per_chunk_rows(rows):
        return [rows[g][:, p * W:(p + 1) * W] for g in range(G) for p in range(n_pairs)]

    def each(fn, *lists):
        return [fn(*args) for args in zip(*lists)]

    def stack(x):
        return _stack_heads(x, first_head)

    def mm(x, y):
        return jnp.dot(x.astype(bf16), stack(y), preferred_element_type=f32)

    def mm2(x, y0, y1):
        return jnp.dot(x.astype(bf16), jnp.concatenate([stack(y0), stack(y1)], axis=1),
                       preferred_element_type=f32)

    r, k, v, kk, a = (pairs(ref[...]) for ref in (r_ref, k_ref, v_ref, kk_ref, a_ref))
    lw, cum = pairs(lw_all), pairs(cum_all)
    tot = per_chunk_rows([jnp.sum(lw_all[g * C:(g + 1) * C, :], axis=0, keepdims=True) for g in range(G)])
    ka = per_chunk_rows([kap_ref[...]] * G)
    kd = each(lambda k_, a_, ka_: k_ * (1.0 + (a_ - 1.0) * ka_), k, a, ka)
    b = each(lambda kk_, a_: kk_ * a_, kk, a)
    e_in = each(jnp.exp, cum)
    e_out = each(lambda c_: jnp.exp(-c_), cum)
    e_end = each(lambda t_, c_: jnp.exp(t_ - c_), tot, cum)
    kap = each(lambda kk_, c_, l_: kk_ * jnp.exp(c_ - l_), kk, cum, lw)
    mul = lambda x_, y_: x_ * y_
    kt, bt, rt, kh, bh = each(mul, kd, e_out), each(mul, b, e_out), each(mul, r, e_in), each(mul, kd, e_end), \
        each(mul, b, e_end)

    lhs = each(lambda x_, y_: jnp.concatenate([x_, y_], axis=0).astype(bf16), kap, rt)
    gram = each(lambda l_, y0_, y1_: _dot_nt(l_, jnp.concatenate([stack(y0_), stack(y1_)], axis=0)), lhs, kt, bt)
    m_kk = each(lambda g_: jnp.where(before, g_[:C, :W], 0.0), gram)
    n_kb = each(lambda g_: jnp.where(before, g_[:C, W:], 0.0), gram)
    a_qk = each(lambda g_: jnp.where(before_eq, g_[C:, :W], 0.0), gram)
    a_qb = each(lambda g_: jnp.where(before_eq, g_[C:, W:], 0.0), gram)

    tinv = each(lambda n_: eye - jnp.where(merge_masks[0], n_, 0.0), n_kb)
    for off in merge_masks[1:]:
        tn = each(lambda t_, n_: mm(t_, jnp.where(off, n_, 0.0)), tinv, n_kb)
        tinv = each(lambda t_, tn_: t_ - mm(tn_, t_), tinv, tn)

    mv = each(mm, m_kk, v)
    tk = each(mm2, tinv, kap, mv)
    kt_c, w1 = each(lambda t_: t_[:, :W], tk), each(lambda t_: t_[:, W:], tk)
    qa = each(mm2, a_qb, kt_c, w1)
    av = each(mm, a_qk, v)
    q_eff = each(lambda rt_, qa_: rt_ - qa_[:, :W], rt, qa)
    o_intra = each(lambda av_, qa_: av_ - qa_[:, W:], av, qa)
    kb = each(lambda x_, b_: _dot_tn(x_.astype(bf16), b_.astype(bf16)), kt_c, bh)
    phi = each(lambda t_, kb_: jnp.where(same_head, jnp.where(diag_w, jnp.exp(t_), 0.0) - kb_, 0.0), tot, kb)
    ds_full = each(lambda v_, w_, kh_, bh_: _dot_tn(jnp.concatenate([v_, -w_], axis=0).astype(bf16),
                                                    jnp.concatenate([kh_, bh_], axis=0).astype(bf16)),
                   v, w1, kh, bh)
    d_s = each(lambda d_: jnp.where(first_head, d_[:RW_HEAD, :], d_[RW_HEAD:, :]), ds_full)
    state = [s_ref[:, p * W:(p + 1) * W] for p in range(n_pairs)]
    for g in (range(G - 1, -1, -1) if reverse else range(G)):
        sel = slice(g * n_pairs, (g + 1) * n_pairs)
        o = each(lambda q_, s_, oi_: _dot_nt(q_.astype(bf16), stack(s_)) + oi_, q_eff[sel], state, o_intra[sel])
        state = each(lambda s_, ph_, ds_: jnp.dot(s_.astype(bf16), ph_.astype(bf16),
                                                  preferred_element_type=f32) + ds_, state, phi[sel], d_s[sel])
        o_ref[g * C:(g + 1) * C, :] = jnp.concatenate(o, axis=1)
    s_ref[...] = jnp.concatenate(state, axis=1)


def _rw_post_kernel(of_ref, ob_ref, bn_ref, g_ref, x_ref, mod_ref, ng_ref, lw_ref, lb_ref, ones_ref,
                    wo_ref, o_ref):
    o = of_ref[...] + ob_ref[...]
    ones = ones_ref[...]
    inv_n = 1.0 / RW_HEAD
    dev = o - _head_sum(o, ones) * inv_n
    var = _head_sum(dev * dev, ones) * inv_n
    on = dev * lax.rsqrt(var + RW_LN_EPS) * lw_ref[...] + lb_ref[...] + bn_ref[...]
    y = jnp.dot((on * g_ref[...]).astype(bf16), wo_ref[...], preferred_element_type=f32)
    o_ref[...] = _residual(x_ref[...], y, ng_ref[1:2, :], mod_ref[2:3, :])


@functools.lru_cache(maxsize=None)
def _pair_ones():
    idx = np.arange(PAIR_W) // RW_HEAD
    return np.asarray(idx[:, None] == idx[None, :], dtype=bf16)


def _rwkv_layer(x, trunk, layer, mod, norm_g, p, s0_fwd, s0_bwd):
    nb, seq = trunk.batch, trunk.seq
    tl = _tile_rows(trunk, 256)
    x8 = x.reshape(nb, seq // 8, 8, D_MODEL)
    prev_spec, next_spec = _halo_specs(trunk, tl)
    lora_w, lora_a, lora_g = p["w1"].shape[-1], p["a1"].shape[-1], p["g1"].shape[-1]
    ones = jnp.asarray(_pair_ones())
    outs = pl.pallas_call(
        _rw_pre_kernel,
        grid=(nb, seq // tl),
        in_specs=[_x_spec(tl), prev_spec, next_spec, _mod_spec(trunk, layer), _ng_spec(layer),
                  _resident((6, D_MODEL)),
                  _resident((D_MODEL, D_MODEL)), _resident((D_MODEL, D_MODEL)), _resident((D_MODEL, D_MODEL)),
                  _resident((D_MODEL, lora_g)), _resident((lora_g, D_MODEL)),
                  _resident((2, D_MODEL)), _resident((2, D_MODEL, lora_w)), _resident((2, lora_w, D_MODEL)),
                  _resident((2, D_MODEL)), _resident((2, D_MODEL, lora_a)), _resident((2, lora_a, D_MODEL)),
                  _resident((1, D_MODEL)), _resident((1, D_MODEL)), _resident((1, D_MODEL)),
                  _resident((PAIR_W, PAIR_W))],
        out_specs=[_x_spec(tl)] * 10,
        out_shape=[jax.ShapeDtypeStruct(x.shape, f32)] * 10,
        compiler_params=_params("parallel", "parallel"),
    )(x, x8, x8, mod, norm_g, p["mix"], p["wr"], p["wk"], p["wv"], p["g1"], p["g2"],
      p["w0"], p["w1"], p["w2"], p["a0"], p["a1"], p["a2"], p["kk"], p["ka"], p["rk"], ones)
    r, k, v, kk, gate, lw_f, lw_b, a_f, a_b, bonus = outs

    def state_in(s):
        return s.transpose(0, 2, 1, 3).reshape(nb, RW_HEAD, D_MODEL)

    def state_out(s):
        return s.reshape(nb, RW_HEAD, RW_HEADS, RW_HEAD).transpose(0, 2, 1, 3)

    rows = SCAN_CHUNK * SCAN_GROUP
    n_chunks = seq // rows
    state_spec = pl.BlockSpec((None, RW_HEAD, D_MODEL), lambda b, c: (b, 0, 0))
    results = []
    for reverse, lw, a, s0 in ((False, lw_f, a_f, s0_fwd), (True, lw_b, a_b, s0_bwd)):
        if reverse:
            chunk_spec = pl.BlockSpec((None, rows, D_MODEL), lambda b, c: (b, n_chunks - 1 - c, 0))
        else:
            chunk_spec = pl.BlockSpec((None, rows, D_MODEL), lambda b, c: (b, c, 0))
        results.append(pl.pallas_call(
            functools.partial(_rw_scan_kernel, reverse=reverse),
            grid=(nb, n_chunks),
            in_specs=[chunk_spec] * 6 + [_resident((1, D_MODEL)), state_spec],
            out_specs=[chunk_spec, state_spec],
            out_shape=[jax.ShapeDtypeStruct(x.shape, f32),
                       jax.ShapeDtypeStruct((nb, RW_HEAD, D_MODEL), f32)],
            compiler_params=_params("parallel", "arbitrary"),
        )(r, k, v, kk, lw, a, p["ka"], state_in(s0)))
    (o_f, s_f), (o_b, s_b) = results

    y = pl.pallas_call(
        _rw_post_kernel,
        grid=(nb, seq // tl),
        in_specs=[_x_spec(tl)] * 5 + [_mod_spec(trunk, layer), _ng_spec(layer),
                                      _resident((1, D_MODEL)), _resident((1, D_MODEL)),
                                      _resident((PAIR_W, PAIR_W)), _resident((D_MODEL, D_MODEL))],
        out_specs=_x_spec(tl),
        out_shape=jax.ShapeDtypeStruct(x.shape, f32),
        compiler_params=_params("parallel", "parallel"),
    )(o_f, o_b, bonus, gate, x, mod, norm_g, p["lnx_w"], p["lnx_b"], ones, p["wo"])
    return y, state_out(s_f), state_out(s_b)


def kernel(x_prompt, x_sample, state_wkv_fwd, state_wkv_bwd, c, c_ctx, ada_w, ada_b, norm_g, ffn_w1, ffn_w3, ffn_w2, fn_w_out, fn_b_out, hy_w_in, hy_b_in, hy_conv_w, hy_conv_b, hy_f_w1, hy_f_b1, hy_f_freq, hy_f_w2, hy_f_b2, hy_f_w3, hy_d, hy_w_out, hy_b_out, rw_mix, rw_wr, rw_wk, rw_wv, rw_wo, rw_w0, rw_w1, rw_w2, rw_a0, rw_a1, rw_a2, rw_g1, rw_g2, rw_kk, rw_ka, rw_rk, rw_lnx_w, rw_lnx_b):
    n_ctx, n_dec = x_prompt.shape[0], x_sample.shape[0]
    assert 1 + n_dec <= MOD_ROWS
    trunks = (Trunk(n_ctx, x_prompt.shape[1], 0, False), Trunk(n_dec, x_sample.shape[1], 1, True))
    cond = jnp.zeros((MOD_ROWS, D_MODEL), f32).at[0].set(c_ctx).at[1:1 + n_dec].set(c)
    mod = _adaln(cond, ada_w, ada_b)

    n_rwkv = rw_w0.shape[0]
    zero_state = jnp.zeros((n_ctx, n_rwkv, RW_HEADS, RW_HEAD, RW_HEAD), f32)
    states = ((zero_state, zero_state), (state_wkv_fwd, state_wkv_bwd))
    xs = [x_prompt, x_sample]
    new_fwd, new_bwd = [], []
    for i in range(DEPTH):
        kind, j = i % N_MIXERS, i // N_MIXERS
        if kind == 0:
            w_out, b_out = fn_w_out[j].astype(bf16), fn_b_out[j][None]
            xs = [_fourier_layer(x, t, i, mod, norm_g, w_out, b_out) for x, t in zip(xs, trunks)]
        elif kind == 1:
            w1_pad = jnp.zeros((HY_EMB_PAD, HY_FILT), f32).at[:HY_EMB].set(hy_f_w1[j])
            p = dict(w_in=hy_w_in[j].astype(bf16), b_in=hy_b_in[j][None], conv_w=hy_conv_w[j],
                     conv_b=hy_conv_b[j][None], f_w1=w1_pad, f_b1=hy_f_b1[j][None], f_freq=hy_f_freq[j],
                     f_w2=hy_f_w2[j], f_b2=hy_f_b2[j][None], f_w3=hy_f_w3[j], d=hy_d[j][None],
                     w_out=hy_w_out[j].astype(bf16), b_out=hy_b_out[j][None])
            xs = [_hyena_layer(x, t, i, mod, norm_g, p) for x, t in zip(xs, trunks)]
        else:
            per_head = lambda t: t.reshape(1, D_MODEL)
            p = dict(mix=rw_mix[j], wr=rw_wr[j].astype(bf16), wk=rw_wk[j].astype(bf16),
                     wv=rw_wv[j].astype(bf16), wo=rw_wo[j].astype(bf16), w0=rw_w0[j],
                     w1=rw_w1[j].astype(bf16), w2=rw_w2[j].astype(bf16), a0=rw_a0[j],
                     a1=rw_a1[j].astype(bf16), a2=rw_a2[j].astype(bf16), g1=rw_g1[j].astype(bf16),
                     g2=rw_g2[j].astype(bf16), kk=per_head(rw_kk[j]), ka=per_head(rw_ka[j]),
                     rk=per_head(rw_rk[j]), lnx_w=rw_lnx_w[j][None], lnx_b=rw_lnx_b[j][None])
            outs = [_rwkv_layer(x, t, i, mod, norm_g, p, sf[:, j], sb[:, j])
                    for x, t, (sf, sb) in zip(xs, trunks, states)]
            xs = [o[0] for o in outs]
            new_fwd.append(outs[0][1])
            new_bwd.append(outs[0][2])
        w1, w3, w2 = ffn_w1[i].astype(bf16), ffn_w3[i].astype(bf16), ffn_w2[i].astype(bf16)
        xs = [_ffn(x, t, i, mod, norm_g, w1, w3, w2) for x, t in zip(xs, trunks)]
    return xs[0], xs[1], jnp.stack(new_fwd, axis=1), jnp.stack(new_bwd, axis=1)
```

```python
import functools
import math
from typing import NamedTuple

import numpy as np
import jax
import jax.numpy as jnp
from jax import lax
from jax.experimental import pallas as pl
from jax.experimental.pallas import tpu as pltpu

f32 = jnp.float32
bf16 = jnp.bfloat16

D_MODEL = 1024
DEPTH = 4
N_MIXERS = 3
D_FF = 2816
NORM_EPS = 1e-6
FN_GROUP_W = 256
FN_HALO = 16
HY_EMB = 33
HY_EMB_PAD = 128
HY_BANDS = 16
HY_FILT = 64
HY_TARGET = 1e-2
HY_FAST = 0.3
HY_SLOW = 1.5
RW_HEAD = 64
RW_HEADS = 16
RW_LN_EPS = 64e-5
SCAN_CHUNK = 64
SCAN_GROUP = 2
PAIR_W = 2 * RW_HEAD
assert SCAN_CHUNK == RW_HEAD, "the scan keeps (chunk, chunk) and (chunk, head) tiles in one lane layout"
MOD_ROWS = 8
VMEM_LIMIT = 56 * 1024 * 1024


class Trunk(NamedTuple):
    batch: int
    seq: int
    mod_base: int
    per_batch_mod: bool


def _params(*sem):
    return pltpu.CompilerParams(dimension_semantics=sem, vmem_limit_bytes=VMEM_LIMIT)


def _resident(shape):
    nd = len(shape)
    return pl.BlockSpec(shape, lambda *_: (0,) * nd, pipeline_mode=pl.Buffered(1))


def _layer_resident(shape, layer):
    nd = len(shape)
    return pl.BlockSpec((None,) + tuple(shape), lambda *_: (layer,) + (0,) * nd, pipeline_mode=pl.Buffered(1))


def _split(x):
    hi = x.astype(bf16)
    lo = (x - hi.astype(f32)).astype(bf16)
    return hi, lo


def _dot3(a, b):
    ah, al = _split(a)
    bh, bl = _split(b)
    d = functools.partial(jnp.dot, preferred_element_type=f32)
    return d(ah, bh) + (d(ah, bl) + d(al, bh))


def _dot_exact_rhs_left(a_exact, b):
    b0 = b.astype(bf16)
    r1 = b - b0.astype(f32)
    b1 = r1.astype(bf16)
    b2 = (r1 - b1.astype(f32)).astype(bf16)
    d = functools.partial(jnp.dot, preferred_element_type=f32)
    return d(a_exact, b0) + (d(a_exact, b1) + d(a_exact, b2))


def _sigmoid(x):
    return 0.5 * jnp.tanh(0.5 * x) + 0.5


def _norm_mod(x, g, shift, scale):
    ms = jnp.mean(x * x, axis=-1, keepdims=True)
    return (x * lax.rsqrt(ms + NORM_EPS)) * (g * (1.0 + scale)) + shift


def _residual(x, y, g, gate):
    ms = jnp.mean(y * y, axis=-1, keepdims=True)
    return x + gate * (y * lax.rsqrt(ms + NORM_EPS) * g)


def _shifted(h, prev_row, next_row):
    rows = h.shape[0]
    ridx = lax.broadcasted_iota(jnp.int32, h.shape, 0)
    h_prev = jnp.where(ridx == 0, prev_row, pltpu.roll(h, 1, 0))
    h_next = jnp.where(ridx == rows - 1, next_row, pltpu.roll(h, rows - 1, 0))
    return h_prev, h_next


def _tile_rows(trunk, cap):
    return min(trunk.seq, cap)


def _x_spec(tl):
    return pl.BlockSpec((None, tl, D_MODEL), lambda b, j: (b, j, 0))


def _halo_specs(trunk, tl):
    g = tl // 8
    last = trunk.seq // 8 - 1
    prev = pl.BlockSpec((None, None, 8, D_MODEL), lambda b, j: (b, jnp.maximum(j * g - 1, 0), 0, 0))
    nxt = pl.BlockSpec((None, None, 8, D_MODEL), lambda b, j: (b, jnp.minimum((j + 1) * g, last), 0, 0))
    return prev, nxt


def _mod_spec(trunk, layer):
    if trunk.per_batch_mod:
        return pl.BlockSpec((None, None, 6, D_MODEL), lambda b, *_: (layer, trunk.mod_base + b, 0, 0))
    return pl.BlockSpec((None, None, 6, D_MODEL), lambda b, *_: (layer, trunk.mod_base, 0, 0))


def _ng_spec(layer):
    return pl.BlockSpec((None, 4, D_MODEL), lambda *_: (layer, 0, 0))


def _mod_kernel(c_ref, w_ref, b_ref, o_ref):
    c = c_ref[...]
    o_ref[...] = _dot3(c * _sigmoid(c), w_ref[...]) + b_ref[...]


def _adaln(cond, ada_w, ada_b):
    tn = 1536
    out = pl.pallas_call(
        _mod_kernel,
        grid=(DEPTH, 6 * D_MODEL // tn),
        in_specs=[pl.BlockSpec((MOD_ROWS, D_MODEL), lambda l, j: (0, 0)),
                  pl.BlockSpec((None, D_MODEL, tn), lambda l, j: (l, 0, j)),
                  pl.BlockSpec((None, 1, tn), lambda l, j: (l, 0, j))],
        out_specs=pl.BlockSpec((None, MOD_ROWS, tn), lambda l, j: (l, 0, j)),
        out_shape=jax.ShapeDtypeStruct((DEPTH, MOD_ROWS, 6 * D_MODEL), f32),
        compiler_params=_params("arbitrary", "arbitrary"),
    )(cond, ada_w, ada_b.reshape(DEPTH, 1, 6 * D_MODEL))
    return out.reshape(DEPTH, MOD_ROWS, 6, D_MODEL)


def _ffn_kernel(x_ref, mod_ref, ng_ref, w1_ref, w3_ref, w2_ref, o_ref):
    x = x_ref[...]
    h = _norm_mod(x, ng_ref[2:3, :], mod_ref[3:4, :], mod_ref[4:5, :]).astype(bf16)
    a = jnp.dot(h, w1_ref[...], preferred_element_type=f32)
    b = jnp.dot(h, w3_ref[...], preferred_element_type=f32)
    gated = (a * _sigmoid(a) * b).astype(bf16)
    y = jnp.dot(gated, w2_ref[...], preferred_element_type=f32)
    o_ref[...] = _residual(x, y, ng_ref[3:4, :], mod_ref[5:6, :])


def _merge_sequences(x, trunk):
    if trunk.per_batch_mod:
        return x, trunk
    return x.reshape(1, -1, D_MODEL), Trunk(1, trunk.batch * trunk.seq, trunk.mod_base, False)


def _ffn(x, trunk, layer, mod, norm_g, w1, w3, w2):
    shape = x.shape
    x, trunk = _merge_sequences(x, trunk)
    tl = _tile_rows(trunk, 512)
    return pl.pallas_call(
        _ffn_kernel,
        grid=(trunk.batch, trunk.seq // tl),
        in_specs=[_x_spec(tl), _mod_spec(trunk, layer), _ng_spec(layer),
                  _layer_resident((D_MODEL, D_FF), layer), _layer_resident((D_MODEL, D_FF), layer),
                  _layer_resident((D_FF, D_MODEL), layer)],
        out_specs=_x_spec(tl),
        out_shape=jax.ShapeDtypeStruct(x.shape, f32),
        compiler_params=_params("parallel", "parallel"),
    )(x, mod, norm_g, w1, w3, w2).reshape(shape)


@functools.lru_cache(maxsize=None)
def _channel_dft():
    w = FN_GROUP_W
    idx = np.arange(w)
    ang = 2.0 * np.pi * ((idx[:, None] * idx[None, :]) % w) / w
    return (np.concatenate([np.cos(ang), np.sin(ang)], axis=1) / math.sqrt(w)).astype(np.float32)


def _table_rows(cs_ref, tile, rows):
    return cs_ref[pl.ds(pl.multiple_of(tile * rows, rows), rows), :]


def _mxu_table(table):
    return jnp.asarray(table).astype(bf16)


def _fn_a_kernel(x_ref, mod_ref, ng_ref, w_ref, a_ref, b_ref):
    h = _norm_mod(x_ref[...], ng_ref[0:1, :], mod_ref[0:1, :], mod_ref[1:2, :]).astype(bf16)
    w = FN_GROUP_W
    for g in range(D_MODEL // w):
        ab = jnp.dot(h[:, g * w:(g + 1) * w], w_ref[...], preferred_element_type=f32).astype(bf16)
        a_ref[:, g * w:(g + 1) * w] = ab[:, :w]
        b_ref[:, g * w:(g + 1) * w] = ab[:, w:]


@functools.lru_cache(maxsize=None)
def _fourier_half_table(seq):
    k = np.arange(seq // 2 + FN_HALO)
    t = np.arange(seq)
    ang = 2.0 * np.pi * ((k[:, None] * t[None, :]) % seq) / seq
    return (np.concatenate([np.cos(ang), np.sin(ang)], axis=1) / math.sqrt(seq)).astype(np.float32)


@functools.lru_cache(maxsize=None)
def _row_reversal(tm):
    out = np.zeros((tm, tm + FN_HALO), np.float32)
    out[np.arange(tm), tm - np.arange(tm)] = 1.0
    return np.asarray(out, dtype=bf16)


def _fn_b_kernel(cs_ref, rev_ref, a_ref, b_ref, x_ref, mod_ref, ng_ref, wo_ref, bo_ref, o_ref, mirror_ref, *,
                 seq):
    s = pl.program_id(1)
    half_tiles = pl.num_programs(1) // 2
    n_seq, tm = x_ref.shape[0], x_ref.shape[1]
    src_rows = tm + FN_HALO

    def finish(f_bf16):
        f = jnp.concatenate(f_bf16, axis=0)
        y = jnp.dot(f, wo_ref[...], preferred_element_type=f32) + bo_ref[...]
        x = x_ref[...].reshape(n_seq * tm, D_MODEL)
        o_ref[...] = _residual(x, y, ng_ref[1:2, :], mod_ref[2:3, :]).reshape(n_seq, tm, D_MODEL)

    @pl.when(s < half_tiles)
    def _():
        first = pl.multiple_of(s * tm, tm)
        rows = cs_ref[pl.ds(first, src_rows), :]
        out = []
        for i in range(n_seq):
            p = jnp.dot(rows[:, :seq], a_ref[i], preferred_element_type=f32)
            q = jnp.dot(rows[:, seq:], b_ref[i], preferred_element_type=f32)
            mirror_ref[i, pl.ds(first, src_rows), :] = (p + q).astype(bf16)
            out.append((p - q)[:tm, :].astype(bf16))
        finish(out)

    @pl.when(s >= half_tiles)
    def _():
        first = pl.multiple_of((2 * half_tiles - 1 - s) * tm, tm)
        finish([jnp.dot(rev_ref[...], mirror_ref[i, pl.ds(first, src_rows), :],
                        preferred_element_type=f32).astype(bf16) for i in range(n_seq)])


def _fourier_layer(x, trunk, layer, mod, norm_g, w_out, b_out):
    nb, seq = trunk.batch, trunk.seq
    xm, merged = _merge_sequences(x, trunk)
    tl = _tile_rows(merged, 512)
    a, b = pl.pallas_call(
        _fn_a_kernel,
        grid=(merged.batch, merged.seq // tl),
        in_specs=[_x_spec(tl), _mod_spec(merged, layer), _ng_spec(layer),
                  _resident((FN_GROUP_W, 2 * FN_GROUP_W))],
        out_specs=[_x_spec(tl)] * 2,
        out_shape=[jax.ShapeDtypeStruct(xm.shape, bf16)] * 2,
        compiler_params=_params("parallel", "parallel"),
    )(xm, mod, norm_g, _mxu_table(_channel_dft()))

    tm = min(seq // 2, 512)
    half_rows = seq // 2 + FN_HALO
    n_seq = 1 if trunk.per_batch_mod else math.gcd(nb, max(1, 512 // tm))
    tile = pl.BlockSpec((n_seq, tm, D_MODEL), lambda b, i: (b, i, 0))
    whole = pl.BlockSpec((n_seq, seq, D_MODEL), lambda b, i: (b, 0, 0))
    return pl.pallas_call(
        functools.partial(_fn_b_kernel, seq=seq),
        grid=(nb // n_seq, seq // tm),
        in_specs=[_resident((half_rows, 2 * seq)), _resident((tm, tm + FN_HALO)), whole, whole,
                  tile, _mod_spec(trunk, layer), _ng_spec(layer),
                  _resident((D_MODEL, D_MODEL)), _resident((1, D_MODEL))],
        out_specs=tile,
        out_shape=jax.ShapeDtypeStruct(x.shape, f32),
        scratch_shapes=[pltpu.VMEM((n_seq, half_rows, D_MODEL), bf16)],
        compiler_params=_params("parallel", "arbitrary"),
    )(_mxu_table(_fourier_half_table(seq)), jnp.asarray(_row_reversal(tm)), a.reshape(x.shape), b.reshape(x.shape),
      x, mod, norm_g, w_out, b_out)


@functools.lru_cache(maxsize=None)
def _hyena_features(seq):
    t = np.linspace(0.0, 1.0, seq)[:, None]
    ang = 2.0 * np.pi * np.arange(seq)[:, None] / seq
    bands = np.linspace(1e-4, HY_BANDS - 1, HY_BANDS)[None]
    z = np.concatenate([t, np.cos(bands * ang), -np.sin(bands * ang)], axis=-1)
    out = np.zeros((seq, HY_EMB_PAD), np.float32)
    out[:, :HY_EMB] = np.concatenate([z[0::2], z[1::2]], axis=0)
    return out


@functools.lru_cache(maxsize=None)
def _hyena_decay_rates():
    d = np.linspace(math.log(HY_TARGET) / HY_FAST, math.log(HY_TARGET) / HY_SLOW, D_MODEL)
    return np.abs(d)[None].astype(np.float32)


@functools.lru_cache(maxsize=None)
def _hyena_tables(seq):
    half = seq // 2
    p = np.arange(half)[:, None]
    m = np.arange(half)[None, :]
    ang_e = np.pi * ((p * 2 * m) % (2 * seq)) / seq
    ang_o = np.pi * ((p * (2 * m + 1)) % (2 * seq)) / seq
    fwd = np.concatenate([np.cos(ang_e), np.cos(ang_o), np.sin(ang_e), np.sin(ang_o)], axis=1)
    inv = np.concatenate([np.cos(ang_e).T, -np.sin(ang_e).T, np.cos(ang_o).T, -np.sin(ang_o).T], axis=1)
    return fwd.astype(np.float32), inv.astype(np.float32)


@functools.lru_cache(maxsize=None)
def _parity_permutation(n):
    out = np.zeros((n, n), np.float32)
    half = n // 2
    out[np.arange(half), 2 * np.arange(half)] = 1.0
    out[half + np.arange(half), 2 * np.arange(half) + 1] = 1.0
    return out


def _hy_in_kernel(x_ref, xp_ref, xn_ref, mod_ref, ng_ref, w_ref, b_ref, cw_ref, cb_ref, sel_ref,
                  x0_ref, z_ref, zb_ref):
    j = pl.program_id(1)
    g, sh, sc = ng_ref[0:1, :], mod_ref[0:1, :], mod_ref[1:2, :]
    tl = x_ref.shape[0]
    rows = jnp.concatenate([x_ref[...], xp_ref[...], xn_ref[...]], axis=0)
    u_all = jnp.dot(_norm_mod(rows, g, sh, sc).astype(bf16), w_ref[...], preferred_element_type=f32) + b_ref[...]
    u = u_all[:tl, :]
    prev_row = jnp.where(j == 0, 0.0, u_all[tl + 7:tl + 8, :])
    next_row = jnp.where(j == pl.num_programs(1) - 1, 0.0, u_all[tl + 8:tl + 9, :])
    u_prev, u_next = _shifted(u, prev_row, next_row)
    uc = u_prev * cw_ref[0:1, :] + u * cw_ref[1:2, :] + u_next * cw_ref[2:3, :] + cb_ref[...]
    x0_ref[...] = uc[:, :D_MODEL]
    z = uc[:, 2 * D_MODEL:] * uc[:, D_MODEL:2 * D_MODEL]
    z_ref[...] = z
    split = jnp.dot(sel_ref[...], z.astype(bf16), preferred_element_type=f32).astype(bf16)
    zb_ref[0] = split[:tl // 2, :]
    zb_ref[1] = split[tl // 2:, :]


def _alternating_sign(first, rows):
    t = first + lax.broadcasted_iota(jnp.int32, (rows, 1), 0)
    return (1 - 2 * (t & 1)).astype(f32)


def _parity_dft(rows, even, odd, half):
    d = functools.partial(jnp.dot, preferred_element_type=f32)
    return (d(rows[:, :half], even), d(rows[:, half:2 * half], odd),
            d(rows[:, 2 * half:3 * half], even), d(rows[:, 3 * half:], odd))


def _hy_filter_kernel(feat_ref, w1_ref, b1_ref, fq_ref, w2_ref, b2_ref, w3f_ref, w3b_ref, rate_ref,
                      tab_ref, o_ref, hdn_ref, *, seq):
    half = seq // 2

    @pl.when(pl.program_id(0) == 0)
    def _():
        hdn = jnp.sin(fq_ref[0:1, :] * (_dot3(feat_ref[...], w1_ref[...]) + b1_ref[...]))
        hdn_ref[...] = jnp.sin(fq_ref[1:2, :] * (_dot3(hdn, w2_ref[...]) + b2_ref[...]))

    hdn = hdn_ref[...]
    rows = lax.broadcasted_iota(jnp.int32, (seq, 1), 0)
    lag = jnp.where(rows < half, 2 * rows, 2 * (rows - half) + 1)
    win = jnp.exp(-(lag.astype(f32) * (1.0 / (seq - 1))) * rate_ref[...])
    k_fwd = _dot3(hdn, w3f_ref[...]) * win
    k_bwd = _dot3(hdn, w3b_ref[...]) * win
    both = k_fwd + k_bwd
    diff = k_fwd - k_bwd
    ec, oc, _, _ = _parity_dft(tab_ref[...], both[:half].astype(bf16), both[half:].astype(bf16), half)
    _, _, es, os_ = _parity_dft(tab_ref[...], diff[:half].astype(bf16), diff[half:].astype(bf16), half)
    sign = _alternating_sign(0, half)
    mid_re = jnp.sum(both[:half] * sign, axis=0, keepdims=True)
    mid_im = -jnp.sum(diff[half:] * sign, axis=0, keepdims=True)
    first = lax.broadcasted_iota(jnp.int32, (half, 1), 0) == 0
    wgt = jnp.where(first, 0.5 / seq, 1.0 / seq)
    o_ref[0] = (ec + oc) * wgt
    o_ref[1] = jnp.where(first, mid_re * (1.0 / seq), -(es + os_) * wgt)
    o_ref[2] = (ec - oc) * wgt
    o_ref[3] = jnp.where(first, mid_im * (1.0 / seq), (es - os_) * wgt)


def _hy_fwd_kernel(tab_ref, z_ref, ks_ref, o_ref, mid_ref, *, seq):
    i = pl.program_id(1)
    half = seq // 2
    tf = o_ref.shape[1]
    z_even, z_odd = z_ref[0], z_ref[1]
    ec, oc, es, os_ = _parity_dft(_table_rows(tab_ref, i, tf), z_even, z_odd, half)
    zre_lo, zim_lo, zre_hi, zim_hi = ec + oc, -(es + os_), ec - oc, es - os_
    kre_lo, kim_lo, kre_hi, kim_hi = ks_ref[0], ks_ref[1], ks_ref[2], ks_ref[3]
    yre_lo = zre_lo * kre_lo - zim_lo * kim_lo
    yim_lo = zre_lo * kim_lo + zim_lo * kre_lo
    yre_hi = zre_hi * kre_hi - zim_hi * kim_hi
    yim_hi = zre_hi * kim_hi + zim_hi * kre_hi

    @pl.when(i == 0)
    def _():
        sign = _alternating_sign(0, half)
        mid_re = jnp.sum(z_even.astype(f32) * sign, axis=0, keepdims=True)
        mid_im = -jnp.sum(z_odd.astype(f32) * sign, axis=0, keepdims=True)
        k_re, k_im = ks_ref[1, 0:1, :], ks_ref[3, 0:1, :]
        mid_ref[0:1, :] = mid_re * k_re - mid_im * k_im
        mid_ref[1:2, :] = mid_re * k_im + mid_im * k_re

    @pl.when(i != 0)
    def _():
        mid_ref[...] = jnp.zeros_like(mid_ref)

    packed = (i * tf + lax.broadcasted_iota(jnp.int32, (tf, 1), 0)) == 0
    o_ref[0] = (yre_lo + yre_hi).astype(bf16)
    o_ref[1] = jnp.where(packed, mid_ref[0:1, :], yim_lo - yim_hi).astype(bf16)
    o_ref[2] = (yre_lo - yre_hi).astype(bf16)
    o_ref[3] = jnp.where(packed, mid_ref[1:2, :], yim_lo + yim_hi).astype(bf16)


def _hy_inv_kernel(tab_ref, il_ref, ys_ref, z_ref, x0_ref, x_ref, mod_ref, ng_ref, d_ref, wo_ref, bo_ref, o_ref, *,
                   seq):
    i = pl.program_id(1)
    tm = x_ref.shape[0] // 2
    half = seq // 2
    rows = _table_rows(tab_ref, i, tm)
    sign = _alternating_sign(i * tm, tm)
    conv_even = (jnp.dot(rows[:, :seq], ys_ref[:seq, :], preferred_element_type=f32)
                 + sign * ys_ref[half:half + 1, :].astype(f32))
    conv_odd = (jnp.dot(rows[:, seq:], ys_ref[seq:, :], preferred_element_type=f32)
                - sign * ys_ref[3 * half:3 * half + 1, :].astype(f32))
    conv = jnp.dot(il_ref[...], jnp.concatenate([conv_even, conv_odd], axis=0).astype(bf16),
                   preferred_element_type=f32)
    y = conv + z_ref[...] * d_ref[...]
    out = jnp.dot((y * x0_ref[...]).astype(bf16), wo_ref[...], preferred_element_type=f32) + bo_ref[...]
    o_ref[...] = _residual(x_ref[...], out, ng_ref[1:2, :], mod_ref[2:3, :])


def _hyena_layer(x, trunk, layer, mod, norm_g, p):
    nb, seq = trunk.batch, trunk.seq
    tl = _tile_rows(trunk, 256)
    x8 = x.reshape(nb, seq // 8, 8, D_MODEL)
    prev_spec, next_spec = _halo_specs(trunk, tl)
    x0, z, zb = pl.pallas_call(
        _hy_in_kernel,
        grid=(nb, seq // tl),
        in_specs=[_x_spec(tl), prev_spec, next_spec, _mod_spec(trunk, layer), _ng_spec(layer),
                  _resident((D_MODEL, 3 * D_MODEL)), _resident((1, 3 * D_MODEL)),
                  _resident((3, 3 * D_MODEL)), _resident((1, 3 * D_MODEL)), _resident((tl, tl))],
        out_specs=[_x_spec(tl), _x_spec(tl),
                   pl.BlockSpec((None, 2, tl // 2, D_MODEL), lambda b, j: (b, 0, j, 0))],
        out_shape=[jax.ShapeDtypeStruct(x.shape, f32)] * 2
        + [jax.ShapeDtypeStruct((nb, 2, seq // 2, D_MODEL), bf16)],
        compiler_params=_params("parallel", "parallel"),
    )(x, x8, x8, mod, norm_g, p["w_in"], p["b_in"], p["conv_w"], p["conv_b"],
      jnp.asarray(_parity_permutation(tl), dtype=bf16))

    half = seq // 2
    fwd_table, inv_table = (_mxu_table(t) for t in _hyena_tables(seq))
    table_spec = _resident((half, 2 * seq))
    tn = 256
    nblk = D_MODEL // tn
    ks = pl.pallas_call(
        functools.partial(_hy_filter_kernel, seq=seq),
        grid=(nblk,),
        in_specs=[_resident((seq, HY_EMB_PAD)), _resident((HY_EMB_PAD, HY_FILT)), _resident((1, HY_FILT)),
                  _resident((2, HY_FILT)), _resident((HY_FILT, HY_FILT)), _resident((1, HY_FILT)),
                  pl.BlockSpec((HY_FILT, tn), lambda j: (0, j)),
                  pl.BlockSpec((HY_FILT, tn), lambda j: (0, nblk + j)),
                  pl.BlockSpec((1, tn), lambda j: (0, j)),
                  table_spec],
        out_specs=pl.BlockSpec((4, half, tn), lambda j: (0, 0, j)),
        out_shape=jax.ShapeDtypeStruct((4, half, D_MODEL), f32),
        scratch_shapes=[pltpu.VMEM((seq, HY_FILT), f32)],
        compiler_params=_params("arbitrary"),
    )(jnp.asarray(_hyena_features(seq)), p["f_w1"], p["f_b1"], p["f_freq"], p["f_w2"], p["f_b2"],
      p["f_w3"], p["f_w3"], jnp.asarray(_hyena_decay_rates()), fwd_table)

    tf = min(half, 512)
    mode = pl.Buffered(1) if half // tf > 1 else None
    ys = pl.pallas_call(
        functools.partial(_hy_fwd_kernel, seq=seq),
        grid=(nb, half // tf),
        in_specs=[table_spec,
                  pl.BlockSpec((None, 2, half, D_MODEL), lambda b, i: (b, 0, 0, 0), pipeline_mode=mode),
                  pl.BlockSpec((4, tf, D_MODEL), lambda b, i: (0, i, 0))],
        out_specs=pl.BlockSpec((None, 4, tf, D_MODEL), lambda b, i: (b, 0, i, 0)),
        out_shape=jax.ShapeDtypeStruct((nb, 4, half, D_MODEL), bf16),
        scratch_shapes=[pltpu.VMEM((2, D_MODEL), f32)],
        compiler_params=_params("parallel", "arbitrary"),
    )(fwd_table, zb, ks)

    tm = min(half, 256)
    pair_tile = _x_spec(2 * tm)
    return pl.pallas_call(
        functools.partial(_hy_inv_kernel, seq=seq),
        grid=(nb, half // tm),
        in_specs=[table_spec, _resident((2 * tm, 2 * tm)),
                  pl.BlockSpec((None, 2 * seq, D_MODEL), lambda b, i: (b, 0, 0)),
                  pair_tile, pair_tile, pair_tile, _mod_spec(trunk, layer), _ng_spec(layer),
                  _resident((1, D_MODEL)), _resident((D_MODEL, D_MODEL)), _resident((1, D_MODEL))],
        out_specs=pair_tile,
        out_shape=jax.ShapeDtypeStruct(x.shape, f32),
        compiler_params=_params("parallel", "parallel"),
    )(inv_table, jnp.asarray(_parity_permutation(2 * tm).T, dtype=bf16), ys.reshape(nb, 2 * seq, D_MODEL),
      z, x0, x, mod, norm_g, p["d"], p["w_out"], p["b_out"])


def _head_sum(t, ones_pair):
    hi, lo = _split(t)
    cols = []
    for p in range(D_MODEL // PAIR_W):
        sl = slice(p * PAIR_W, (p + 1) * PAIR_W)
        cols.append(jnp.dot(hi[:, sl], ones_pair, preferred_element_type=f32)
                    + jnp.dot(lo[:, sl], ones_pair, preferred_element_type=f32))
    return jnp.concatenate(cols, axis=1)


def _rw_pre_kernel(x_ref, xp_ref, xn_ref, mod_ref, ng_ref, mix_ref, wr_ref, wk_ref, wv_ref, g1_ref, g2_ref,
                   w0_ref, w1_ref, w2_ref, a0_ref, a1_ref, a2_ref, kkp_ref, kap_ref, rkp_ref, ones_ref,
                   r_ref, k_ref, v_ref, kk_ref, g_ref, lwf_ref, lwb_ref, af_ref, ab_ref, bn_ref):
    j = pl.program_id(1)
    g, sh, sc = ng_ref[0:1, :], mod_ref[0:1, :], mod_ref[1:2, :]
    h = _norm_mod(x_ref[...], g, sh, sc)
    prev_row = jnp.where(j == 0, 0.0, _norm_mod(xp_ref[...], g, sh, sc)[7:8, :])
    next_row = jnp.where(j == pl.num_programs(1) - 1, 0.0, _norm_mod(xn_ref[...], g, sh, sc)[0:1, :])
    h_prev, h_next = _shifted(h, prev_row, next_row)
    xx = 0.5 * (h_prev + h_next) - h

    def mixed(m):
        return (h + xx * mix_ref[m:m + 1, :]).astype(bf16)

    r = jnp.dot(mixed(0), wr_ref[...], preferred_element_type=f32)
    xw = mixed(1)
    k = jnp.dot(mixed(2), wk_ref[...], preferred_element_type=f32)
    v = jnp.dot(mixed(3), wv_ref[...], preferred_element_type=f32)
    xa = mixed(4)
    r_ref[...] = r
    k_ref[...] = k
    v_ref[...] = v
    gate = _sigmoid(jnp.dot(mixed(5), g1_ref[...], preferred_element_type=f32))
    g_ref[...] = jnp.dot(gate.astype(bf16), g2_ref[...], preferred_element_type=f32)
    ones = ones_ref[...]
    kk = k * kkp_ref[...]
    kk_ref[...] = kk * lax.rsqrt(jnp.maximum(_head_sum(kk * kk, ones), 1e-24))
    a_sum = None
    for dd, (lw_ref, a_ref) in enumerate(((lwf_ref, af_ref), (lwb_ref, ab_ref))):
        lora = jnp.tanh(jnp.dot(xw, w1_ref[dd], preferred_element_type=f32))
        wl = w0_ref[dd:dd + 1, :] + jnp.dot(lora.astype(bf16), w2_ref[dd], preferred_element_type=f32)
        lw_ref[...] = -math.exp(-0.5) * _sigmoid(wl)
        al = jnp.dot(xa, a1_ref[dd], preferred_element_type=f32)
        a = _sigmoid(a0_ref[dd:dd + 1, :] + jnp.dot(al.astype(bf16), a2_ref[dd], preferred_element_type=f32))
        a_ref[...] = a
        a_sum = a if a_sum is None else a_sum + a
    kd_sum = k * (2.0 + (a_sum - 2.0) * kap_ref[...])
    bn_ref[...] = _head_sum(r * kd_sum * rkp_ref[...], ones) * v


def _stack_heads(x, first_head):
    return jnp.concatenate([jnp.where(first_head, x, 0.0), jnp.where(first_head, 0.0, x)], axis=0).astype(bf16)


def _dot_nt(a, b):
    return lax.dot_general(a, b, (((1,), (1,)), ((), ())), preferred_element_type=f32)


def _dot_tn(a, b):
    return lax.dot_general(a, b, (((0,), (0,)), ((), ())), preferred_element_type=f32)


def _rw_scan_kernel(r_ref, k_ref, v_ref, kk_ref, lw_ref, a_ref, kap_ref, s0_ref, o_ref, s_ref, *, reverse):
    c = pl.program_id(1)
    C, W, G = SCAN_CHUNK, PAIR_W, SCAN_GROUP

    @pl.when(c == 0)
    def _():
        s_ref[...] = s0_ref[...]

    row = lax.broadcasted_iota(jnp.int32, (C, W), 0)
    lane = lax.broadcasted_iota(jnp.int32, (C, W), 1)
    col = lane & (C - 1)
    first_head = lane < RW_HEAD
    before = (col > row) if reverse else (col < row)
    before_eq = before | (col == row)
    eye = (col == row).astype(f32)
    row_w = lax.broadcasted_iota(jnp.int32, (W, W), 0)
    lane_w = lax.broadcasted_iota(jnp.int32, (W, W), 1)
    same_head = (row_w >= RW_HEAD) == (lane_w >= RW_HEAD)
    diag_w = row_w == lane_w
    merge_masks = []
    s = 1
    while s < C:
        sh = s.bit_length() - 1
        same_pair = (row >> (sh + 1)) == (col >> (sh + 1))
        later, earlier = ((row >> sh) & 1, (col >> sh) & 1)
        merge_masks.append(same_pair & (((later == 0) & (earlier == 1)) if reverse
                                        else ((later == 1) & (earlier == 0))))
        s *= 2

    row_c = lax.broadcasted_iota(jnp.int32, (G * C, G * C), 0)
    col_c = lax.broadcasted_iota(jnp.int32, (G * C, G * C), 1)
    chunk_bits = C.bit_length() - 1
    same_chunk = (row_c >> chunk_bits) == (col_c >> chunk_bits)
    tri = (same_chunk & ((col_c >= row_c) if reverse else (col_c <= row_c))).astype(bf16)
    lw_all = lw_ref[...]
    cum_all = _dot_exact_rhs_left(tri, lw_all)
    n_pairs = D_MODEL // W

    def pairs(x):
        return [x[g * C:(g + 1) * C, p * W:(p + 1) * W] for g in range(G) for p in range(n_pairs)]

    def per_chunk_rows(rows):
        return [rows[g][:, p * W:(p + 1) * W] for g in range(G) for p in range(n_pairs)]

    def each(fn, *lists):
        return [fn(*args) for args in zip(*lists)]

    def stack(x):
        return _stack_heads(x, first_head)

    def mm(x, y):
        return jnp.dot(x.astype(bf16), stack(y), preferred_element_type=f32)

    def mm2(x, y0, y1):
        return jnp.dot(x.astype(bf16), jnp.concatenate([stack(y0), stack(y1)], axis=1),
                       preferred_element_type=f32)

    r, k, v, kk, a = (pairs(ref[...]) for ref in (r_ref, k_ref, v_ref, kk_ref, a_ref))
    lw, cum = pairs(lw_all), pairs(cum_all)
    tot = per_chunk_rows([jnp.sum(lw_all[g * C:(g + 1) * C, :], axis=0, keepdims=True) for g in range(G)])
    ka = per_chunk_rows([kap_ref[...]] * G)
    kd = each(lambda k_, a_, ka_: k_ * (1.0 + (a_ - 1.0) * ka_), k, a, ka)
    b = each(lambda kk_, a_: kk_ * a_, kk, a)
    e_in = each(jnp.exp, cum)
    e_out = each(lambda c_: jnp.exp(-c_), cum)
    e_end = each(lambda t_, c_: jnp.exp(t_ - c_), tot, cum)
    kap = each(lambda kk_, c_, l_: kk_ * jnp.exp(c_ - l_), kk, cum, lw)
    mul = lambda x_, y_: x_ * y_
    kt, bt, rt, kh, bh = each(mul, kd, e_out), each(mul, b, e_out), each(mul, r, e_in), each(mul, kd, e_end), \
        each(mul, b, e_end)

    lhs = each(lambda x_, y_: jnp.concatenate([x_, y_], axis=0).astype(bf16), kap, rt)
    gram = each(lambda l_, y0_, y1_: _dot_nt(l_, jnp.concatenate([stack(y0_), stack(y1_)], axis=0)), lhs, kt, bt)
    m_kk = each(lambda g_: jnp.where(before, g_[:C, :W], 0.0), gram)
    n_kb = each(lambda g_: jnp.where(before, g_[:C, W:], 0.0), gram)
    a_qk = each(lambda g_: jnp.where(before_eq, g_[C:, :W], 0.0), gram)
    a_qb = each(lambda g_: jnp.where(before_eq, g_[C:, W:], 0.0), gram)

    tinv = each(lambda n_: eye - jnp.where(merge_masks[0], n_, 0.0), n_kb)
    for off in merge_masks[1:]:
        tn = each(lambda t_, n_: mm(t_, jnp.where(off, n_, 0.0)), tinv, n_kb)
        tinv = each(lambda t_, tn_: t_ - mm(tn_, t_), tinv, tn)

    mv = each(mm, m_kk, v)
    tk = each(mm2, tinv, kap, mv)
    kt_c, w1 = each(lambda t_: t_[:, :W], tk), each(lambda t_: t_[:, W:], tk)
    qa = each(mm2, a_qb, kt_c, w1)
    av = each(mm, a_qk, v)
    q_eff = each(lambda rt_, qa_: rt_ - qa_[:, :W], rt, qa)
    o_intra = each(lambda av_, qa_: av_ - qa_[:, W:], av, qa)
    kb = each(lambda x_, b_: _dot_tn(x_.astype(bf16), b_.astype(bf16)), kt_c, bh)
    phi = each(lambda t_, kb_: jnp.where(same_head, jnp.where(diag_w, jnp.exp(t_), 0.0) - kb_, 0.0), tot, kb)
    ds_full = each(lambda v_, w_, kh_, bh_: _dot_tn(jnp.concatenate([v_, -w_], axis=0).astype(bf16),
                                                    jnp.concatenate([kh_, bh_], axis=0).astype(bf16)),
                   v, w1, kh, bh)
    d_s = each(lambda d_: jnp.where(first_head, d_[:RW_HEAD, :], d_[RW_HEAD:, :]), ds_full)
    state = [s_ref[:, p * W:(p + 1) * W] for p in range(n_pairs)]
    for g in (range(G - 1, -1, -1) if reverse else range(G)):
        sel = slice(g * n_pairs, (g + 1) * n_pairs)
        o = each(lambda q_, s_, oi_: _dot_nt(q_.astype(bf16), stack(s_)) + oi_, q_eff[sel], state, o_intra[sel])
        state = each(lambda s_, ph_, ds_: jnp.dot(s_.astype(bf16), ph_.astype(bf16),
                                                  preferred_element_type=f32) + ds_, state, phi[sel], d_s[sel])
        o_ref[g * C:(g + 1) * C, :] = jnp.concatenate(o, axis=1)
    s_ref[...] = jnp.concatenate(state, axis=1)


def _rw_post_kernel(of_ref, ob_ref, bn_ref, g_ref, x_ref, mod_ref, ng_ref, lw_ref, lb_ref, ones_ref,
                    wo_ref, o_ref):
    o = of_ref[...] + ob_ref[...]
    ones = ones_ref[...]
    inv_n = 1.0 / RW_HEAD
    dev = o - _head_sum(o, ones) * inv_n
    var = _head_sum(dev * dev, ones) * inv_n
    on = dev * lax.rsqrt(var + RW_LN_EPS) * lw_ref[...] + lb_ref[...] + bn_ref[...]
    y = jnp.dot((on * g_ref[...]).astype(bf16), wo_ref[...], preferred_element_type=f32)
    o_ref[...] = _residual(x_ref[...], y, ng_ref[1:2, :], mod_ref[2:3, :])


@functools.lru_cache(maxsize=None)
def _pair_ones():
    idx = np.arange(PAIR_W) // RW_HEAD
    return np.asarray(idx[:, None] == idx[None, :], dtype=bf16)


def _rwkv_layer(x, trunk, layer, mod, norm_g, p, s0_fwd, s0_bwd):
    nb, seq = trunk.batch, trunk.seq
    tl = _tile_rows(trunk, 256)
    x8 = x.reshape(nb, seq // 8, 8, D_MODEL)
    prev_spec, next_spec = _halo_specs(trunk, tl)
    lora_w, lora_a, lora_g = p["w1"].shape[-1], p["a1"].shape[-1], p["g1"].shape[-1]
    ones = jnp.asarray(_pair_ones())
    outs = pl.pallas_call(
        _rw_pre_kernel,
        grid=(nb, seq // tl),
        in_specs=[_x_spec(tl), prev_spec, next_spec, _mod_spec(trunk, layer), _ng_spec(layer),
                  _resident((6, D_MODEL)),
                  _resident((D_MODEL, D_MODEL)), _resident((D_MODEL, D_MODEL)), _resident((D_MODEL, D_MODEL)),
                  _resident((D_MODEL, lora_g)), _resident((lora_g, D_MODEL)),
                  _resident((2, D_MODEL)), _resident((2, D_MODEL, lora_w)), _resident((2, lora_w, D_MODEL)),
                  _resident((2, D_MODEL)), _resident((2, D_MODEL, lora_a)), _resident((2, lora_a, D_MODEL)),
                  _resident((1, D_MODEL)), _resident((1, D_MODEL)), _resident((1, D_MODEL)),
                  _resident((PAIR_W, PAIR_W))],
        out_specs=[_x_spec(tl)] * 10,
        out_shape=[jax.ShapeDtypeStruct(x.shape, f32)] * 10,
        compiler_params=_params("parallel", "parallel"),
    )(x, x8, x8, mod, norm_g, p["mix"], p["wr"], p["wk"], p["wv"], p["g1"], p["g2"],
      p["w0"], p["w1"], p["w2"], p["a0"], p["a1"], p["a2"], p["kk"], p["ka"], p["rk"], ones)
    r, k, v, kk, gate, lw_f, lw_b, a_f, a_b, bonus = outs

    def state_in(s):
        return s.transpose(0, 2, 1, 3).reshape(nb, RW_HEAD, D_MODEL)

    def state_out(s):
        return s.reshape(nb, RW_HEAD, RW_HEADS, RW_HEAD).transpose(0, 2, 1, 3)

    rows = SCAN_CHUNK * SCAN_GROUP
    n_chunks = seq // rows
    state_spec = pl.BlockSpec((None, RW_HEAD, D_MODEL), lambda b, c: (b, 0, 0))
    results = []
    for reverse, lw, a, s0 in ((False, lw_f, a_f, s0_fwd), (True, lw_b, a_b, s0_bwd)):
        if reverse:
            chunk_spec = pl.BlockSpec((None, rows, D_MODEL), lambda b, c: (b, n_chunks - 1 - c, 0))
        else:
            chunk_spec = pl.BlockSpec((None, rows, D_MODEL), lambda b, c: (b, c, 0))
        results.append(pl.pallas_call(
            functools.partial(_rw_scan_kernel, reverse=reverse),
            grid=(nb, n_chunks),
            in_specs=[chunk_spec] * 6 + [_resident((1, D_MODEL)), state_spec],
            out_specs=[chunk_spec, state_spec],
            out_shape=[jax.ShapeDtypeStruct(x.shape, f32),
                       jax.ShapeDtypeStruct((nb, RW_HEAD, D_MODEL), f32)],
            compiler_params=_params("parallel", "arbitrary"),
        )(r, k, v, kk, lw, a, p["ka"], state_in(s0)))
    (o_f, s_f), (o_b, s_b) = results

    y = pl.pallas_call(
        _rw_post_kernel,
        grid=(nb, seq // tl),
        in_specs=[_x_spec(tl)] * 5 + [_mod_spec(trunk, layer), _ng_spec(layer),
                                      _resident((1, D_MODEL)), _resident((1, D_MODEL)),
                                      _resident((PAIR_W, PAIR_W)), _resident((D_MODEL, D_MODEL))],
        out_specs=_x_spec(tl),
        out_shape=jax.ShapeDtypeStruct(x.shape, f32),
        compiler_params=_params("parallel", "parallel"),
    )(o_f, o_b, bonus, gate, x, mod, norm_g, p["lnx_w"], p["lnx_b"], ones, p["wo"])
    return y, state_out(s_f), state_out(s_b)


def kernel(x_prompt, x_sample, state_wkv_fwd, state_wkv_bwd, c, c_ctx, ada_w, ada_b, norm_g, ffn_w1, ffn_w3, ffn_w2, fn_w_out, fn_b_out, hy_w_in, hy_b_in, hy_conv_w, hy_conv_b, hy_f_w1, hy_f_b1, hy_f_freq, hy_f_w2, hy_f_b2, hy_f_w3, hy_d, hy_w_out, hy_b_out, rw_mix, rw_wr, rw_wk, rw_wv, rw_wo, rw_w0, rw_w1, rw_w2, rw_a0, rw_a1, rw_a2, rw_g1, rw_g2, rw_kk, rw_ka, rw_rk, rw_lnx_w, rw_lnx_b):
    n_ctx, n_dec = x_prompt.shape[0], x_sample.shape[0]
    assert 1 + n_dec <= MOD_ROWS
    trunks = (Trunk(n_ctx, x_prompt.shape[1], 0, False), Trunk(n_dec, x_sample.shape[1], 1, True))
    cond = jnp.zeros((MOD_ROWS, D_MODEL), f32).at[0].set(c_ctx).at[1:1 + n_dec].set(c)
    mod = _adaln(cond, ada_w, ada_b)

    n_rwkv = rw_w0.shape[0]
    zero_state = jnp.zeros((n_ctx, n_rwkv, RW_HEADS, RW_HEAD, RW_HEAD), f32)
    states = ((zero_state, zero_state), (state_wkv_fwd, state_wkv_bwd))
    xs = [x_prompt, x_sample]
    ffn_bf16 = [w.astype(bf16) for w in (ffn_w1, ffn_w3, ffn_w2)]
    new_fwd, new_bwd = [], []
    for i in range(DEPTH):
        kind, j = i % N_MIXERS, i // N_MIXERS
        if kind == 0:
            w_out, b_out = fn_w_out[j].astype(bf16), fn_b_out[j][None]
            xs = [_fourier_layer(x, t, i, mod, norm_g, w_out, b_out) for x, t in zip(xs, trunks)]
        elif kind == 1:
            w1_pad = jnp.zeros((HY_EMB_PAD, HY_FILT), f32).at[:HY_EMB].set(hy_f_w1[j])
            p = dict(w_in=hy_w_in[j].astype(bf16), b_in=hy_b_in[j][None], conv_w=hy_conv_w[j],
                     conv_b=hy_conv_b[j][None], f_w1=w1_pad, f_b1=hy_f_b1[j][None], f_freq=hy_f_freq[j],
                     f_w2=hy_f_w2[j], f_b2=hy_f_b2[j][None], f_w3=hy_f_w3[j], d=hy_d[j][None],
                     w_out=hy_w_out[j].astype(bf16), b_out=hy_b_out[j][None])
            xs = [_hyena_layer(x, t, i, mod, norm_g, p) for x, t in zip(xs, trunks)]
        else:
            per_head = lambda t: t.reshape(1, D_MODEL)
            p = dict(mix=rw_mix[j], wr=rw_wr[j].astype(bf16), wk=rw_wk[j].astype(bf16),
                     wv=rw_wv[j].astype(bf16), wo=rw_wo[j].astype(bf16), w0=rw_w0[j],
                     w1=rw_w1[j].astype(bf16), w2=rw_w2[j].astype(bf16), a0=rw_a0[j],
                     a1=rw_a1[j].astype(bf16), a2=rw_a2[j].astype(bf16), g1=rw_g1[j].astype(bf16),
                     g2=rw_g2[j].astype(bf16), kk=per_head(rw_kk[j]), ka=per_head(rw_ka[j]),
                     rk=per_head(rw_rk[j]), lnx_w=rw_lnx_w[j][None], lnx_b=rw_lnx_b[j][None])
            outs = [_rwkv_layer(x, t, i, mod, norm_g, p, sf[:, j], sb[:, j])
                    for x, t, (sf, sb) in zip(xs, trunks, states)]
            xs = [o[0] for o in outs]
            new_fwd.append(outs[0][1])
            new_bwd.append(outs[0][2])
        xs = [_ffn(x, t, i, mod, norm_g, ffn_bf16[0], ffn_bf16[1], ffn_bf16[2]) for x, t in zip(xs, trunks)]
    return xs[0], xs[1], jnp.stack(new_fwd, axis=1), jnp.stack(new_bwd, axis=1)
```

```python
import functools
import math
from typing import NamedTuple

import numpy as np
import jax
import jax.numpy as jnp
from jax import lax
from jax.experimental import pallas as pl
from jax.experimental.pallas import tpu as pltpu

f32 = jnp.float32
bf16 = jnp.bfloat16

D_MODEL = 1024
DEPTH = 4
N_MIXERS = 3
D_FF = 2816
NORM_EPS = 1e-6
FN_GROUP_W = 256
FN_HALO = 16
HY_EMB = 33
HY_EMB_PAD = 128
HY_BANDS = 16
HY_FILT = 64
HY_TARGET = 1e-2
HY_FAST = 0.3
HY_SLOW = 1.5
RW_HEAD = 64
RW_HEADS = 16
RW_LN_EPS = 64e-5
SCAN_CHUNK = 64
SCAN_GROUP = 2
PAIR_W = 2 * RW_HEAD
assert SCAN_CHUNK == RW_HEAD, "the scan keeps (chunk, chunk) and (chunk, head) tiles in one lane layout"
MOD_ROWS = 8
VMEM_LIMIT = 56 * 1024 * 1024


class Trunk(NamedTuple):
    batch: int
    seq: int
    mod_base: int
    per_batch_mod: bool


def _params(*sem):
    return pltpu.CompilerParams(dimension_semantics=sem, vmem_limit_bytes=VMEM_LIMIT)


def _resident(shape):
    nd = len(shape)
    return pl.BlockSpec(shape, lambda *_: (0,) * nd, pipeline_mode=pl.Buffered(1))


def _layer_resident(shape, layer):
    nd = len(shape)
    return pl.BlockSpec((None,) + tuple(shape), lambda *_: (layer,) + (0,) * nd, pipeline_mode=pl.Buffered(1))


def _split(x):
    hi = x.astype(bf16)
    lo = (x - hi.astype(f32)).astype(bf16)
    return hi, lo


def _dot3(a, b):
    ah, al = _split(a)
    bh, bl = _split(b)
    d = functools.partial(jnp.dot, preferred_element_type=f32)
    return d(ah, bh) + (d(ah, bl) + d(al, bh))


def _dot_exact_rhs_left(a_exact, b):
    b0, b1 = _split(b)
    d = functools.partial(jnp.dot, preferred_element_type=f32)
    return d(a_exact, b0) + d(a_exact, b1)


def _sigmoid(x):
    return 0.5 * jnp.tanh(0.5 * x) + 0.5


def _norm_mod(x, g, shift, scale):
    ms = jnp.mean(x * x, axis=-1, keepdims=True)
    return (x * lax.rsqrt(ms + NORM_EPS)) * (g * (1.0 + scale)) + shift


def _residual(x, y, g, gate):
    ms = jnp.mean(y * y, axis=-1, keepdims=True)
    return x + gate * (y * lax.rsqrt(ms + NORM_EPS) * g)


def _shifted(h, prev_row, next_row):
    rows = h.shape[0]
    ridx = lax.broadcasted_iota(jnp.int32, h.shape, 0)
    h_prev = jnp.where(ridx == 0, prev_row, pltpu.roll(h, 1, 0))
    h_next = jnp.where(ridx == rows - 1, next_row, pltpu.roll(h, rows - 1, 0))
    return h_prev, h_next


def _tile_rows(trunk, cap):
    return min(trunk.seq, cap)


def _x_spec(tl):
    return pl.BlockSpec((None, tl, D_MODEL), lambda b, j: (b, j, 0))


def _halo_specs(trunk, tl):
    g = tl // 8
    last = trunk.seq // 8 - 1
    prev = pl.BlockSpec((None, None, 8, D_MODEL), lambda b, j: (b, jnp.maximum(j * g - 1, 0), 0, 0))
    nxt = pl.BlockSpec((None, None, 8, D_MODEL), lambda b, j: (b, jnp.minimum((j + 1) * g, last), 0, 0))
    return prev, nxt


def _mod_spec(trunk, layer):
    if trunk.per_batch_mod:
        return pl.BlockSpec((None, None, 6, D_MODEL), lambda b, *_: (layer, trunk.mod_base + b, 0, 0))
    return pl.BlockSpec((None, None, 6, D_MODEL), lambda b, *_: (layer, trunk.mod_base, 0, 0))


def _ng_spec(layer):
    return pl.BlockSpec((None, 4, D_MODEL), lambda *_: (layer, 0, 0))


def _mod_kernel(c_ref, w_ref, b_ref, o_ref):
    c = c_ref[...]
    o_ref[...] = _dot3(c * _sigmoid(c), w_ref[...]) + b_ref[...]


def _adaln(cond, ada_w, ada_b):
    tn = 1536
    out = pl.pallas_call(
        _mod_kernel,
        grid=(DEPTH, 6 * D_MODEL // tn),
        in_specs=[pl.BlockSpec((MOD_ROWS, D_MODEL), lambda l, j: (0, 0)),
                  pl.BlockSpec((None, D_MODEL, tn), lambda l, j: (l, 0, j)),
                  pl.BlockSpec((None, 1, tn), lambda l, j: (l, 0, j))],
        out_specs=pl.BlockSpec((None, MOD_ROWS, tn), lambda l, j: (l, 0, j)),
        out_shape=jax.ShapeDtypeStruct((DEPTH, MOD_ROWS, 6 * D_MODEL), f32),
        compiler_params=_params("arbitrary", "arbitrary"),
    )(cond, ada_w, ada_b.reshape(DEPTH, 1, 6 * D_MODEL))
    return out.reshape(DEPTH, MOD_ROWS, 6, D_MODEL)


def _ffn_kernel(x_ref, mod_ref, ng_ref, w1_ref, w3_ref, w2_ref, o_ref):
    x = x_ref[...]
    h = _norm_mod(x, ng_ref[2:3, :], mod_ref[3:4, :], mod_ref[4:5, :]).astype(bf16)
    a = jnp.dot(h, w1_ref[...], preferred_element_type=f32)
    b = jnp.dot(h, w3_ref[...], preferred_element_type=f32)
    gated = (a * _sigmoid(a) * b).astype(bf16)
    y = jnp.dot(gated, w2_ref[...], preferred_element_type=f32)
    o_ref[...] = _residual(x, y, ng_ref[3:4, :], mod_ref[5:6, :])


def _merge_sequences(x, trunk):
    if trunk.per_batch_mod:
        return x, trunk
    return x.reshape(1, -1, D_MODEL), Trunk(1, trunk.batch * trunk.seq, trunk.mod_base, False)


def _ffn(x, trunk, layer, mod, norm_g, w1, w3, w2):
    shape = x.shape
    x, trunk = _merge_sequences(x, trunk)
    tl = _tile_rows(trunk, 512)
    return pl.pallas_call(
        _ffn_kernel,
        grid=(trunk.batch, trunk.seq // tl),
        in_specs=[_x_spec(tl), _mod_spec(trunk, layer), _ng_spec(layer),
                  _layer_resident((D_MODEL, D_FF), layer), _layer_resident((D_MODEL, D_FF), layer),
                  _layer_resident((D_FF, D_MODEL), layer)],
        out_specs=_x_spec(tl),
        out_shape=jax.ShapeDtypeStruct(x.shape, f32),
        compiler_params=_params("parallel", "parallel"),
    )(x, mod, norm_g, w1, w3, w2).reshape(shape)


@functools.lru_cache(maxsize=None)
def _channel_dft():
    w = FN_GROUP_W
    idx = np.arange(w)
    ang = 2.0 * np.pi * ((idx[:, None] * idx[None, :]) % w) / w
    return (np.concatenate([np.cos(ang), np.sin(ang)], axis=1) / math.sqrt(w)).astype(np.float32)


def _table_rows(cs_ref, tile, rows):
    return cs_ref[pl.ds(pl.multiple_of(tile * rows, rows), rows), :]


def _mxu_table(table):
    return jnp.asarray(table).astype(bf16)


def _fn_a_kernel(x_ref, mod_ref, ng_ref, w_ref, a_ref, b_ref):
    h = _norm_mod(x_ref[...], ng_ref[0:1, :], mod_ref[0:1, :], mod_ref[1:2, :]).astype(bf16)
    w = FN_GROUP_W
    for g in range(D_MODEL // w):
        ab = jnp.dot(h[:, g * w:(g + 1) * w], w_ref[...], preferred_element_type=f32).astype(bf16)
        a_ref[:, g * w:(g + 1) * w] = ab[:, :w]
        b_ref[:, g * w:(g + 1) * w] = ab[:, w:]


@functools.lru_cache(maxsize=None)
def _fourier_half_table(seq):
    k = np.arange(seq // 2 + FN_HALO)
    t = np.arange(seq)
    ang = 2.0 * np.pi * ((k[:, None] * t[None, :]) % seq) / seq
    return (np.concatenate([np.cos(ang), np.sin(ang)], axis=1) / math.sqrt(seq)).astype(np.float32)


@functools.lru_cache(maxsize=None)
def _row_reversal(tm):
    out = np.zeros((tm, tm + FN_HALO), np.float32)
    out[np.arange(tm), tm - np.arange(tm)] = 1.0
    return np.asarray(out, dtype=bf16)


def _fn_b_kernel(cs_ref, rev_ref, a_ref, b_ref, x_ref, mod_ref, ng_ref, wo_ref, bo_ref, o_ref, mirror_ref, *,
                 seq):
    s = pl.program_id(1)
    half_tiles = pl.num_programs(1) // 2
    n_seq, tm = x_ref.shape[0], x_ref.shape[1]
    src_rows = tm + FN_HALO

    def finish(f_bf16):
        f = jnp.concatenate(f_bf16, axis=0)
        y = jnp.dot(f, wo_ref[...], preferred_element_type=f32) + bo_ref[...]
        x = x_ref[...].reshape(n_seq * tm, D_MODEL)
        o_ref[...] = _residual(x, y, ng_ref[1:2, :], mod_ref[2:3, :]).reshape(n_seq, tm, D_MODEL)

    @pl.when(s < half_tiles)
    def _():
        first = pl.multiple_of(s * tm, tm)
        rows = cs_ref[pl.ds(first, src_rows), :]
        out = []
        for i in range(n_seq):
            p = jnp.dot(rows[:, :seq], a_ref[i], preferred_element_type=f32)
            q = jnp.dot(rows[:, seq:], b_ref[i], preferred_element_type=f32)
            mirror_ref[i, pl.ds(first, src_rows), :] = (p + q).astype(bf16)
            out.append((p - q)[:tm, :].astype(bf16))
        finish(out)

    @pl.when(s >= half_tiles)
    def _():
        first = pl.multiple_of((2 * half_tiles - 1 - s) * tm, tm)
        finish([jnp.dot(rev_ref[...], mirror_ref[i, pl.ds(first, src_rows), :],
                        preferred_element_type=f32).astype(bf16) for i in range(n_seq)])


def _fourier_layer(x, trunk, layer, mod, norm_g, w_out, b_out):
    nb, seq = trunk.batch, trunk.seq
    xm, merged = _merge_sequences(x, trunk)
    tl = _tile_rows(merged, 512)
    a, b = pl.pallas_call(
        _fn_a_kernel,
        grid=(merged.batch, merged.seq // tl),
        in_specs=[_x_spec(tl), _mod_spec(merged, layer), _ng_spec(layer),
                  _resident((FN_GROUP_W, 2 * FN_GROUP_W))],
        out_specs=[_x_spec(tl)] * 2,
        out_shape=[jax.ShapeDtypeStruct(xm.shape, bf16)] * 2,
        compiler_params=_params("parallel", "parallel"),
    )(xm, mod, norm_g, _mxu_table(_channel_dft()))

    tm = min(seq // 2, 512)
    half_rows = seq // 2 + FN_HALO
    n_seq = 1 if trunk.per_batch_mod else math.gcd(nb, max(1, 512 // tm))
    tile = pl.BlockSpec((n_seq, tm, D_MODEL), lambda b, i: (b, i, 0))
    whole = pl.BlockSpec((n_seq, seq, D_MODEL), lambda b, i: (b, 0, 0))
    return pl.pallas_call(
        functools.partial(_fn_b_kernel, seq=seq),
        grid=(nb // n_seq, seq // tm),
        in_specs=[_resident((half_rows, 2 * seq)), _resident((tm, tm + FN_HALO)), whole, whole,
                  tile, _mod_spec(trunk, layer), _ng_spec(layer),
                  _resident((D_MODEL, D_MODEL)), _resident((1, D_MODEL))],
        out_specs=tile,
        out_shape=jax.ShapeDtypeStruct(x.shape, f32),
        scratch_shapes=[pltpu.VMEM((n_seq, half_rows, D_MODEL), bf16)],
        compiler_params=_params("parallel", "arbitrary"),
    )(_mxu_table(_fourier_half_table(seq)), jnp.asarray(_row_reversal(tm)), a.reshape(x.shape), b.reshape(x.shape),
      x, mod, norm_g, w_out, b_out)


@functools.lru_cache(maxsize=None)
def _hyena_features(seq):
    t = np.linspace(0.0, 1.0, seq)[:, None]
    ang = 2.0 * np.pi * np.arange(seq)[:, None] / seq
    bands = np.linspace(1e-4, HY_BANDS - 1, HY_BANDS)[None]
    z = np.concatenate([t, np.cos(bands * ang), -np.sin(bands * ang)], axis=-1)
    out = np.zeros((seq, HY_EMB_PAD), np.float32)
    out[:, :HY_EMB] = np.concatenate([z[0::2], z[1::2]], axis=0)
    return out


@functools.lru_cache(maxsize=None)
def _hyena_decay_rates():
    d = np.linspace(math.log(HY_TARGET) / HY_FAST, math.log(HY_TARGET) / HY_SLOW, D_MODEL)
    return np.abs(d)[None].astype(np.float32)


@functools.lru_cache(maxsize=None)
def _hyena_tables(seq):
    half = seq // 2
    p = np.arange(half)[:, None]
    m = np.arange(half)[None, :]
    ang_e = np.pi * ((p * 2 * m) % (2 * seq)) / seq
    ang_o = np.pi * ((p * (2 * m + 1)) % (2 * seq)) / seq
    fwd = np.concatenate([np.cos(ang_e), np.cos(ang_o), np.sin(ang_e), np.sin(ang_o)], axis=1)
    inv = np.concatenate([np.cos(ang_e).T, -np.sin(ang_e).T, np.cos(ang_o).T, -np.sin(ang_o).T], axis=1)
    return fwd.astype(np.float32), inv.astype(np.float32)


@functools.lru_cache(maxsize=None)
def _parity_permutation(n):
    out = np.zeros((n, n), np.float32)
    half = n // 2
    out[np.arange(half), 2 * np.arange(half)] = 1.0
    out[half + np.arange(half), 2 * np.arange(half) + 1] = 1.0
    return out


def _hy_in_kernel(x_ref, xp_ref, xn_ref, mod_ref, ng_ref, w_ref, b_ref, cw_ref, cb_ref, sel_ref,
                  x0_ref, z_ref, zb_ref):
    j = pl.program_id(1)
    g, sh, sc = ng_ref[0:1, :], mod_ref[0:1, :], mod_ref[1:2, :]
    tl = x_ref.shape[0]
    rows = jnp.concatenate([x_ref[...], xp_ref[...], xn_ref[...]], axis=0)
    u_all = jnp.dot(_norm_mod(rows, g, sh, sc).astype(bf16), w_ref[...], preferred_element_type=f32) + b_ref[...]
    u = u_all[:tl, :]
    prev_row = jnp.where(j == 0, 0.0, u_all[tl + 7:tl + 8, :])
    next_row = jnp.where(j == pl.num_programs(1) - 1, 0.0, u_all[tl + 8:tl + 9, :])
    u_prev, u_next = _shifted(u, prev_row, next_row)
    uc = u_prev * cw_ref[0:1, :] + u * cw_ref[1:2, :] + u_next * cw_ref[2:3, :] + cb_ref[...]
    x0_ref[...] = uc[:, :D_MODEL]
    z = uc[:, 2 * D_MODEL:] * uc[:, D_MODEL:2 * D_MODEL]
    z_ref[...] = z
    split = jnp.dot(sel_ref[...], z.astype(bf16), preferred_element_type=f32).astype(bf16)
    zb_ref[0] = split[:tl // 2, :]
    zb_ref[1] = split[tl // 2:, :]


def _alternating_sign(first, rows):
    t = first + lax.broadcasted_iota(jnp.int32, (rows, 1), 0)
    return (1 - 2 * (t & 1)).astype(f32)


def _parity_dft(rows, even, odd, half):
    d = functools.partial(jnp.dot, preferred_element_type=f32)
    return (d(rows[:, :half], even), d(rows[:, half:2 * half], odd),
            d(rows[:, 2 * half:3 * half], even), d(rows[:, 3 * half:], odd))


def _hy_filter_kernel(feat_ref, w1_ref, b1_ref, fq_ref, w2_ref, b2_ref, w3f_ref, w3b_ref, rate_ref,
                      tab_ref, o_ref, hdn_ref, *, seq):
    half = seq // 2

    @pl.when(pl.program_id(0) == 0)
    def _():
        hdn = jnp.sin(fq_ref[0:1, :] * (_dot3(feat_ref[...], w1_ref[...]) + b1_ref[...]))
        hdn_ref[...] = jnp.sin(fq_ref[1:2, :] * (_dot3(hdn, w2_ref[...]) + b2_ref[...]))

    hdn = hdn_ref[...]
    rows = lax.broadcasted_iota(jnp.int32, (seq, 1), 0)
    lag = jnp.where(rows < half, 2 * rows, 2 * (rows - half) + 1)
    win = jnp.exp(-(lag.astype(f32) * (1.0 / (seq - 1))) * rate_ref[...])
    k_fwd = _dot3(hdn, w3f_ref[...]) * win
    k_bwd = _dot3(hdn, w3b_ref[...]) * win
    both = k_fwd + k_bwd
    diff = k_fwd - k_bwd
    ec, oc, _, _ = _parity_dft(tab_ref[...], both[:half].astype(bf16), both[half:].astype(bf16), half)
    _, _, es, os_ = _parity_dft(tab_ref[...], diff[:half].astype(bf16), diff[half:].astype(bf16), half)
    sign = _alternating_sign(0, half)
    mid_re = jnp.sum(both[:half] * sign, axis=0, keepdims=True)
    mid_im = -jnp.sum(diff[half:] * sign, axis=0, keepdims=True)
    first = lax.broadcasted_iota(jnp.int32, (half, 1), 0) == 0
    wgt = jnp.where(first, 0.5 / seq, 1.0 / seq)
    o_ref[0] = (ec + oc) * wgt
    o_ref[1] = jnp.where(first, mid_re * (1.0 / seq), -(es + os_) * wgt)
    o_ref[2] = (ec - oc) * wgt
    o_ref[3] = jnp.where(first, mid_im * (1.0 / seq), (es - os_) * wgt)


def _hy_fwd_kernel(tab_ref, z_ref, ks_ref, o_ref, mid_ref, *, seq):
    i = pl.program_id(1)
    half = seq // 2
    tf = o_ref.shape[1]
    z_even, z_odd = z_ref[0], z_ref[1]
    ec, oc, es, os_ = _parity_dft(_table_rows(tab_ref, i, tf), z_even, z_odd, half)
    zre_lo, zim_lo, zre_hi, zim_hi = ec + oc, -(es + os_), ec - oc, es - os_
    kre_lo, kim_lo, kre_hi, kim_hi = ks_ref[0], ks_ref[1], ks_ref[2], ks_ref[3]
    yre_lo = zre_lo * kre_lo - zim_lo * kim_lo
    yim_lo = zre_lo * kim_lo + zim_lo * kre_lo
    yre_hi = zre_hi * kre_hi - zim_hi * kim_hi
    yim_hi = zre_hi * kim_hi + zim_hi * kre_hi

    @pl.when(i == 0)
    def _():
        sign = _alternating_sign(0, half)
        mid_re = jnp.sum(z_even.astype(f32) * sign, axis=0, keepdims=True)
        mid_im = -jnp.sum(z_odd.astype(f32) * sign, axis=0, keepdims=True)
        k_re, k_im = ks_ref[1, 0:1, :], ks_ref[3, 0:1, :]
        mid_ref[0:1, :] = mid_re * k_re - mid_im * k_im
        mid_ref[1:2, :] = mid_re * k_im + mid_im * k_re

    @pl.when(i != 0)
    def _():
        mid_ref[...] = jnp.zeros_like(mid_ref)

    packed = (i * tf + lax.broadcasted_iota(jnp.int32, (tf, 1), 0)) == 0
    o_ref[0] = (yre_lo + yre_hi).astype(bf16)
    o_ref[1] = jnp.where(packed, mid_ref[0:1, :], yim_lo - yim_hi).astype(bf16)
    o_ref[2] = (yre_lo - yre_hi).astype(bf16)
    o_ref[3] = jnp.where(packed, mid_ref[1:2, :], yim_lo + yim_hi).astype(bf16)


def _hy_inv_kernel(tab_ref, il_ref, ys_ref, z_ref, x0_ref, x_ref, mod_ref, ng_ref, d_ref, wo_ref, bo_ref, o_ref, *,
                   seq):
    i = pl.program_id(1)
    tm = x_ref.shape[0] // 2
    half = seq // 2
    rows = _table_rows(tab_ref, i, tm)
    sign = _alternating_sign(i * tm, tm)
    conv_even = (jnp.dot(rows[:, :seq], ys_ref[:seq, :], preferred_element_type=f32)
                 + sign * ys_ref[half:half + 1, :].astype(f32))
    conv_odd = (jnp.dot(rows[:, seq:], ys_ref[seq:, :], preferred_element_type=f32)
                - sign * ys_ref[3 * half:3 * half + 1, :].astype(f32))
    conv = jnp.dot(il_ref[...], jnp.concatenate([conv_even, conv_odd], axis=0).astype(bf16),
                   preferred_element_type=f32)
    y = conv + z_ref[...] * d_ref[...]
    out = jnp.dot((y * x0_ref[...]).astype(bf16), wo_ref[...], preferred_element_type=f32) + bo_ref[...]
    o_ref[...] = _residual(x_ref[...], out, ng_ref[1:2, :], mod_ref[2:3, :])


def _hyena_layer(x, trunk, layer, mod, norm_g, p):
    nb, seq = trunk.batch, trunk.seq
    tl = _tile_rows(trunk, 256)
    x8 = x.reshape(nb, seq // 8, 8, D_MODEL)
    prev_spec, next_spec = _halo_specs(trunk, tl)
    x0, z, zb = pl.pallas_call(
        _hy_in_kernel,
        grid=(nb, seq // tl),
        in_specs=[_x_spec(tl), prev_spec, next_spec, _mod_spec(trunk, layer), _ng_spec(layer),
                  _resident((D_MODEL, 3 * D_MODEL)), _resident((1, 3 * D_MODEL)),
                  _resident((3, 3 * D_MODEL)), _resident((1, 3 * D_MODEL)), _resident((tl, tl))],
        out_specs=[_x_spec(tl), _x_spec(tl),
                   pl.BlockSpec((None, 2, tl // 2, D_MODEL), lambda b, j: (b, 0, j, 0))],
        out_shape=[jax.ShapeDtypeStruct(x.shape, f32)] * 2
        + [jax.ShapeDtypeStruct((nb, 2, seq // 2, D_MODEL), bf16)],
        compiler_params=_params("parallel", "parallel"),
    )(x, x8, x8, mod, norm_g, p["w_in"], p["b_in"], p["conv_w"], p["conv_b"],
      jnp.asarray(_parity_permutation(tl), dtype=bf16))

    half = seq // 2
    fwd_table, inv_table = (_mxu_table(t) for t in _hyena_tables(seq))
    table_spec = _resident((half, 2 * seq))
    tn = 256
    nblk = D_MODEL // tn
    ks = pl.pallas_call(
        functools.partial(_hy_filter_kernel, seq=seq),
        grid=(nblk,),
        in_specs=[_resident((seq, HY_EMB_PAD)), _resident((HY_EMB_PAD, HY_FILT)), _resident((1, HY_FILT)),
                  _resident((2, HY_FILT)), _resident((HY_FILT, HY_FILT)), _resident((1, HY_FILT)),
                  pl.BlockSpec((HY_FILT, tn), lambda j: (0, j)),
                  pl.BlockSpec((HY_FILT, tn), lambda j: (0, nblk + j)),
                  pl.BlockSpec((1, tn), lambda j: (0, j)),
                  table_spec],
        out_specs=pl.BlockSpec((4, half, tn), lambda j: (0, 0, j)),
        out_shape=jax.ShapeDtypeStruct((4, half, D_MODEL), f32),
        scratch_shapes=[pltpu.VMEM((seq, HY_FILT), f32)],
        compiler_params=_params("arbitrary"),
    )(jnp.asarray(_hyena_features(seq)), p["f_w1"], p["f_b1"], p["f_freq"], p["f_w2"], p["f_b2"],
      p["f_w3"], p["f_w3"], jnp.asarray(_hyena_decay_rates()), fwd_table)

    tf = min(half, 512)
    mode = pl.Buffered(1) if half // tf > 1 else None
    ys = pl.pallas_call(
        functools.partial(_hy_fwd_kernel, seq=seq),
        grid=(nb, half // tf),
        in_specs=[table_spec,
                  pl.BlockSpec((None, 2, half, D_MODEL), lambda b, i: (b, 0, 0, 0), pipeline_mode=mode),
                  pl.BlockSpec((4, tf, D_MODEL), lambda b, i: (0, i, 0))],
        out_specs=pl.BlockSpec((None, 4, tf, D_MODEL), lambda b, i: (b, 0, i, 0)),
        out_shape=jax.ShapeDtypeStruct((nb, 4, half, D_MODEL), bf16),
        scratch_shapes=[pltpu.VMEM((2, D_MODEL), f32)],
        compiler_params=_params("parallel", "arbitrary"),
    )(fwd_table, zb, ks)

    tm = min(half, 256)
    pair_tile = _x_spec(2 * tm)
    return pl.pallas_call(
        functools.partial(_hy_inv_kernel, seq=seq),
        grid=(nb, half // tm),
        in_specs=[table_spec, _resident((2 * tm, 2 * tm)),
                  pl.BlockSpec((None, 2 * seq, D_MODEL), lambda b, i: (b, 0, 0)),
                  pair_tile, pair_tile, pair_tile, _mod_spec(trunk, layer), _ng_spec(layer),
                  _resident((1, D_MODEL)), _resident((D_MODEL, D_MODEL)), _resident((1, D_MODEL))],
        out_specs=pair_tile,
        out_shape=jax.ShapeDtypeStruct(x.shape, f32),
        compiler_params=_params("parallel", "parallel"),
    )(inv_table, jnp.asarray(_parity_permutation(2 * tm).T, dtype=bf16), ys.reshape(nb, 2 * seq, D_MODEL),
      z, x0, x, mod, norm_g, p["d"], p["w_out"], p["b_out"])


def _head_sum(t, ones_pair):
    tb = t.astype(bf16)
    cols = [jnp.dot(tb[:, p * PAIR_W:(p + 1) * PAIR_W], ones_pair, preferred_element_type=f32)
            for p in range(D_MODEL // PAIR_W)]
    return jnp.concatenate(cols, axis=1)


def _rw_pre_kernel(x_ref, xp_ref, xn_ref, mod_ref, ng_ref, mix_ref, wr_ref, wk_ref, wv_ref, g1_ref, g2_ref,
                   w0_ref, w1_ref, w2_ref, a0_ref, a1_ref, a2_ref, kkp_ref, kap_ref, rkp_ref, ones_ref,
                   r_ref, k_ref, v_ref, kk_ref, g_ref, lwf_ref, lwb_ref, af_ref, ab_ref, bn_ref):
    j = pl.program_id(1)
    g, sh, sc = ng_ref[0:1, :], mod_ref[0:1, :], mod_ref[1:2, :]
    h = _norm_mod(x_ref[...], g, sh, sc)
    prev_row = jnp.where(j == 0, 0.0, _norm_mod(xp_ref[...], g, sh, sc)[7:8, :])
    next_row = jnp.where(j == pl.num_programs(1) - 1, 0.0, _norm_mod(xn_ref[...], g, sh, sc)[0:1, :])
    h_prev, h_next = _shifted(h, prev_row, next_row)
    xx = 0.5 * (h_prev + h_next) - h

    def mixed(m):
        return (h + xx * mix_ref[m:m + 1, :]).astype(bf16)

    r = jnp.dot(mixed(0), wr_ref[...], preferred_element_type=f32)
    xw = mixed(1)
    k = jnp.dot(mixed(2), wk_ref[...], preferred_element_type=f32)
    v = jnp.dot(mixed(3), wv_ref[...], preferred_element_type=f32)
    xa = mixed(4)
    r_ref[...] = r
    k_ref[...] = k
    v_ref[...] = v
    gate = _sigmoid(jnp.dot(mixed(5), g1_ref[...], preferred_element_type=f32))
    g_ref[...] = jnp.dot(gate.astype(bf16), g2_ref[...], preferred_element_type=f32)
    ones = ones_ref[...]
    kk = k * kkp_ref[...]
    kk_ref[...] = kk * lax.rsqrt(jnp.maximum(_head_sum(kk * kk, ones), 1e-24))
    a_sum = None
    for dd, (lw_ref, a_ref) in enumerate(((lwf_ref, af_ref), (lwb_ref, ab_ref))):
        lora = jnp.tanh(jnp.dot(xw, w1_ref[dd], preferred_element_type=f32))
        wl = w0_ref[dd:dd + 1, :] + jnp.dot(lora.astype(bf16), w2_ref[dd], preferred_element_type=f32)
        lw_ref[...] = -math.exp(-0.5) * _sigmoid(wl)
        al = jnp.dot(xa, a1_ref[dd], preferred_element_type=f32)
        a = _sigmoid(a0_ref[dd:dd + 1, :] + jnp.dot(al.astype(bf16), a2_ref[dd], preferred_element_type=f32))
        a_ref[...] = a
        a_sum = a if a_sum is None else a_sum + a
    kd_sum = k * (2.0 + (a_sum - 2.0) * kap_ref[...])
    bn_ref[...] = _head_sum(r * kd_sum * rkp_ref[...], ones) * v


def _stack_heads(x, first_head):
    return jnp.concatenate([jnp.where(first_head, x, 0.0), jnp.where(first_head, 0.0, x)], axis=0).astype(bf16)


def _dot_nt(a, b):
    return lax.dot_general(a, b, (((1,), (1,)), ((), ())), preferred_element_type=f32)


def _dot_tn(a, b):
    return lax.dot_general(a, b, (((0,), (0,)), ((), ())), preferred_element_type=f32)


def _rw_scan_kernel(r_ref, k_ref, v_ref, kk_ref, lw_ref, a_ref, kap_ref, s0_ref, o_ref, s_ref, *, reverse):
    c = pl.program_id(1)
    C, W, G = SCAN_CHUNK, PAIR_W, SCAN_GROUP

    @pl.when(c == 0)
    def _():
        s_ref[...] = s0_ref[...]

    row = lax.broadcasted_iota(jnp.int32, (C, W), 0)
    lane = lax.broadcasted_iota(jnp.int32, (C, W), 1)
    col = lane & (C - 1)
    first_head = lane < RW_HEAD
    before = (col > row) if reverse else (col < row)
    before_eq = before | (col == row)
    eye = (col == row).astype(f32)
    row_w = lax.broadcasted_iota(jnp.int32, (W, W), 0)
    lane_w = lax.broadcasted_iota(jnp.int32, (W, W), 1)
    same_head = (row_w >= RW_HEAD) == (lane_w >= RW_HEAD)
    diag_w = row_w == lane_w
    merge_masks = []
    s = 1
    while s < C:
        sh = s.bit_length() - 1
        same_pair = (row >> (sh + 1)) == (col >> (sh + 1))
        later, earlier = ((row >> sh) & 1, (col >> sh) & 1)
        merge_masks.append(same_pair & (((later == 0) & (earlier == 1)) if reverse
                                        else ((later == 1) & (earlier == 0))))
        s *= 2

    row_c = lax.broadcasted_iota(jnp.int32, (G * C, G * C), 0)
    col_c = lax.broadcasted_iota(jnp.int32, (G * C, G * C), 1)
    chunk_bits = C.bit_length() - 1
    same_chunk = (row_c >> chunk_bits) == (col_c >> chunk_bits)
    tri = (same_chunk & ((col_c >= row_c) if reverse else (col_c <= row_c))).astype(bf16)
    lw_all = lw_ref[...]
    cum_all = _dot_exact_rhs_left(tri, lw_all)
    n_pairs = D_MODEL // W

    def pairs(x):
        return [x[g * C:(g + 1) * C, p * W:(p + 1) * W] for g in range(G) for p in range(n_pairs)]

    def per_chunk_rows(rows):
        return [rows[g][:, p * W:(p + 1) * W] for g in range(G) for p in range(n_pairs)]

    def each(fn, *lists):
        return [fn(*args) for args in zip(*lists)]

    def stack(x):
        return _stack_heads(x, first_head)

    def mm(x, y):
        return jnp.dot(x.astype(bf16), stack(y), preferred_element_type=f32)

    def mm2(x, y0, y1):
        return jnp.dot(x.astype(bf16), jnp.concatenate([stack(y0), stack(y1)], axis=1),
                       preferred_element_type=f32)

    r, k, v, kk, a = (pairs(ref[...]) for ref in (r_ref, k_ref, v_ref, kk_ref, a_ref))
    lw, cum = pairs(lw_all), pairs(cum_all)
    tot = per_chunk_rows([jnp.sum(lw_all[g * C:(g + 1) * C, :], axis=0, keepdims=True) for g in range(G)])
    ka = per_chunk_rows([kap_ref[...]] * G)
    kd = each(lambda k_, a_, ka_: k_ * (1.0 + (a_ - 1.0) * ka_), k, a, ka)
    b = each(lambda kk_, a_: kk_ * a_, kk, a)
    e_in = each(jnp.exp, cum)
    e_out = each(lambda c_: jnp.exp(-c_), cum)
    e_end = each(lambda t_, c_: jnp.exp(t_ - c_), tot, cum)
    kap = each(lambda kk_, c_, l_: kk_ * jnp.exp(c_ - l_), kk, cum, lw)
    mul = lambda x_, y_: x_ * y_
    kt, bt, rt, kh, bh = each(mul, kd, e_out), each(mul, b, e_out), each(mul, r, e_in), each(mul, kd, e_end), \
        each(mul, b, e_end)

    lhs = each(lambda x_, y_: jnp.concatenate([x_, y_], axis=0).astype(bf16), kap, rt)
    gram = each(lambda l_, y0_, y1_: _dot_nt(l_, jnp.concatenate([stack(y0_), stack(y1_)], axis=0)), lhs, kt, bt)
    m_kk = each(lambda g_: jnp.where(before, g_[:C, :W], 0.0), gram)
    n_kb = each(lambda g_: jnp.where(before, g_[:C, W:], 0.0), gram)
    a_qk = each(lambda g_: jnp.where(before_eq, g_[C:, :W], 0.0), gram)
    a_qb = each(lambda g_: jnp.where(before_eq, g_[C:, W:], 0.0), gram)

    tinv = each(lambda n_: eye - jnp.where(merge_masks[0], n_, 0.0), n_kb)
    for off in merge_masks[1:]:
        tn = each(lambda t_, n_: mm(t_, jnp.where(off, n_, 0.0)), tinv, n_kb)
        tinv = each(lambda t_, tn_: t_ - mm(tn_, t_), tinv, tn)

    mva = each(lambda m_, a_, v_: mm(jnp.concatenate([m_, a_], axis=0), v_), m_kk, a_qk, v)
    mv, av = each(lambda t_: t_[:C, :], mva), each(lambda t_: t_[C:, :], mva)
    tk = each(mm2, tinv, kap, mv)
    kt_c, w1 = each(lambda t_: t_[:, :W], tk), each(lambda t_: t_[:, W:], tk)
    qa = each(mm2, a_qb, kt_c, w1)
    q_eff = each(lambda rt_, qa_: rt_ - qa_[:, :W], rt, qa)
    o_intra = each(lambda av_, qa_: av_ - qa_[:, W:], av, qa)
    kb = each(lambda x_, b_: _dot_tn(x_.astype(bf16), b_.astype(bf16)), kt_c, bh)
    phi = each(lambda t_, kb_: jnp.where(same_head, jnp.where(diag_w, jnp.exp(t_), 0.0) - kb_, 0.0), tot, kb)
    ds_full = each(lambda v_, w_, kh_, bh_: _dot_tn(jnp.concatenate([v_, -w_], axis=0).astype(bf16),
                                                    jnp.concatenate([kh_, bh_], axis=0).astype(bf16)),
                   v, w1, kh, bh)
    d_s = each(lambda d_: jnp.where(first_head, d_[:RW_HEAD, :], d_[RW_HEAD:, :]), ds_full)
    state = [s_ref[:, p * W:(p + 1) * W] for p in range(n_pairs)]
    for g in (range(G - 1, -1, -1) if reverse else range(G)):
        sel = slice(g * n_pairs, (g + 1) * n_pairs)
        o = each(lambda q_, s_, oi_: _dot_nt(q_.astype(bf16), stack(s_)) + oi_, q_eff[sel], state, o_intra[sel])
        state = each(lambda s_, ph_, ds_: jnp.dot(s_.astype(bf16), ph_.astype(bf16),
                                                  preferred_element_type=f32) + ds_, state, phi[sel], d_s[sel])
        o_ref[g * C:(g + 1) * C, :] = jnp.concatenate(o, axis=1)
    s_ref[...] = jnp.concatenate(state, axis=1)


def _rw_post_kernel(of_ref, ob_ref, bn_ref, g_ref, x_ref, mod_ref, ng_ref, lw_ref, lb_ref, ones_ref,
                    wo_ref, o_ref):
    o = of_ref[...] + ob_ref[...]
    ones = ones_ref[...]
    inv_n = 1.0 / RW_HEAD
    dev = o - _head_sum(o, ones) * inv_n
    var = _head_sum(dev * dev, ones) * inv_n
    on = dev * lax.rsqrt(var + RW_LN_EPS) * lw_ref[...] + lb_ref[...] + bn_ref[...]
    y = jnp.dot((on * g_ref[...]).astype(bf16), wo_ref[...], preferred_element_type=f32)
    o_ref[...] = _residual(x_ref[...], y, ng_ref[1:2, :], mod_ref[2:3, :])


@functools.lru_cache(maxsize=None)
def _pair_ones():
    idx = np.arange(PAIR_W) // RW_HEAD
    return np.asarray(idx[:, None] == idx[None, :], dtype=bf16)


def _rwkv_layer(x, trunk, layer, mod, norm_g, p, s0_fwd, s0_bwd):
    nb, seq = trunk.batch, trunk.seq
    tl = _tile_rows(trunk, 256)
    x8 = x.reshape(nb, seq // 8, 8, D_MODEL)
    prev_spec, next_spec = _halo_specs(trunk, tl)
    lora_w, lora_a, lora_g = p["w1"].shape[-1], p["a1"].shape[-1], p["g1"].shape[-1]
    ones = jnp.asarray(_pair_ones())
    outs = pl.pallas_call(
        _rw_pre_kernel,
        grid=(nb, seq // tl),
        in_specs=[_x_spec(tl), prev_spec, next_spec, _mod_spec(trunk, layer), _ng_spec(layer),
                  _resident((6, D_MODEL)),
                  _resident((D_MODEL, D_MODEL)), _resident((D_MODEL, D_MODEL)), _resident((D_MODEL, D_MODEL)),
                  _resident((D_MODEL, lora_g)), _resident((lora_g, D_MODEL)),
                  _resident((2, D_MODEL)), _resident((2, D_MODEL, lora_w)), _resident((2, lora_w, D_MODEL)),
                  _resident((2, D_MODEL)), _resident((2, D_MODEL, lora_a)), _resident((2, lora_a, D_MODEL)),
                  _resident((1, D_MODEL)), _resident((1, D_MODEL)), _resident((1, D_MODEL)),
                  _resident((PAIR_W, PAIR_W))],
        out_specs=[_x_spec(tl)] * 10,
        out_shape=[jax.ShapeDtypeStruct(x.shape, f32)] * 10,
        compiler_params=_params("parallel", "parallel"),
    )(x, x8, x8, mod, norm_g, p["mix"], p["wr"], p["wk"], p["wv"], p["g1"], p["g2"],
      p["w0"], p["w1"], p["w2"], p["a0"], p["a1"], p["a2"], p["kk"], p["ka"], p["rk"], ones)
    r, k, v, kk, gate, lw_f, lw_b, a_f, a_b, bonus = outs

    def state_in(s):
        return s.transpose(0, 2, 1, 3).reshape(nb, RW_HEAD, D_MODEL)

    def state_out(s):
        return s.reshape(nb, RW_HEAD, RW_HEADS, RW_HEAD).transpose(0, 2, 1, 3)

    rows = SCAN_CHUNK * SCAN_GROUP
    n_chunks = seq // rows
    state_spec = pl.BlockSpec((None, RW_HEAD, D_MODEL), lambda b, c: (b, 0, 0))
    results = []
    for reverse, lw, a, s0 in ((False, lw_f, a_f, s0_fwd), (True, lw_b, a_b, s0_bwd)):
        if reverse:
            chunk_spec = pl.BlockSpec((None, rows, D_MODEL), lambda b, c: (b, n_chunks - 1 - c, 0))
        else:
            chunk_spec = pl.BlockSpec((None, rows, D_MODEL), lambda b, c: (b, c, 0))
        results.append(pl.pallas_call(
            functools.partial(_rw_scan_kernel, reverse=reverse),
            grid=(nb, n_chunks),
            in_specs=[chunk_spec] * 6 + [_resident((1, D_MODEL)), state_spec],
            out_specs=[chunk_spec, state_spec],
            out_shape=[jax.ShapeDtypeStruct(x.shape, f32),
                       jax.ShapeDtypeStruct((nb, RW_HEAD, D_MODEL), f32)],
            compiler_params=_params("parallel", "arbitrary"),
        )(r, k, v, kk, lw, a, p["ka"], state_in(s0)))
    (o_f, s_f), (o_b, s_b) = results

    y = pl.pallas_call(
        _rw_post_kernel,
        grid=(nb, seq // tl),
        in_specs=[_x_spec(tl)] * 5 + [_mod_spec(trunk, layer), _ng_spec(layer),
                                      _resident((1, D_MODEL)), _resident((1, D_MODEL)),
                                      _resident((PAIR_W, PAIR_W)), _resident((D_MODEL, D_MODEL))],
        out_specs=_x_spec(tl),
        out_shape=jax.ShapeDtypeStruct(x.shape, f32),
        compiler_params=_params("parallel", "parallel"),
    )(o_f, o_b, bonus, gate, x, mod, norm_g, p["lnx_w"], p["lnx_b"], ones, p["wo"])
    return y, state_out(s_f), state_out(s_b)


def kernel(x_prompt, x_sample, state_wkv_fwd, state_wkv_bwd, c, c_ctx, ada_w, ada_b, norm_g, ffn_w1, ffn_w3, ffn_w2, fn_w_out, fn_b_out, hy_w_in, hy_b_in, hy_conv_w, hy_conv_b, hy_f_w1, hy_f_b1, hy_f_freq, hy_f_w2, hy_f_b2, hy_f_w3, hy_d, hy_w_out, hy_b_out, rw_mix, rw_wr, rw_wk, rw_wv, rw_wo, rw_w0, rw_w1, rw_w2, rw_a0, rw_a1, rw_a2, rw_g1, rw_g2, rw_kk, rw_ka, rw_rk, rw_lnx_w, rw_lnx_b):
    n_ctx, n_dec = x_prompt.shape[0], x_sample.shape[0]
    assert 1 + n_dec <= MOD_ROWS
    trunks = (Trunk(n_ctx, x_prompt.shape[1], 0, False), Trunk(n_dec, x_sample.shape[1], 1, True))
    cond = jnp.zeros((MOD_ROWS, D_MODEL), f32).at[0].set(c_ctx).at[1:1 + n_dec].set(c)
    mod = _adaln(cond, ada_w, ada_b)

    n_rwkv = rw_w0.shape[0]
    zero_state = jnp.zeros((n_ctx, n_rwkv, RW_HEADS, RW_HEAD, RW_HEAD), f32)
    states = ((zero_state, zero_state), (state_wkv_fwd, state_wkv_bwd))
    xs = [x_prompt, x_sample]
    ffn_bf16 = [w.astype(bf16) for w in (ffn_w1, ffn_w3, ffn_w2)]
    new_fwd, new_bwd = [], []
    for i in range(DEPTH):
        kind, j = i % N_MIXERS, i // N_MIXERS
        if kind == 0:
            w_out, b_out = fn_w_out[j].astype(bf16), fn_b_out[j][None]
            xs = [_fourier_layer(x, t, i, mod, norm_g, w_out, b_out) for x, t in zip(xs, trunks)]
        elif kind == 1:
            w1_pad = jnp.zeros((HY_EMB_PAD, HY_FILT), f32).at[:HY_EMB].set(hy_f_w1[j])
            p = dict(w_in=hy_w_in[j].astype(bf16), b_in=hy_b_in[j][None], conv_w=hy_conv_w[j],
                     conv_b=hy_conv_b[j][None], f_w1=w1_pad, f_b1=hy_f_b1[j][None], f_freq=hy_f_freq[j],
                     f_w2=hy_f_w2[j], f_b2=hy_f_b2[j][None], f_w3=hy_f_w3[j], d=hy_d[j][None],
                     w_out=hy_w_out[j].astype(bf16), b_out=hy_b_out[j][None])
            xs = [_hyena_layer(x, t, i, mod, norm_g, p) for x, t in zip(xs, trunks)]
        else:
            per_head = lambda t: t.reshape(1, D_MODEL)
            p = dict(mix=rw_mix[j], wr=rw_wr[j].astype(bf16), wk=rw_wk[j].astype(bf16),
                     wv=rw_wv[j].astype(bf16), wo=rw_wo[j].astype(bf16), w0=rw_w0[j],
                     w1=rw_w1[j].astype(bf16), w2=rw_w2[j].astype(bf16), a0=rw_a0[j],
                     a1=rw_a1[j].astype(bf16), a2=rw_a2[j].astype(bf16), g1=rw_g1[j].astype(bf16),
                     g2=rw_g2[j].astype(bf16), kk=per_head(rw_kk[j]), ka=per_head(rw_ka[j]),
                     rk=per_head(rw_rk[j]), lnx_w=rw_lnx_w[j][None], lnx_b=rw_lnx_b[j][None])
            outs = [_rwkv_layer(x, t, i, mod, norm_g, p, sf[:, j], sb[:, j])
                    for x, t, (sf, sb) in zip(xs, trunks, states)]
            xs = [o[0] for o in outs]
            new_fwd.append(outs[0][1])
            new_bwd.append(outs[0][2])
        xs = [_ffn(x, t, i, mod, norm_g, ffn_bf16[0], ffn_bf16[1], ffn_bf16[2]) for x, t in zip(xs, trunks)]
    return xs[0], xs[1], jnp.stack(new_fwd, axis=1), jnp.stack(new_bwd, axis=1)
```

```python
import functools
import math
from typing import NamedTuple

import numpy as np
import jax
import jax.numpy as jnp
from jax import lax
from jax.experimental import pallas as pl
from jax.experimental.pallas import tpu as pltpu

f32 = jnp.float32
bf16 = jnp.bfloat16

D_MODEL = 1024
DEPTH = 4
N_MIXERS = 3
D_FF = 2816
NORM_EPS = 1e-6
FN_GROUP_W = 256
FN_HALO = 16
HY_EMB = 33
HY_EMB_PAD = 128
HY_BANDS = 16
HY_FILT = 64
HY_TARGET = 1e-2
HY_FAST = 0.3
HY_SLOW = 1.5
RW_HEAD = 64
RW_HEADS = 16
RW_LN_EPS = 64e-5
SCAN_CHUNK = 64
SCAN_GROUP = 4
PAIR_W = 2 * RW_HEAD
assert SCAN_CHUNK == RW_HEAD, "the scan keeps (chunk, chunk) and (chunk, head) tiles in one lane layout"
MOD_ROWS = 8
VMEM_LIMIT = 56 * 1024 * 1024


class Trunk(NamedTuple):
    batch: int
    seq: int
    mod_base: int
    per_batch_mod: bool


def _params(*sem):
    return pltpu.CompilerParams(dimension_semantics=sem, vmem_limit_bytes=VMEM_LIMIT)


def _resident(shape):
    nd = len(shape)
    return pl.BlockSpec(shape, lambda *_: (0,) * nd, pipeline_mode=pl.Buffered(1))


def _layer_resident(shape, layer):
    nd = len(shape)
    return pl.BlockSpec((None,) + tuple(shape), lambda *_: (layer,) + (0,) * nd, pipeline_mode=pl.Buffered(1))


def _split(x):
    hi = x.astype(bf16)
    lo = (x - hi.astype(f32)).astype(bf16)
    return hi, lo


def _dot3(a, b):
    ah, al = _split(a)
    bh, bl = _split(b)
    d = functools.partial(jnp.dot, preferred_element_type=f32)
    return d(ah, bh) + (d(ah, bl) + d(al, bh))


def _dot_exact_rhs_left(a_exact, b):
    b0, b1 = _split(b)
    d = functools.partial(jnp.dot, preferred_element_type=f32)
    return d(a_exact, b0) + d(a_exact, b1)


def _sigmoid(x):
    return 0.5 * jnp.tanh(0.5 * x) + 0.5


def _norm_mod(x, g, shift, scale):
    ms = jnp.mean(x * x, axis=-1, keepdims=True)
    return (x * lax.rsqrt(ms + NORM_EPS)) * (g * (1.0 + scale)) + shift


def _residual(x, y, g, gate):
    ms = jnp.mean(y * y, axis=-1, keepdims=True)
    return x + gate * (y * lax.rsqrt(ms + NORM_EPS) * g)


def _shifted(h, prev_row, next_row):
    rows = h.shape[0]
    ridx = lax.broadcasted_iota(jnp.int32, h.shape, 0)
    h_prev = jnp.where(ridx == 0, prev_row, pltpu.roll(h, 1, 0))
    h_next = jnp.where(ridx == rows - 1, next_row, pltpu.roll(h, rows - 1, 0))
    return h_prev, h_next


def _tile_rows(trunk, cap):
    return min(trunk.seq, cap)


def _x_spec(tl):
    return pl.BlockSpec((None, tl, D_MODEL), lambda b, j: (b, j, 0))


def _halo_specs(trunk, tl):
    g = tl // 8
    last = trunk.seq // 8 - 1
    prev = pl.BlockSpec((None, None, 8, D_MODEL), lambda b, j: (b, jnp.maximum(j * g - 1, 0), 0, 0))
    nxt = pl.BlockSpec((None, None, 8, D_MODEL), lambda b, j: (b, jnp.minimum((j + 1) * g, last), 0, 0))
    return prev, nxt


def _mod_spec(trunk, layer):
    if trunk.per_batch_mod:
        return pl.BlockSpec((None, None, 6, D_MODEL), lambda b, *_: (layer, trunk.mod_base + b, 0, 0))
    return pl.BlockSpec((None, None, 6, D_MODEL), lambda b, *_: (layer, trunk.mod_base, 0, 0))


def _ng_spec(layer):
    return pl.BlockSpec((None, 4, D_MODEL), lambda *_: (layer, 0, 0))


def _mod_kernel(c_ref, w_ref, b_ref, o_ref):
    c = c_ref[...]
    o_ref[...] = _dot3(c * _sigmoid(c), w_ref[...]) + b_ref[...]


def _adaln(cond, ada_w, ada_b):
    tn = 1536
    out = pl.pallas_call(
        _mod_kernel,
        grid=(DEPTH, 6 * D_MODEL // tn),
        in_specs=[pl.BlockSpec((MOD_ROWS, D_MODEL), lambda l, j: (0, 0)),
                  pl.BlockSpec((None, D_MODEL, tn), lambda l, j: (l, 0, j)),
                  pl.BlockSpec((None, 1, tn), lambda l, j: (l, 0, j))],
        out_specs=pl.BlockSpec((None, MOD_ROWS, tn), lambda l, j: (l, 0, j)),
        out_shape=jax.ShapeDtypeStruct((DEPTH, MOD_ROWS, 6 * D_MODEL), f32),
        compiler_params=_params("arbitrary", "arbitrary"),
    )(cond, ada_w, ada_b.reshape(DEPTH, 1, 6 * D_MODEL))
    return out.reshape(DEPTH, MOD_ROWS, 6, D_MODEL)


def _ffn_kernel(x_ref, mod_ref, ng_ref, w1_ref, w3_ref, w2_ref, o_ref):
    x = x_ref[...]
    h = _norm_mod(x, ng_ref[2:3, :], mod_ref[3:4, :], mod_ref[4:5, :]).astype(bf16)
    a = jnp.dot(h, w1_ref[...], preferred_element_type=f32)
    b = jnp.dot(h, w3_ref[...], preferred_element_type=f32)
    gated = (a * _sigmoid(a) * b).astype(bf16)
    y = jnp.dot(gated, w2_ref[...], preferred_element_type=f32)
    o_ref[...] = _residual(x, y, ng_ref[3:4, :], mod_ref[5:6, :])


def _merge_sequences(x, trunk):
    if trunk.per_batch_mod:
        return x, trunk
    return x.reshape(1, -1, D_MODEL), Trunk(1, trunk.batch * trunk.seq, trunk.mod_base, False)


def _ffn(x, trunk, layer, mod, norm_g, w1, w3, w2):
    shape = x.shape
    x, trunk = _merge_sequences(x, trunk)
    tl = _tile_rows(trunk, 512)
    return pl.pallas_call(
        _ffn_kernel,
        grid=(trunk.batch, trunk.seq // tl),
        in_specs=[_x_spec(tl), _mod_spec(trunk, layer), _ng_spec(layer),
                  _layer_resident((D_MODEL, D_FF), layer), _layer_resident((D_MODEL, D_FF), layer),
                  _layer_resident((D_FF, D_MODEL), layer)],
        out_specs=_x_spec(tl),
        out_shape=jax.ShapeDtypeStruct(x.shape, f32),
        compiler_params=_params("parallel", "parallel"),
    )(x, mod, norm_g, w1, w3, w2).reshape(shape)


@functools.lru_cache(maxsize=None)
def _channel_dft():
    w = FN_GROUP_W
    idx = np.arange(w)
    ang = 2.0 * np.pi * ((idx[:, None] * idx[None, :]) % w) / w
    return (np.concatenate([np.cos(ang), np.sin(ang)], axis=1) / math.sqrt(w)).astype(np.float32)


def _table_rows(cs_ref, tile, rows):
    return cs_ref[pl.ds(pl.multiple_of(tile * rows, rows), rows), :]


def _mxu_table(table):
    return jnp.asarray(table).astype(bf16)


def _fn_a_kernel(x_ref, mod_ref, ng_ref, w_ref, a_ref, b_ref):
    h = _norm_mod(x_ref[...], ng_ref[0:1, :], mod_ref[0:1, :], mod_ref[1:2, :]).astype(bf16)
    w = FN_GROUP_W
    for g in range(D_MODEL // w):
        ab = jnp.dot(h[:, g * w:(g + 1) * w], w_ref[...], preferred_element_type=f32).astype(bf16)
        a_ref[:, g * w:(g + 1) * w] = ab[:, :w]
        b_ref[:, g * w:(g + 1) * w] = ab[:, w:]


@functools.lru_cache(maxsize=None)
def _fourier_half_table(seq):
    k = np.arange(seq // 2 + FN_HALO)
    t = np.arange(seq)
    ang = 2.0 * np.pi * ((k[:, None] * t[None, :]) % seq) / seq
    return (np.concatenate([np.cos(ang), np.sin(ang)], axis=1) / math.sqrt(seq)).astype(np.float32)


@functools.lru_cache(maxsize=None)
def _row_reversal(tm):
    out = np.zeros((tm, tm + FN_HALO), np.float32)
    out[np.arange(tm), tm - np.arange(tm)] = 1.0
    return np.asarray(out, dtype=bf16)


def _fn_b_kernel(cs_ref, rev_ref, a_ref, b_ref, x_ref, mod_ref, ng_ref, wo_ref, bo_ref, o_ref, mirror_ref, *,
                 seq):
    s = pl.program_id(1)
    half_tiles = pl.num_programs(1) // 2
    n_seq, tm = x_ref.shape[0], x_ref.shape[1]
    src_rows = tm + FN_HALO

    def finish(f_bf16):
        f = jnp.concatenate(f_bf16, axis=0)
        y = jnp.dot(f, wo_ref[...], preferred_element_type=f32) + bo_ref[...]
        x = x_ref[...].reshape(n_seq * tm, D_MODEL)
        o_ref[...] = _residual(x, y, ng_ref[1:2, :], mod_ref[2:3, :]).reshape(n_seq, tm, D_MODEL)

    @pl.when(s < half_tiles)
    def _():
        first = pl.multiple_of(s * tm, tm)
        rows = cs_ref[pl.ds(first, src_rows), :]
        out = []
        for i in range(n_seq):
            p = jnp.dot(rows[:, :seq], a_ref[i], preferred_element_type=f32)
            q = jnp.dot(rows[:, seq:], b_ref[i], preferred_element_type=f32)
            mirror_ref[i, pl.ds(first, src_rows), :] = (p + q).astype(bf16)
            out.append((p - q)[:tm, :].astype(bf16))
        finish(out)

    @pl.when(s >= half_tiles)
    def _():
        first = pl.multiple_of((2 * half_tiles - 1 - s) * tm, tm)
        finish([jnp.dot(rev_ref[...], mirror_ref[i, pl.ds(first, src_rows), :],
                        preferred_element_type=f32).astype(bf16) for i in range(n_seq)])


def _fourier_layer(x, trunk, layer, mod, norm_g, w_out, b_out):
    nb, seq = trunk.batch, trunk.seq
    xm, merged = _merge_sequences(x, trunk)
    tl = _tile_rows(merged, 512)
    a, b = pl.pallas_call(
        _fn_a_kernel,
        grid=(merged.batch, merged.seq // tl),
        in_specs=[_x_spec(tl), _mod_spec(merged, layer), _ng_spec(layer),
                  _resident((FN_GROUP_W, 2 * FN_GROUP_W))],
        out_specs=[_x_spec(tl)] * 2,
        out_shape=[jax.ShapeDtypeStruct(xm.shape, bf16)] * 2,
        compiler_params=_params("parallel", "parallel"),
    )(xm, mod, norm_g, _mxu_table(_channel_dft()))

    tm = min(seq // 2, 512)
    half_rows = seq // 2 + FN_HALO
    n_seq = 1 if trunk.per_batch_mod else math.gcd(nb, max(1, 512 // tm))
    tile = pl.BlockSpec((n_seq, tm, D_MODEL), lambda b, i: (b, i, 0))
    whole = pl.BlockSpec((n_seq, seq, D_MODEL), lambda b, i: (b, 0, 0))
    return pl.pallas_call(
        functools.partial(_fn_b_kernel, seq=seq),
        grid=(nb // n_seq, seq // tm),
        in_specs=[_resident((half_rows, 2 * seq)), _resident((tm, tm + FN_HALO)), whole, whole,
                  tile, _mod_spec(trunk, layer), _ng_spec(layer),
                  _resident((D_MODEL, D_MODEL)), _resident((1, D_MODEL))],
        out_specs=tile,
        out_shape=jax.ShapeDtypeStruct(x.shape, f32),
        scratch_shapes=[pltpu.VMEM((n_seq, half_rows, D_MODEL), bf16)],
        compiler_params=_params("parallel", "arbitrary"),
    )(_mxu_table(_fourier_half_table(seq)), jnp.asarray(_row_reversal(tm)), a.reshape(x.shape), b.reshape(x.shape),
      x, mod, norm_g, w_out, b_out)


@functools.lru_cache(maxsize=None)
def _hyena_features(seq):
    t = np.linspace(0.0, 1.0, seq)[:, None]
    ang = 2.0 * np.pi * np.arange(seq)[:, None] / seq
    bands = np.linspace(1e-4, HY_BANDS - 1, HY_BANDS)[None]
    z = np.concatenate([t, np.cos(bands * ang), -np.sin(bands * ang)], axis=-1)
    out = np.zeros((seq, HY_EMB_PAD), np.float32)
    out[:, :HY_EMB] = np.concatenate([z[0::2], z[1::2]], axis=0)
    return out


@functools.lru_cache(maxsize=None)
def _hyena_decay_rates():
    d = np.linspace(math.log(HY_TARGET) / HY_FAST, math.log(HY_TARGET) / HY_SLOW, D_MODEL)
    return np.abs(d)[None].astype(np.float32)


@functools.lru_cache(maxsize=None)
def _hyena_tables(seq):
    half = seq // 2
    p = np.arange(half)[:, None]
    m = np.arange(half)[None, :]
    ang_e = np.pi * ((p * 2 * m) % (2 * seq)) / seq
    ang_o = np.pi * ((p * (2 * m + 1)) % (2 * seq)) / seq
    fwd = np.concatenate([np.cos(ang_e), np.cos(ang_o), np.sin(ang_e), np.sin(ang_o)], axis=1)
    inv = np.concatenate([np.cos(ang_e).T, -np.sin(ang_e).T, np.cos(ang_o).T, -np.sin(ang_o).T], axis=1)
    return fwd.astype(np.float32), inv.astype(np.float32)


@functools.lru_cache(maxsize=None)
def _parity_permutation(n):
    out = np.zeros((n, n), np.float32)
    half = n // 2
    out[np.arange(half), 2 * np.arange(half)] = 1.0
    out[half + np.arange(half), 2 * np.arange(half) + 1] = 1.0
    return out


def _hy_in_kernel(x_ref, xp_ref, xn_ref, mod_ref, ng_ref, w_ref, b_ref, cw_ref, cb_ref, sel_ref,
                  x0_ref, z_ref, zb_ref):
    j = pl.program_id(1)
    g, sh, sc = ng_ref[0:1, :], mod_ref[0:1, :], mod_ref[1:2, :]
    tl = x_ref.shape[0]
    rows = jnp.concatenate([x_ref[...], xp_ref[...], xn_ref[...]], axis=0)
    u_all = jnp.dot(_norm_mod(rows, g, sh, sc).astype(bf16), w_ref[...], preferred_element_type=f32) + b_ref[...]
    u = u_all[:tl, :]
    prev_row = jnp.where(j == 0, 0.0, u_all[tl + 7:tl + 8, :])
    next_row = jnp.where(j == pl.num_programs(1) - 1, 0.0, u_all[tl + 8:tl + 9, :])
    u_prev, u_next = _shifted(u, prev_row, next_row)
    uc = u_prev * cw_ref[0:1, :] + u * cw_ref[1:2, :] + u_next * cw_ref[2:3, :] + cb_ref[...]
    x0_ref[...] = uc[:, :D_MODEL]
    z = uc[:, 2 * D_MODEL:] * uc[:, D_MODEL:2 * D_MODEL]
    z_ref[...] = z
    split = jnp.dot(sel_ref[...], z.astype(bf16), preferred_element_type=f32).astype(bf16)
    zb_ref[0] = split[:tl // 2, :]
    zb_ref[1] = split[tl // 2:, :]


def _alternating_sign(first, rows):
    t = first + lax.broadcasted_iota(jnp.int32, (rows, 1), 0)
    return (1 - 2 * (t & 1)).astype(f32)


def _parity_dft(rows, even, odd, half):
    d = functools.partial(jnp.dot, preferred_element_type=f32)
    return (d(rows[:, :half], even), d(rows[:, half:2 * half], odd),
            d(rows[:, 2 * half:3 * half], even), d(rows[:, 3 * half:], odd))


def _hy_filter_kernel(feat_ref, w1_ref, b1_ref, fq_ref, w2_ref, b2_ref, w3f_ref, w3b_ref, rate_ref,
                      tab_ref, o_ref, hdn_ref, *, seq):
    half = seq // 2

    @pl.when(pl.program_id(0) == 0)
    def _():
        hdn = jnp.sin(fq_ref[0:1, :] * (_dot3(feat_ref[...], w1_ref[...]) + b1_ref[...]))
        hdn_ref[...] = jnp.sin(fq_ref[1:2, :] * (_dot3(hdn, w2_ref[...]) + b2_ref[...]))

    hdn = hdn_ref[...]
    rows = lax.broadcasted_iota(jnp.int32, (seq, 1), 0)
    lag = jnp.where(rows < half, 2 * rows, 2 * (rows - half) + 1)
    win = jnp.exp(-(lag.astype(f32) * (1.0 / (seq - 1))) * rate_ref[...])
    k_fwd = _dot3(hdn, w3f_ref[...]) * win
    k_bwd = _dot3(hdn, w3b_ref[...]) * win
    both = k_fwd + k_bwd
    diff = k_fwd - k_bwd
    ec, oc, _, _ = _parity_dft(tab_ref[...], both[:half].astype(bf16), both[half:].astype(bf16), half)
    _, _, es, os_ = _parity_dft(tab_ref[...], diff[:half].astype(bf16), diff[half:].astype(bf16), half)
    sign = _alternating_sign(0, half)
    mid_re = jnp.sum(both[:half] * sign, axis=0, keepdims=True)
    mid_im = -jnp.sum(diff[half:] * sign, axis=0, keepdims=True)
    first = lax.broadcasted_iota(jnp.int32, (half, 1), 0) == 0
    wgt = jnp.where(first, 0.5 / seq, 1.0 / seq)
    o_ref[0] = (ec + oc) * wgt
    o_ref[1] = jnp.where(first, mid_re * (1.0 / seq), -(es + os_) * wgt)
    o_ref[2] = (ec - oc) * wgt
    o_ref[3] = jnp.where(first, mid_im * (1.0 / seq), (es - os_) * wgt)


def _hy_fwd_kernel(tab_ref, z_ref, ks_ref, o_ref, mid_ref, *, seq):
    i = pl.program_id(1)
    half = seq // 2
    tf = o_ref.shape[1]
    z_even, z_odd = z_ref[0], z_ref[1]
    ec, oc, es, os_ = _parity_dft(_table_rows(tab_ref, i, tf), z_even, z_odd, half)
    zre_lo, zim_lo, zre_hi, zim_hi = ec + oc, -(es + os_), ec - oc, es - os_
    kre_lo, kim_lo, kre_hi, kim_hi = ks_ref[0], ks_ref[1], ks_ref[2], ks_ref[3]
    yre_lo = zre_lo * kre_lo - zim_lo * kim_lo
    yim_lo = zre_lo * kim_lo + zim_lo * kre_lo
    yre_hi = zre_hi * kre_hi - zim_hi * kim_hi
    yim_hi = zre_hi * kim_hi + zim_hi * kre_hi

    @pl.when(i == 0)
    def _():
        sign = _alternating_sign(0, half)
        mid_re = jnp.sum(z_even.astype(f32) * sign, axis=0, keepdims=True)
        mid_im = -jnp.sum(z_odd.astype(f32) * sign, axis=0, keepdims=True)
        k_re, k_im = ks_ref[1, 0:1, :], ks_ref[3, 0:1, :]
        mid_ref[0:1, :] = mid_re * k_re - mid_im * k_im
        mid_ref[1:2, :] = mid_re * k_im + mid_im * k_re

    @pl.when(i != 0)
    def _():
        mid_ref[...] = jnp.zeros_like(mid_ref)

    packed = (i * tf + lax.broadcasted_iota(jnp.int32, (tf, 1), 0)) == 0
    o_ref[0] = (yre_lo + yre_hi).astype(bf16)
    o_ref[1] = jnp.where(packed, mid_ref[0:1, :], yim_lo - yim_hi).astype(bf16)
    o_ref[2] = (yre_lo - yre_hi).astype(bf16)
    o_ref[3] = jnp.where(packed, mid_ref[1:2, :], yim_lo + yim_hi).astype(bf16)


def _hy_inv_kernel(tab_ref, il_ref, ys_ref, z_ref, x0_ref, x_ref, mod_ref, ng_ref, d_ref, wo_ref, bo_ref, o_ref, *,
                   seq):
    i = pl.program_id(1)
    tm = x_ref.shape[0] // 2
    half = seq // 2
    rows = _table_rows(tab_ref, i, tm)
    sign = _alternating_sign(i * tm, tm)
    conv_even = (jnp.dot(rows[:, :seq], ys_ref[:seq, :], preferred_element_type=f32)
                 + sign * ys_ref[half:half + 1, :].astype(f32))
    conv_odd = (jnp.dot(rows[:, seq:], ys_ref[seq:, :], preferred_element_type=f32)
                - sign * ys_ref[3 * half:3 * half + 1, :].astype(f32))
    conv = jnp.dot(il_ref[...], jnp.concatenate([conv_even, conv_odd], axis=0).astype(bf16),
                   preferred_element_type=f32)
    y = conv + z_ref[...] * d_ref[...]
    out = jnp.dot((y * x0_ref[...]).astype(bf16), wo_ref[...], preferred_element_type=f32) + bo_ref[...]
    o_ref[...] = _residual(x_ref[...], out, ng_ref[1:2, :], mod_ref[2:3, :])


def _hyena_layer(x, trunk, layer, mod, norm_g, p):
    nb, seq = trunk.batch, trunk.seq
    tl = _tile_rows(trunk, 256)
    x8 = x.reshape(nb, seq // 8, 8, D_MODEL)
    prev_spec, next_spec = _halo_specs(trunk, tl)
    x0, z, zb = pl.pallas_call(
        _hy_in_kernel,
        grid=(nb, seq // tl),
        in_specs=[_x_spec(tl), prev_spec, next_spec, _mod_spec(trunk, layer), _ng_spec(layer),
                  _resident((D_MODEL, 3 * D_MODEL)), _resident((1, 3 * D_MODEL)),
                  _resident((3, 3 * D_MODEL)), _resident((1, 3 * D_MODEL)), _resident((tl, tl))],
        out_specs=[_x_spec(tl), _x_spec(tl),
                   pl.BlockSpec((None, 2, tl // 2, D_MODEL), lambda b, j: (b, 0, j, 0))],
        out_shape=[jax.ShapeDtypeStruct(x.shape, f32)] * 2
        + [jax.ShapeDtypeStruct((nb, 2, seq // 2, D_MODEL), bf16)],
        compiler_params=_params("parallel", "parallel"),
    )(x, x8, x8, mod, norm_g, p["w_in"], p["b_in"], p["conv_w"], p["conv_b"],
      jnp.asarray(_parity_permutation(tl), dtype=bf16))

    half = seq // 2
    fwd_table, inv_table = (_mxu_table(t) for t in _hyena_tables(seq))
    table_spec = _resident((half, 2 * seq))
    tn = 256
    nblk = D_MODEL // tn
    ks = pl.pallas_call(
        functools.partial(_hy_filter_kernel, seq=seq),
        grid=(nblk,),
        in_specs=[_resident((seq, HY_EMB_PAD)), _resident((HY_EMB_PAD, HY_FILT)), _resident((1, HY_FILT)),
                  _resident((2, HY_FILT)), _resident((HY_FILT, HY_FILT)), _resident((1, HY_FILT)),
                  pl.BlockSpec((HY_FILT, tn), lambda j: (0, j)),
                  pl.BlockSpec((HY_FILT, tn), lambda j: (0, nblk + j)),
                  pl.BlockSpec((1, tn), lambda j: (0, j)),
                  table_spec],
        out_specs=pl.BlockSpec((4, half, tn), lambda j: (0, 0, j)),
        out_shape=jax.ShapeDtypeStruct((4, half, D_MODEL), f32),
        scratch_shapes=[pltpu.VMEM((seq, HY_FILT), f32)],
        compiler_params=_params("arbitrary"),
    )(jnp.asarray(_hyena_features(seq)), p["f_w1"], p["f_b1"], p["f_freq"], p["f_w2"], p["f_b2"],
      p["f_w3"], p["f_w3"], jnp.asarray(_hyena_decay_rates()), fwd_table)

    tf = min(half, 512)
    mode = pl.Buffered(1) if half // tf > 1 else None
    ys = pl.pallas_call(
        functools.partial(_hy_fwd_kernel, seq=seq),
        grid=(nb, half // tf),
        in_specs=[table_spec,
                  pl.BlockSpec((None, 2, half, D_MODEL), lambda b, i: (b, 0, 0, 0), pipeline_mode=mode),
                  pl.BlockSpec((4, tf, D_MODEL), lambda b, i: (0, i, 0))],
        out_specs=pl.BlockSpec((None, 4, tf, D_MODEL), lambda b, i: (b, 0, i, 0)),
        out_shape=jax.ShapeDtypeStruct((nb, 4, half, D_MODEL), bf16),
        scratch_shapes=[pltpu.VMEM((2, D_MODEL), f32)],
        compiler_params=_params("parallel", "arbitrary"),
    )(fwd_table, zb, ks)

    tm = min(half, 256)
    pair_tile = _x_spec(2 * tm)
    return pl.pallas_call(
        functools.partial(_hy_inv_kernel, seq=seq),
        grid=(nb, half // tm),
        in_specs=[table_spec, _resident((2 * tm, 2 * tm)),
                  pl.BlockSpec((None, 2 * seq, D_MODEL), lambda b, i: (b, 0, 0)),
                  pair_tile, pair_tile, pair_tile, _mod_spec(trunk, layer), _ng_spec(layer),
                  _resident((1, D_MODEL)), _resident((D_MODEL, D_MODEL)), _resident((1, D_MODEL))],
        out_specs=pair_tile,
        out_shape=jax.ShapeDtypeStruct(x.shape, f32),
        compiler_params=_params("parallel", "parallel"),
    )(inv_table, jnp.asarray(_parity_permutation(2 * tm).T, dtype=bf16), ys.reshape(nb, 2 * seq, D_MODEL),
      z, x0, x, mod, norm_g, p["d"], p["w_out"], p["b_out"])


def _head_sum(t, ones_pair):
    tb = t.astype(bf16)
    cols = [jnp.dot(tb[:, p * PAIR_W:(p + 1) * PAIR_W], ones_pair, preferred_element_type=f32)
            for p in range(D_MODEL // PAIR_W)]
    return jnp.concatenate(cols, axis=1)


def _rw_pre_kernel(x_ref, xp_ref, xn_ref, mod_ref, ng_ref, mix_ref, wr_ref, wk_ref, wv_ref, g1_ref, g2_ref,
                   w0_ref, w1_ref, w2_ref, a0_ref, a1_ref, a2_ref, kkp_ref, kap_ref, rkp_ref, ones_ref,
                   r_ref, k_ref, v_ref, kk_ref, g_ref, lwf_ref, lwb_ref, af_ref, ab_ref, bn_ref):
    j = pl.program_id(1)
    g, sh, sc = ng_ref[0:1, :], mod_ref[0:1, :], mod_ref[1:2, :]
    h = _norm_mod(x_ref[...], g, sh, sc)
    prev_row = jnp.where(j == 0, 0.0, _norm_mod(xp_ref[...], g, sh, sc)[7:8, :])
    next_row = jnp.where(j == pl.num_programs(1) - 1, 0.0, _norm_mod(xn_ref[...], g, sh, sc)[0:1, :])
    h_prev, h_next = _shifted(h, prev_row, next_row)
    xx = 0.5 * (h_prev + h_next) - h

    def mixed(m):
        return (h + xx * mix_ref[m:m + 1, :]).astype(bf16)

    r = jnp.dot(mixed(0), wr_ref[...], preferred_element_type=f32)
    xw = mixed(1)
    k = jnp.dot(mixed(2), wk_ref[...], preferred_element_type=f32)
    v = jnp.dot(mixed(3), wv_ref[...], preferred_element_type=f32)
    xa = mixed(4)
    r_ref[...] = r
    k_ref[...] = k
    v_ref[...] = v
    gate = _sigmoid(jnp.dot(mixed(5), g1_ref[...], preferred_element_type=f32))
    g_ref[...] = jnp.dot(gate.astype(bf16), g2_ref[...], preferred_element_type=f32)
    ones = ones_ref[...]
    kk = k * kkp_ref[...]
    kk_ref[...] = kk * lax.rsqrt(jnp.maximum(_head_sum(kk * kk, ones), 1e-24))
    a_sum = None
    for dd, (lw_ref, a_ref) in enumerate(((lwf_ref, af_ref), (lwb_ref, ab_ref))):
        lora = jnp.tanh(jnp.dot(xw, w1_ref[dd], preferred_element_type=f32))
        wl = w0_ref[dd:dd + 1, :] + jnp.dot(lora.astype(bf16), w2_ref[dd], preferred_element_type=f32)
        lw_ref[...] = -math.exp(-0.5) * _sigmoid(wl)
        al = jnp.dot(xa, a1_ref[dd], preferred_element_type=f32)
        a = _sigmoid(a0_ref[dd:dd + 1, :] + jnp.dot(al.astype(bf16), a2_ref[dd], preferred_element_type=f32))
        a_ref[...] = a
        a_sum = a if a_sum is None else a_sum + a
    kd_sum = k * (2.0 + (a_sum - 2.0) * kap_ref[...])
    bn_ref[...] = _head_sum(r * kd_sum * rkp_ref[...], ones) * v


def _stack_heads(x, first_head):
    return jnp.concatenate([jnp.where(first_head, x, 0.0), jnp.where(first_head, 0.0, x)], axis=0).astype(bf16)


def _dot_nt(a, b):
    return lax.dot_general(a, b, (((1,), (1,)), ((), ())), preferred_element_type=f32)


def _dot_tn(a, b):
    return lax.dot_general(a, b, (((0,), (0,)), ((), ())), preferred_element_type=f32)


def _rw_scan_kernel(r_ref, k_ref, v_ref, kk_ref, lw_ref, a_ref, kap_ref, s0_ref, o_ref, s_ref, *, reverse):
    c = pl.program_id(1)
    C, W, G = SCAN_CHUNK, PAIR_W, SCAN_GROUP

    @pl.when(c == 0)
    def _():
        s_ref[...] = s0_ref[...]

    row = lax.broadcasted_iota(jnp.int32, (C, W), 0)
    lane = lax.broadcasted_iota(jnp.int32, (C, W), 1)
    col = lane & (C - 1)
    first_head = lane < RW_HEAD
    before = (col > row) if reverse else (col < row)
    before_eq = before | (col == row)
    eye = (col == row).astype(f32)
    row_w = lax.broadcasted_iota(jnp.int32, (W, W), 0)
    lane_w = lax.broadcasted_iota(jnp.int32, (W, W), 1)
    same_head = (row_w >= RW_HEAD) == (lane_w >= RW_HEAD)
    diag_w = row_w == lane_w
    merge_masks = []
    s = 1
    while s < C:
        sh = s.bit_length() - 1
        same_pair = (row >> (sh + 1)) == (col >> (sh + 1))
        later, earlier = ((row >> sh) & 1, (col >> sh) & 1)
        merge_masks.append(same_pair & (((later == 0) & (earlier == 1)) if reverse
                                        else ((later == 1) & (earlier == 0))))
        s *= 2

    row_c = lax.broadcasted_iota(jnp.int32, (G * C, G * C), 0)
    col_c = lax.broadcasted_iota(jnp.int32, (G * C, G * C), 1)
    chunk_bits = C.bit_length() - 1
    same_chunk = (row_c >> chunk_bits) == (col_c >> chunk_bits)
    tri = (same_chunk & ((col_c >= row_c) if reverse else (col_c <= row_c))).astype(bf16)
    lw_all = lw_ref[...]
    cum_all = _dot_exact_rhs_left(tri, lw_all)
    n_pairs = D_MODEL // W

    def pairs(x):
        return [x[g * C:(g + 1) * C, p * W:(p + 1) * W] for g in range(G) for p in range(n_pairs)]

    def per_chunk_rows(rows):
        return [rows[g][:, p * W:(p + 1) * W] for g in range(G) for p in range(n_pairs)]

    def each(fn, *lists):
        return [fn(*args) for args in zip(*lists)]

    def stack(x):
        return _stack_heads(x, first_head)

    def mm(x, y):
        return jnp.dot(x.astype(bf16), stack(y), preferred_element_type=f32)

    def mm2(x, y0, y1):
        return jnp.dot(x.astype(bf16), jnp.concatenate([stack(y0), stack(y1)], axis=1),
                       preferred_element_type=f32)

    r, k, v, kk, a = (pairs(ref[...]) for ref in (r_ref, k_ref, v_ref, kk_ref, a_ref))
    lw, cum = pairs(lw_all), pairs(cum_all)
    tot = per_chunk_rows([jnp.sum(lw_all[g * C:(g + 1) * C, :], axis=0, keepdims=True) for g in range(G)])
    ka = per_chunk_rows([kap_ref[...]] * G)
    kd = each(lambda k_, a_, ka_: k_ * (1.0 + (a_ - 1.0) * ka_), k, a, ka)
    b = each(lambda kk_, a_: kk_ * a_, kk, a)
    e_in = each(jnp.exp, cum)
    e_out = each(lambda c_: jnp.exp(-c_), cum)
    e_end = each(lambda t_, c_: jnp.exp(t_ - c_), tot, cum)
    kap = each(lambda kk_, c_, l_: kk_ * jnp.exp(c_ - l_), kk, cum, lw)
    mul = lambda x_, y_: x_ * y_
    kt, bt, rt, kh, bh = each(mul, kd, e_out), each(mul, b, e_out), each(mul, r, e_in), each(mul, kd, e_end), \
        each(mul, b, e_end)

    lhs = each(lambda x_, y_: jnp.concatenate([x_, y_], axis=0).astype(bf16), kap, rt)
    gram = each(lambda l_, y0_, y1_: _dot_nt(l_, jnp.concatenate([stack(y0_), stack(y1_)], axis=0)), lhs, kt, bt)
    m_kk = each(lambda g_: jnp.where(before, g_[:C, :W], 0.0), gram)
    n_kb = each(lambda g_: jnp.where(before, g_[:C, W:], 0.0), gram)
    a_qk = each(lambda g_: jnp.where(before_eq, g_[C:, :W], 0.0), gram)
    a_qb = each(lambda g_: jnp.where(before_eq, g_[C:, W:], 0.0), gram)

    tinv = each(lambda n_: eye - jnp.where(merge_masks[0], n_, 0.0), n_kb)
    for off in merge_masks[1:]:
        tn = each(lambda t_, n_: mm(t_, jnp.where(off, n_, 0.0)), tinv, n_kb)
        tinv = each(lambda t_, tn_: t_ - mm(tn_, t_), tinv, tn)

    mva = each(lambda m_, a_, v_: mm(jnp.concatenate([m_, a_], axis=0), v_), m_kk, a_qk, v)
    mv, av = each(lambda t_: t_[:C, :], mva), each(lambda t_: t_[C:, :], mva)
    tk = each(mm2, tinv, kap, mv)
    kt_c, w1 = each(lambda t_: t_[:, :W], tk), each(lambda t_: t_[:, W:], tk)
    qa = each(mm2, a_qb, kt_c, w1)
    q_eff = each(lambda rt_, qa_: rt_ - qa_[:, :W], rt, qa)
    o_intra = each(lambda av_, qa_: av_ - qa_[:, W:], av, qa)
    kb = each(lambda x_, b_: _dot_tn(x_.astype(bf16), b_.astype(bf16)), kt_c, bh)
    phi = each(lambda t_, kb_: jnp.where(same_head, jnp.where(diag_w, jnp.exp(t_), 0.0) - kb_, 0.0), tot, kb)
    ds_full = each(lambda v_, w_, kh_, bh_: _dot_tn(jnp.concatenate([v_, -w_], axis=0).astype(bf16),
                                                    jnp.concatenate([kh_, bh_], axis=0).astype(bf16)),
                   v, w1, kh, bh)
    d_s = each(lambda d_: jnp.where(first_head, d_[:RW_HEAD, :], d_[RW_HEAD:, :]), ds_full)
    state = [s_ref[:, p * W:(p + 1) * W] for p in range(n_pairs)]
    for g in (range(G - 1, -1, -1) if reverse else range(G)):
        sel = slice(g * n_pairs, (g + 1) * n_pairs)
        o = each(lambda q_, s_, oi_: _dot_nt(q_.astype(bf16), stack(s_)) + oi_, q_eff[sel], state, o_intra[sel])
        state = each(lambda s_, ph_, ds_: jnp.dot(s_.astype(bf16), ph_.astype(bf16),
                                                  preferred_element_type=f32) + ds_, state, phi[sel], d_s[sel])
        o_ref[g * C:(g + 1) * C, :] = jnp.concatenate(o, axis=1)
    s_ref[...] = jnp.concatenate(state, axis=1)


def _rw_post_kernel(of_ref, ob_ref, bn_ref, g_ref, x_ref, mod_ref, ng_ref, lw_ref, lb_ref, ones_ref,
                    wo_ref, o_ref):
    o = of_ref[...] + ob_ref[...]
    ones = ones_ref[...]
    inv_n = 1.0 / RW_HEAD
    dev = o - _head_sum(o, ones) * inv_n
    var = _head_sum(dev * dev, ones) * inv_n
    on = dev * lax.rsqrt(var + RW_LN_EPS) * lw_ref[...] + lb_ref[...] + bn_ref[...]
    y = jnp.dot((on * g_ref[...]).astype(bf16), wo_ref[...], preferred_element_type=f32)
    o_ref[...] = _residual(x_ref[...], y, ng_ref[1:2, :], mod_ref[2:3, :])


@functools.lru_cache(maxsize=None)
def _pair_ones():
    idx = np.arange(PAIR_W) // RW_HEAD
    return np.asarray(idx[:, None] == idx[None, :], dtype=bf16)


def _rwkv_layer(x, trunk, layer, mod, norm_g, p, s0_fwd, s0_bwd):
    nb, seq = trunk.batch, trunk.seq
    tl = _tile_rows(trunk, 256)
    x8 = x.reshape(nb, seq // 8, 8, D_MODEL)
    prev_spec, next_spec = _halo_specs(trunk, tl)
    lora_w, lora_a, lora_g = p["w1"].shape[-1], p["a1"].shape[-1], p["g1"].shape[-1]
    ones = jnp.asarray(_pair_ones())
    outs = pl.pallas_call(
        _rw_pre_kernel,
        grid=(nb, seq // tl),
        in_specs=[_x_spec(tl), prev_spec, next_spec, _mod_spec(trunk, layer), _ng_spec(layer),
                  _resident((6, D_MODEL)),
                  _resident((D_MODEL, D_MODEL)), _resident((D_MODEL, D_MODEL)), _resident((D_MODEL, D_MODEL)),
                  _resident((D_MODEL, lora_g)), _resident((lora_g, D_MODEL)),
                  _resident((2, D_MODEL)), _resident((2, D_MODEL, lora_w)), _resident((2, lora_w, D_MODEL)),
                  _resident((2, D_MODEL)), _resident((2, D_MODEL, lora_a)), _resident((2, lora_a, D_MODEL)),
                  _resident((1, D_MODEL)), _resident((1, D_MODEL)), _resident((1, D_MODEL)),
                  _resident((PAIR_W, PAIR_W))],
        out_specs=[_x_spec(tl)] * 10,
        out_shape=[jax.ShapeDtypeStruct(x.shape, f32)] * 10,
        compiler_params=_params("parallel", "parallel"),
    )(x, x8, x8, mod, norm_g, p["mix"], p["wr"], p["wk"], p["wv"], p["g1"], p["g2"],
      p["w0"], p["w1"], p["w2"], p["a0"], p["a1"], p["a2"], p["kk"], p["ka"], p["rk"], ones)
    r, k, v, kk, gate, lw_f, lw_b, a_f, a_b, bonus = outs

    def state_in(s):
        return s.transpose(0, 2, 1, 3).reshape(nb, RW_HEAD, D_MODEL)

    def state_out(s):
        return s.reshape(nb, RW_HEAD, RW_HEADS, RW_HEAD).transpose(0, 2, 1, 3)

    rows = SCAN_CHUNK * SCAN_GROUP
    n_chunks = seq // rows
    state_spec = pl.BlockSpec((None, RW_HEAD, D_MODEL), lambda b, c: (b, 0, 0))
    results = []
    for reverse, lw, a, s0 in ((False, lw_f, a_f, s0_fwd), (True, lw_b, a_b, s0_bwd)):
        if reverse:
            chunk_spec = pl.BlockSpec((None, rows, D_MODEL), lambda b, c: (b, n_chunks - 1 - c, 0))
        else:
            chunk_spec = pl.BlockSpec((None, rows, D_MODEL), lambda b, c: (b, c, 0))
        results.append(pl.pallas_call(
            functools.partial(_rw_scan_kernel, reverse=reverse),
            grid=(nb, n_chunks),
            in_specs=[chunk_spec] * 6 + [_resident((1, D_MODEL)), state_spec],
            out_specs=[chunk_spec, state_spec],
            out_shape=[jax.ShapeDtypeStruct(x.shape, f32),
                       jax.ShapeDtypeStruct((nb, RW_HEAD, D_MODEL), f32)],
            compiler_params=_params("parallel", "arbitrary"),
        )(r, k, v, kk, lw, a, p["ka"], state_in(s0)))
    (o_f, s_f), (o_b, s_b) = results

    y = pl.pallas_call(
        _rw_post_kernel,
        grid=(nb, seq // tl),
        in_specs=[_x_spec(tl)] * 5 + [_mod_spec(trunk, layer), _ng_spec(layer),
                                      _resident((1, D_MODEL)), _resident((1, D_MODEL)),
                                      _resident((PAIR_W, PAIR_W)), _resident((D_MODEL, D_MODEL))],
        out_specs=_x_spec(tl),
        out_shape=jax.ShapeDtypeStruct(x.shape, f32),
        compiler_params=_params("parallel", "parallel"),
    )(o_f, o_b, bonus, gate, x, mod, norm_g, p["lnx_w"], p["lnx_b"], ones, p["wo"])
    return y, state_out(s_f), state_out(s_b)


def kernel(x_prompt, x_sample, state_wkv_fwd, state_wkv_bwd, c, c_ctx, ada_w, ada_b, norm_g, ffn_w1, ffn_w3, ffn_w2, fn_w_out, fn_b_out, hy_w_in, hy_b_in, hy_conv_w, hy_conv_b, hy_f_w1, hy_f_b1, hy_f_freq, hy_f_w2, hy_f_b2, hy_f_w3, hy_d, hy_w_out, hy_b_out, rw_mix, rw_wr, rw_wk, rw_wv, rw_wo, rw_w0, rw_w1, rw_w2, rw_a0, rw_a1, rw_a2, rw_g1, rw_g2, rw_kk, rw_ka, rw_rk, rw_lnx_w, rw_lnx_b):
    n_ctx, n_dec = x_prompt.shape[0], x_sample.shape[0]
    assert 1 + n_dec <= MOD_ROWS
    trunks = (Trunk(n_ctx, x_prompt.shape[1], 0, False), Trunk(n_dec, x_sample.shape[1], 1, True))
    cond = jnp.zeros((MOD_ROWS, D_MODEL), f32).at[0].set(c_ctx).at[1:1 + n_dec].set(c)
    mod = _adaln(cond, ada_w, ada_b)

    n_rwkv = rw_w0.shape[0]
    zero_state = jnp.zeros((n_ctx, n_rwkv, RW_HEADS, RW_HEAD, RW_HEAD), f32)
    states = ((zero_state, zero_state), (state_wkv_fwd, state_wkv_bwd))
    xs = [x_prompt, x_sample]
    ffn_bf16 = [w.astype(bf16) for w in (ffn_w1, ffn_w3, ffn_w2)]
    new_fwd, new_bwd = [], []
    for i in range(DEPTH):
        kind, j = i % N_MIXERS, i // N_MIXERS
        if kind == 0:
            w_out, b_out = fn_w_out[j].astype(bf16), fn_b_out[j][None]
            xs = [_fourier_layer(x, t, i, mod, norm_g, w_out, b_out) for x, t in zip(xs, trunks)]
        elif kind == 1:
            w1_pad = jnp.zeros((HY_EMB_PAD, HY_FILT), f32).at[:HY_EMB].set(hy_f_w1[j])
            p = dict(w_in=hy_w_in[j].astype(bf16), b_in=hy_b_in[j][None], conv_w=hy_conv_w[j],
                     conv_b=hy_conv_b[j][None], f_w1=w1_pad, f_b1=hy_f_b1[j][None], f_freq=hy_f_freq[j],
                     f_w2=hy_f_w2[j], f_b2=hy_f_b2[j][None], f_w3=hy_f_w3[j], d=hy_d[j][None],
                     w_out=hy_w_out[j].astype(bf16), b_out=hy_b_out[j][None])
            xs = [_hyena_layer(x, t, i, mod, norm_g, p) for x, t in zip(xs, trunks)]
        else:
            per_head = lambda t: t.reshape(1, D_MODEL)
            p = dict(mix=rw_mix[j], wr=rw_wr[j].astype(bf16), wk=rw_wk[j].astype(bf16),
                     wv=rw_wv[j].astype(bf16), wo=rw_wo[j].astype(bf16), w0=rw_w0[j],
                     w1=rw_w1[j].astype(bf16), w2=rw_w2[j].astype(bf16), a0=rw_a0[j],
                     a1=rw_a1[j].astype(bf16), a2=rw_a2[j].astype(bf16), g1=rw_g1[j].astype(bf16),
                     g2=rw_g2[j].astype(bf16), kk=per_head(rw_kk[j]), ka=per_head(rw_ka[j]),
                     rk=per_head(rw_rk[j]), lnx_w=rw_lnx_w[j][None], lnx_b=rw_lnx_b[j][None])
            outs = [_rwkv_layer(x, t, i, mod, norm_g, p, sf[:, j], sb[:, j])
                    for x, t, (sf, sb) in zip(xs, trunks, states)]
            xs = [o[0] for o in outs]
            new_fwd.append(outs[0][1])
            new_bwd.append(outs[0][2])
        xs = [_ffn(x, t, i, mod, norm_g, ffn_bf16[0], ffn_bf16[1], ffn_bf16[2]) for x, t in zip(xs, trunks)]
    return xs[0], xs[1], jnp.stack(new_fwd, axis=1), jnp.stack(new_bwd, axis=1)
```

```python
import functools
import math
from typing import NamedTuple

import numpy as np
import jax
import jax.numpy as jnp
from jax import lax
from jax.experimental import pallas as pl
from jax.experimental.pallas import tpu as pltpu

f32 = jnp.float32
bf16 = jnp.bfloat16

D_MODEL = 1024
DEPTH = 4
N_MIXERS = 3
D_FF = 2816
NORM_EPS = 1e-6
FN_GROUP_W = 256
FN_HALO = 16
HY_EMB = 33
HY_EMB_PAD = 128
HY_BANDS = 16
HY_FILT = 64
HY_TARGET = 1e-2
HY_FAST = 0.3
HY_SLOW = 1.5
RW_HEAD = 64
RW_HEADS = 16
RW_LN_EPS = 64e-5
SCAN_CHUNK = 64
SCAN_GROUP = 4
PAIR_W = 2 * RW_HEAD
assert SCAN_CHUNK == RW_HEAD, "the scan keeps (chunk, chunk) and (chunk, head) tiles in one lane layout"
MOD_ROWS = 8
VMEM_LIMIT = 56 * 1024 * 1024


class Trunk(NamedTuple):
    batch: int
    seq: int
    mod_base: int
    per_batch_mod: bool


def _params(*sem):
    return pltpu.CompilerParams(dimension_semantics=sem, vmem_limit_bytes=VMEM_LIMIT)


def _resident(shape):
    nd = len(shape)
    return pl.BlockSpec(shape, lambda *_: (0,) * nd, pipeline_mode=pl.Buffered(1))


def _layer_resident(shape, layer):
    nd = len(shape)
    return pl.BlockSpec((None,) + tuple(shape), lambda *_: (layer,) + (0,) * nd, pipeline_mode=pl.Buffered(1))


def _split(x):
    hi = x.astype(bf16)
    lo = (x - hi.astype(f32)).astype(bf16)
    return hi, lo


def _dot3(a, b):
    ah, al = _split(a)
    bh, bl = _split(b)
    d = functools.partial(jnp.dot, preferred_element_type=f32)
    return d(ah, bh) + (d(ah, bl) + d(al, bh))


def _dot_exact_rhs_left(a_exact, b):
    b0, b1 = _split(b)
    d = functools.partial(jnp.dot, preferred_element_type=f32)
    return d(a_exact, b0) + d(a_exact, b1)


def _sigmoid(x):
    return 0.5 * jnp.tanh(0.5 * x) + 0.5


def _norm_mod(x, g, shift, scale):
    ms = jnp.mean(x * x, axis=-1, keepdims=True)
    return (x * lax.rsqrt(ms + NORM_EPS)) * (g * (1.0 + scale)) + shift


def _residual(x, y, g, gate):
    ms = jnp.mean(y * y, axis=-1, keepdims=True)
    return x + gate * (y * lax.rsqrt(ms + NORM_EPS) * g)


def _shifted(h, prev_row, next_row):
    rows = h.shape[0]
    ridx = lax.broadcasted_iota(jnp.int32, h.shape, 0)
    h_prev = jnp.where(ridx == 0, prev_row, pltpu.roll(h, 1, 0))
    h_next = jnp.where(ridx == rows - 1, next_row, pltpu.roll(h, rows - 1, 0))
    return h_prev, h_next


def _tile_rows(trunk, cap):
    return min(trunk.seq, cap)


def _x_spec(tl):
    return pl.BlockSpec((None, tl, D_MODEL), lambda b, j: (b, j, 0))


def _halo_specs(trunk, tl):
    g = tl // 8
    last = trunk.seq // 8 - 1
    prev = pl.BlockSpec((None, None, 8, D_MODEL), lambda b, j: (b, jnp.maximum(j * g - 1, 0), 0, 0))
    nxt = pl.BlockSpec((None, None, 8, D_MODEL), lambda b, j: (b, jnp.minimum((j + 1) * g, last), 0, 0))
    return prev, nxt


def _mod_spec(trunk, layer):
    if trunk.per_batch_mod:
        return pl.BlockSpec((None, None, 6, D_MODEL), lambda b, *_: (layer, trunk.mod_base + b, 0, 0))
    return pl.BlockSpec((None, None, 6, D_MODEL), lambda b, *_: (layer, trunk.mod_base, 0, 0))


def _ng_spec(layer):
    return pl.BlockSpec((None, 4, D_MODEL), lambda *_: (layer, 0, 0))


def _mod_kernel(c_ref, w_ref, b_ref, o_ref):
    c = c_ref[...]
    o_ref[...] = _dot3(c * _sigmoid(c), w_ref[...]) + b_ref[...]


def _adaln(cond, ada_w, ada_b):
    tn = 1536
    out = pl.pallas_call(
        _mod_kernel,
        grid=(DEPTH, 6 * D_MODEL // tn),
        in_specs=[pl.BlockSpec((MOD_ROWS, D_MODEL), lambda l, j: (0, 0)),
                  pl.BlockSpec((None, D_MODEL, tn), lambda l, j: (l, 0, j)),
                  pl.BlockSpec((None, 1, tn), lambda l, j: (l, 0, j))],
        out_specs=pl.BlockSpec((None, MOD_ROWS, tn), lambda l, j: (l, 0, j)),
        out_shape=jax.ShapeDtypeStruct((DEPTH, MOD_ROWS, 6 * D_MODEL), f32),
        compiler_params=_params("arbitrary", "arbitrary"),
    )(cond, ada_w, ada_b.reshape(DEPTH, 1, 6 * D_MODEL))
    return out.reshape(DEPTH, MOD_ROWS, 6, D_MODEL)


def _ffn_kernel(x_ref, mod_ref, ng_ref, w1_ref, w3_ref, w2_ref, o_ref):
    tl = x_ref.shape[0]
    groups = 2 if tl >= 512 else 1
    tg = tl // groups
    for g in range(groups):
        rs = slice(g * tg, (g + 1) * tg)
        x = x_ref[rs, :]
        h = _norm_mod(x, ng_ref[2:3, :], mod_ref[3:4, :], mod_ref[4:5, :]).astype(bf16)
        a = jnp.dot(h, w1_ref[...], preferred_element_type=f32)
        b = jnp.dot(h, w3_ref[...], preferred_element_type=f32)
        gated = (a * _sigmoid(a) * b).astype(bf16)
        y = jnp.dot(gated, w2_ref[...], preferred_element_type=f32)
        o_ref[rs, :] = _residual(x, y, ng_ref[3:4, :], mod_ref[5:6, :])


def _merge_sequences(x, trunk):
    if trunk.per_batch_mod:
        return x, trunk
    return x.reshape(1, -1, D_MODEL), Trunk(1, trunk.batch * trunk.seq, trunk.mod_base, False)


def _ffn(x, trunk, layer, mod, norm_g, w1, w3, w2):
    shape = x.shape
    x, trunk = _merge_sequences(x, trunk)
    tl = _tile_rows(trunk, 512)
    return pl.pallas_call(
        _ffn_kernel,
        grid=(trunk.batch, trunk.seq // tl),
        in_specs=[_x_spec(tl), _mod_spec(trunk, layer), _ng_spec(layer),
                  _layer_resident((D_MODEL, D_FF), layer), _layer_resident((D_MODEL, D_FF), layer),
                  _layer_resident((D_FF, D_MODEL), layer)],
        out_specs=_x_spec(tl),
        out_shape=jax.ShapeDtypeStruct(x.shape, f32),
        compiler_params=_params("parallel", "parallel"),
    )(x, mod, norm_g, w1, w3, w2).reshape(shape)


@functools.lru_cache(maxsize=None)
def _channel_dft():
    w = FN_GROUP_W
    idx = np.arange(w)
    ang = 2.0 * np.pi * ((idx[:, None] * idx[None, :]) % w) / w
    return (np.concatenate([np.cos(ang), np.sin(ang)], axis=1) / math.sqrt(w)).astype(np.float32)


def _table_rows(cs_ref, tile, rows):
    return cs_ref[pl.ds(pl.multiple_of(tile * rows, rows), rows), :]


def _mxu_table(table):
    return jnp.asarray(table).astype(bf16)


def _fn_a_kernel(x_ref, mod_ref, ng_ref, w_ref, a_ref, b_ref):
    h = _norm_mod(x_ref[...], ng_ref[0:1, :], mod_ref[0:1, :], mod_ref[1:2, :]).astype(bf16)
    w = FN_GROUP_W
    for g in range(D_MODEL // w):
        ab = jnp.dot(h[:, g * w:(g + 1) * w], w_ref[...], preferred_element_type=f32).astype(bf16)
        a_ref[:, g * w:(g + 1) * w] = ab[:, :w]
        b_ref[:, g * w:(g + 1) * w] = ab[:, w:]


@functools.lru_cache(maxsize=None)
def _fourier_half_table(seq):
    k = np.arange(seq // 2 + FN_HALO)
    t = np.arange(seq)
    ang = 2.0 * np.pi * ((k[:, None] * t[None, :]) % seq) / seq
    return (np.concatenate([np.cos(ang), np.sin(ang)], axis=1) / math.sqrt(seq)).astype(np.float32)


@functools.lru_cache(maxsize=None)
def _row_reversal(tm):
    out = np.zeros((tm, tm + FN_HALO), np.float32)
    out[np.arange(tm), tm - np.arange(tm)] = 1.0
    return np.asarray(out, dtype=bf16)


def _fn_b_kernel(cs_ref, rev_ref, a_ref, b_ref, x_ref, mod_ref, ng_ref, wo_ref, bo_ref, o_ref, mirror_ref, *,
                 seq):
    s = pl.program_id(1)
    half_tiles = pl.num_programs(1) // 2
    n_seq, tm = x_ref.shape[0], x_ref.shape[1]
    src_rows = tm + FN_HALO

    def finish(f_bf16):
        f = jnp.concatenate(f_bf16, axis=0)
        y = jnp.dot(f, wo_ref[...], preferred_element_type=f32) + bo_ref[...]
        x = x_ref[...].reshape(n_seq * tm, D_MODEL)
        o_ref[...] = _residual(x, y, ng_ref[1:2, :], mod_ref[2:3, :]).reshape(n_seq, tm, D_MODEL)

    @pl.when(s < half_tiles)
    def _():
        first = pl.multiple_of(s * tm, tm)
        rows = cs_ref[pl.ds(first, src_rows), :]
        out = []
        for i in range(n_seq):
            p = jnp.dot(rows[:, :seq], a_ref[i], preferred_element_type=f32)
            q = jnp.dot(rows[:, seq:], b_ref[i], preferred_element_type=f32)
            mirror_ref[i, pl.ds(first, src_rows), :] = (p + q).astype(bf16)
            out.append((p - q)[:tm, :].astype(bf16))
        finish(out)

    @pl.when(s >= half_tiles)
    def _():
        first = pl.multiple_of((2 * half_tiles - 1 - s) * tm, tm)
        finish([jnp.dot(rev_ref[...], mirror_ref[i, pl.ds(first, src_rows), :],
                        preferred_element_type=f32).astype(bf16) for i in range(n_seq)])


def _fourier_layer(x, trunk, layer, mod, norm_g, w_out, b_out):
    nb, seq = trunk.batch, trunk.seq
    xm, merged = _merge_sequences(x, trunk)
    tl = _tile_rows(merged, 512)
    a, b = pl.pallas_call(
        _fn_a_kernel,
        grid=(merged.batch, merged.seq // tl),
        in_specs=[_x_spec(tl), _mod_spec(merged, layer), _ng_spec(layer),
                  _resident((FN_GROUP_W, 2 * FN_GROUP_W))],
        out_specs=[_x_spec(tl)] * 2,
        out_shape=[jax.ShapeDtypeStruct(xm.shape, bf16)] * 2,
        compiler_params=_params("parallel", "parallel"),
    )(xm, mod, norm_g, _mxu_table(_channel_dft()))

    tm = min(seq // 2, 512)
    half_rows = seq // 2 + FN_HALO
    n_seq = 1 if trunk.per_batch_mod else math.gcd(nb, max(1, 512 // tm))
    tile = pl.BlockSpec((n_seq, tm, D_MODEL), lambda b, i: (b, i, 0))
    whole = pl.BlockSpec((n_seq, seq, D_MODEL), lambda b, i: (b, 0, 0))
    return pl.pallas_call(
        functools.partial(_fn_b_kernel, seq=seq),
        grid=(nb // n_seq, seq // tm),
        in_specs=[_resident((half_rows, 2 * seq)), _resident((tm, tm + FN_HALO)), whole, whole,
                  tile, _mod_spec(trunk, layer), _ng_spec(layer),
                  _resident((D_MODEL, D_MODEL)), _resident((1, D_MODEL))],
        out_specs=tile,
        out_shape=jax.ShapeDtypeStruct(x.shape, f32),
        scratch_shapes=[pltpu.VMEM((n_seq, half_rows, D_MODEL), bf16)],
        compiler_params=_params("parallel", "arbitrary"),
    )(_mxu_table(_fourier_half_table(seq)), jnp.asarray(_row_reversal(tm)), a.reshape(x.shape), b.reshape(x.shape),
      x, mod, norm_g, w_out, b_out)


@functools.lru_cache(maxsize=None)
def _hyena_features(seq):
    t = np.linspace(0.0, 1.0, seq)[:, None]
    ang = 2.0 * np.pi * np.arange(seq)[:, None] / seq
    bands = np.linspace(1e-4, HY_BANDS - 1, HY_BANDS)[None]
    z = np.concatenate([t, np.cos(bands * ang), -np.sin(bands * ang)], axis=-1)
    out = np.zeros((seq, HY_EMB_PAD), np.float32)
    out[:, :HY_EMB] = np.concatenate([z[0::2], z[1::2]], axis=0)
    return out


@functools.lru_cache(maxsize=None)
def _hyena_decay_rates():
    d = np.linspace(math.log(HY_TARGET) / HY_FAST, math.log(HY_TARGET) / HY_SLOW, D_MODEL)
    return np.abs(d)[None].astype(np.float32)


@functools.lru_cache(maxsize=None)
def _hyena_tables(seq):
    half = seq // 2
    p = np.arange(half)[:, None]
    m = np.arange(half)[None, :]
    ang_e = np.pi * ((p * 2 * m) % (2 * seq)) / seq
    ang_o = np.pi * ((p * (2 * m + 1)) % (2 * seq)) / seq
    fwd = np.concatenate([np.cos(ang_e), np.cos(ang_o), np.sin(ang_e), np.sin(ang_o)], axis=1)
    inv = np.concatenate([np.cos(ang_e).T, -np.sin(ang_e).T, np.cos(ang_o).T, -np.sin(ang_o).T], axis=1)
    return fwd.astype(np.float32), inv.astype(np.float32)


@functools.lru_cache(maxsize=None)
def _parity_permutation(n):
    out = np.zeros((n, n), np.float32)
    half = n // 2
    out[np.arange(half), 2 * np.arange(half)] = 1.0
    out[half + np.arange(half), 2 * np.arange(half) + 1] = 1.0
    return out


def _hy_in_kernel(x_ref, xp_ref, xn_ref, mod_ref, ng_ref, w_ref, b_ref, cw_ref, cb_ref, sel_ref,
                  x0_ref, z_ref, zb_ref):
    j = pl.program_id(1)
    g, sh, sc = ng_ref[0:1, :], mod_ref[0:1, :], mod_ref[1:2, :]
    tl = x_ref.shape[0]
    rows = jnp.concatenate([x_ref[...], xp_ref[...], xn_ref[...]], axis=0)
    u_all = jnp.dot(_norm_mod(rows, g, sh, sc).astype(bf16), w_ref[...], preferred_element_type=f32) + b_ref[...]
    u = u_all[:tl, :]
    prev_row = jnp.where(j == 0, 0.0, u_all[tl + 7:tl + 8, :])
    next_row = jnp.where(j == pl.num_programs(1) - 1, 0.0, u_all[tl + 8:tl + 9, :])
    u_prev, u_next = _shifted(u, prev_row, next_row)
    uc = u_prev * cw_ref[0:1, :] + u * cw_ref[1:2, :] + u_next * cw_ref[2:3, :] + cb_ref[...]
    x0_ref[...] = uc[:, :D_MODEL]
    z = uc[:, 2 * D_MODEL:] * uc[:, D_MODEL:2 * D_MODEL]
    z_ref[...] = z
    split = jnp.dot(sel_ref[...], z.astype(bf16), preferred_element_type=f32).astype(bf16)
    zb_ref[0] = split[:tl // 2, :]
    zb_ref[1] = split[tl // 2:, :]


def _alternating_sign(first, rows):
    t = first + lax.broadcasted_iota(jnp.int32, (rows, 1), 0)
    return (1 - 2 * (t & 1)).astype(f32)


def _parity_dft(rows, even, odd, half):
    d = functools.partial(jnp.dot, preferred_element_type=f32)
    return (d(rows[:, :half], even), d(rows[:, half:2 * half], odd),
            d(rows[:, 2 * half:3 * half], even), d(rows[:, 3 * half:], odd))


def _hy_filter_kernel(feat_ref, w1_ref, b1_ref, fq_ref, w2_ref, b2_ref, w3f_ref, w3b_ref, rate_ref,
                      tab_ref, o_ref, hdn_ref, *, seq):
    half = seq // 2

    @pl.when(pl.program_id(0) == 0)
    def _():
        hdn = jnp.sin(fq_ref[0:1, :] * (_dot3(feat_ref[...], w1_ref[...]) + b1_ref[...]))
        hdn_ref[...] = jnp.sin(fq_ref[1:2, :] * (_dot3(hdn, w2_ref[...]) + b2_ref[...]))

    hdn = hdn_ref[...]
    rows = lax.broadcasted_iota(jnp.int32, (seq, 1), 0)
    lag = jnp.where(rows < half, 2 * rows, 2 * (rows - half) + 1)
    win = jnp.exp(-(lag.astype(f32) * (1.0 / (seq - 1))) * rate_ref[...])
    k_fwd = _dot3(hdn, w3f_ref[...]) * win
    k_bwd = _dot3(hdn, w3b_ref[...]) * win
    both = k_fwd + k_bwd
    diff = k_fwd - k_bwd
    ec, oc, _, _ = _parity_dft(tab_ref[...], both[:half].astype(bf16), both[half:].astype(bf16), half)
    _, _, es, os_ = _parity_dft(tab_ref[...], diff[:half].astype(bf16), diff[half:].astype(bf16), half)
    sign = _alternating_sign(0, half)
    mid_re = jnp.sum(both[:half] * sign, axis=0, keepdims=True)
    mid_im = -jnp.sum(diff[half:] * sign, axis=0, keepdims=True)
    first = lax.broadcasted_iota(jnp.int32, (half, 1), 0) == 0
    wgt = jnp.where(first, 0.5 / seq, 1.0 / seq)
    o_ref[0] = (ec + oc) * wgt
    o_ref[1] = jnp.where(first, mid_re * (1.0 / seq), -(es + os_) * wgt)
    o_ref[2] = (ec - oc) * wgt
    o_ref[3] = jnp.where(first, mid_im * (1.0 / seq), (es - os_) * wgt)


def _hy_fwd_kernel(tab_ref, z_ref, ks_ref, o_ref, mid_ref, *, seq):
    i = pl.program_id(1)
    half = seq // 2
    tf = o_ref.shape[1]
    z_even, z_odd = z_ref[0], z_ref[1]
    ec, oc, es, os_ = _parity_dft(_table_rows(tab_ref, i, tf), z_even, z_odd, half)
    zre_lo, zim_lo, zre_hi, zim_hi = ec + oc, -(es + os_), ec - oc, es - os_
    kre_lo, kim_lo, kre_hi, kim_hi = ks_ref[0], ks_ref[1], ks_ref[2], ks_ref[3]
    yre_lo = zre_lo * kre_lo - zim_lo * kim_lo
    yim_lo = zre_lo * kim_lo + zim_lo * kre_lo
    yre_hi = zre_hi * kre_hi - zim_hi * kim_hi
    yim_hi = zre_hi * kim_hi + zim_hi * kre_hi

    @pl.when(i == 0)
    def _():
        sign = _alternating_sign(0, half)
        mid_re = jnp.sum(z_even.astype(f32) * sign, axis=0, keepdims=True)
        mid_im = -jnp.sum(z_odd.astype(f32) * sign, axis=0, keepdims=True)
        k_re, k_im = ks_ref[1, 0:1, :], ks_ref[3, 0:1, :]
        mid_ref[0:1, :] = mid_re * k_re - mid_im * k_im
        mid_ref[1:2, :] = mid_re * k_im + mid_im * k_re

    @pl.when(i != 0)
    def _():
        mid_ref[...] = jnp.zeros_like(mid_ref)

    packed = (i * tf + lax.broadcasted_iota(jnp.int32, (tf, 1), 0)) == 0
    o_ref[0] = (yre_lo + yre_hi).astype(bf16)
    o_ref[1] = jnp.where(packed, mid_ref[0:1, :], yim_lo - yim_hi).astype(bf16)
    o_ref[2] = (yre_lo - yre_hi).astype(bf16)
    o_ref[3] = jnp.where(packed, mid_ref[1:2, :], yim_lo + yim_hi).astype(bf16)


def _hy_inv_kernel(tab_ref, il_ref, ys_ref, z_ref, x0_ref, x_ref, mod_ref, ng_ref, d_ref, wo_ref, bo_ref, o_ref, *,
                   seq):
    i = pl.program_id(1)
    tm = x_ref.shape[0] // 2
    half = seq // 2
    rows = _table_rows(tab_ref, i, tm)
    sign = _alternating_sign(i * tm, tm)
    conv_even = (jnp.dot(rows[:, :seq], ys_ref[:seq, :], preferred_element_type=f32)
                 + sign * ys_ref[half:half + 1, :].astype(f32))
    conv_odd = (jnp.dot(rows[:, seq:], ys_ref[seq:, :], preferred_element_type=f32)
                - sign * ys_ref[3 * half:3 * half + 1, :].astype(f32))
    conv = jnp.dot(il_ref[...], jnp.concatenate([conv_even, conv_odd], axis=0).astype(bf16),
                   preferred_element_type=f32)
    y = conv + z_ref[...] * d_ref[...]
    out = jnp.dot((y * x0_ref[...]).astype(bf16), wo_ref[...], preferred_element_type=f32) + bo_ref[...]
    o_ref[...] = _residual(x_ref[...], out, ng_ref[1:2, :], mod_ref[2:3, :])


def _hyena_layer(x, trunk, layer, mod, norm_g, p):
    nb, seq = trunk.batch, trunk.seq
    tl = _tile_rows(trunk, 256)
    x8 = x.reshape(nb, seq // 8, 8, D_MODEL)
    prev_spec, next_spec = _halo_specs(trunk, tl)
    x0, z, zb = pl.pallas_call(
        _hy_in_kernel,
        grid=(nb, seq // tl),
        in_specs=[_x_spec(tl), prev_spec, next_spec, _mod_spec(trunk, layer), _ng_spec(layer),
                  _resident((D_MODEL, 3 * D_MODEL)), _resident((1, 3 * D_MODEL)),
                  _resident((3, 3 * D_MODEL)), _resident((1, 3 * D_MODEL)), _resident((tl, tl))],
        out_specs=[_x_spec(tl), _x_spec(tl),
                   pl.BlockSpec((None, 2, tl // 2, D_MODEL), lambda b, j: (b, 0, j, 0))],
        out_shape=[jax.ShapeDtypeStruct(x.shape, f32)] * 2
        + [jax.ShapeDtypeStruct((nb, 2, seq // 2, D_MODEL), bf16)],
        compiler_params=_params("parallel", "parallel"),
    )(x, x8, x8, mod, norm_g, p["w_in"], p["b_in"], p["conv_w"], p["conv_b"],
      jnp.asarray(_parity_permutation(tl), dtype=bf16))

    half = seq // 2
    fwd_table, inv_table = (_mxu_table(t) for t in _hyena_tables(seq))
    table_spec = _resident((half, 2 * seq))
    tn = 256
    nblk = D_MODEL // tn
    ks = pl.pallas_call(
        functools.partial(_hy_filter_kernel, seq=seq),
        grid=(nblk,),
        in_specs=[_resident((seq, HY_EMB_PAD)), _resident((HY_EMB_PAD, HY_FILT)), _resident((1, HY_FILT)),
                  _resident((2, HY_FILT)), _resident((HY_FILT, HY_FILT)), _resident((1, HY_FILT)),
                  pl.BlockSpec((HY_FILT, tn), lambda j: (0, j)),
                  pl.BlockSpec((HY_FILT, tn), lambda j: (0, nblk + j)),
                  pl.BlockSpec((1, tn), lambda j: (0, j)),
                  table_spec],
        out_specs=pl.BlockSpec((4, half, tn), lambda j: (0, 0, j)),
        out_shape=jax.ShapeDtypeStruct((4, half, D_MODEL), f32),
        scratch_shapes=[pltpu.VMEM((seq, HY_FILT), f32)],
        compiler_params=_params("arbitrary"),
    )(jnp.asarray(_hyena_features(seq)), p["f_w1"], p["f_b1"], p["f_freq"], p["f_w2"], p["f_b2"],
      p["f_w3"], p["f_w3"], jnp.asarray(_hyena_decay_rates()), fwd_table)

    tf = min(half, 512)
    mode = pl.Buffered(1) if half // tf > 1 else None
    ys = pl.pallas_call(
        functools.partial(_hy_fwd_kernel, seq=seq),
        grid=(nb, half // tf),
        in_specs=[table_spec,
                  pl.BlockSpec((None, 2, half, D_MODEL), lambda b, i: (b, 0, 0, 0), pipeline_mode=mode),
                  pl.BlockSpec((4, tf, D_MODEL), lambda b, i: (0, i, 0))],
        out_specs=pl.BlockSpec((None, 4, tf, D_MODEL), lambda b, i: (b, 0, i, 0)),
        out_shape=jax.ShapeDtypeStruct((nb, 4, half, D_MODEL), bf16),
        scratch_shapes=[pltpu.VMEM((2, D_MODEL), f32)],
        compiler_params=_params("parallel", "arbitrary"),
    )(fwd_table, zb, ks)

    tm = min(half, 256)
    pair_tile = _x_spec(2 * tm)
    return pl.pallas_call(
        functools.partial(_hy_inv_kernel, seq=seq),
        grid=(nb, half // tm),
        in_specs=[table_spec, _resident((2 * tm, 2 * tm)),
                  pl.BlockSpec((None, 2 * seq, D_MODEL), lambda b, i: (b, 0, 0)),
                  pair_tile, pair_tile, pair_tile, _mod_spec(trunk, layer), _ng_spec(layer),
                  _resident((1, D_MODEL)), _resident((D_MODEL, D_MODEL)), _resident((1, D_MODEL))],
        out_specs=pair_tile,
        out_shape=jax.ShapeDtypeStruct(x.shape, f32),
        compiler_params=_params("parallel", "parallel"),
    )(inv_table, jnp.asarray(_parity_permutation(2 * tm).T, dtype=bf16), ys.reshape(nb, 2 * seq, D_MODEL),
      z, x0, x, mod, norm_g, p["d"], p["w_out"], p["b_out"])


def _head_sum(t, ones_pair):
    tb = t.astype(bf16)
    cols = [jnp.dot(tb[:, p * PAIR_W:(p + 1) * PAIR_W], ones_pair, preferred_element_type=f32)
            for p in range(D_MODEL // PAIR_W)]
    return jnp.concatenate(cols, axis=1)


def _rw_pre_kernel(x_ref, xp_ref, xn_ref, mod_ref, ng_ref, mix_ref, wr_ref, wk_ref, wv_ref, g1_ref, g2_ref,
                   w0_ref, w1_ref, w2_ref, a0_ref, a1_ref, a2_ref, kkp_ref, kap_ref, rkp_ref, ones_ref,
                   r_ref, k_ref, v_ref, kk_ref, g_ref, lwf_ref, lwb_ref, af_ref, ab_ref, bn_ref):
    j = pl.program_id(1)
    g, sh, sc = ng_ref[0:1, :], mod_ref[0:1, :], mod_ref[1:2, :]
    h = _norm_mod(x_ref[...], g, sh, sc)
    prev_row = jnp.where(j == 0, 0.0, _norm_mod(xp_ref[...], g, sh, sc)[7:8, :])
    next_row = jnp.where(j == pl.num_programs(1) - 1, 0.0, _norm_mod(xn_ref[...], g, sh, sc)[0:1, :])
    h_prev, h_next = _shifted(h, prev_row, next_row)
    xx = 0.5 * (h_prev + h_next) - h

    def mixed(m):
        return (h + xx * mix_ref[m:m + 1, :]).astype(bf16)

    r = jnp.dot(mixed(0), wr_ref[...], preferred_element_type=f32)
    xw = mixed(1)
    k = jnp.dot(mixed(2), wk_ref[...], preferred_element_type=f32)
    v = jnp.dot(mixed(3), wv_ref[...], preferred_element_type=f32)
    xa = mixed(4)
    r_ref[...] = r
    k_ref[...] = k
    v_ref[...] = v
    gate = _sigmoid(jnp.dot(mixed(5), g1_ref[...], preferred_element_type=f32))
    g_ref[...] = jnp.dot(gate.astype(bf16), g2_ref[...], preferred_element_type=f32)
    ones = ones_ref[...]
    kk = k * kkp_ref[...]
    kk_ref[...] = kk * lax.rsqrt(jnp.maximum(_head_sum(kk * kk, ones), 1e-24))
    a_sum = None
    for dd, (lw_ref, a_ref) in enumerate(((lwf_ref, af_ref), (lwb_ref, ab_ref))):
        lora = jnp.tanh(jnp.dot(xw, w1_ref[dd], preferred_element_type=f32))
        wl = w0_ref[dd:dd + 1, :] + jnp.dot(lora.astype(bf16), w2_ref[dd], preferred_element_type=f32)
        lw_ref[...] = -math.exp(-0.5) * _sigmoid(wl)
        al = jnp.dot(xa, a1_ref[dd], preferred_element_type=f32)
        a = _sigmoid(a0_ref[dd:dd + 1, :] + jnp.dot(al.astype(bf16), a2_ref[dd], preferred_element_type=f32))
        a_ref[...] = a
        a_sum = a if a_sum is None else a_sum + a
    kd_sum = k * (2.0 + (a_sum - 2.0) * kap_ref[...])
    bn_ref[...] = _head_sum(r * kd_sum * rkp_ref[...], ones) * v


def _stack_heads(x, first_head):
    return jnp.concatenate([jnp.where(first_head, x, 0.0), jnp.where(first_head, 0.0, x)], axis=0).astype(bf16)


def _dot_nt(a, b):
    return lax.dot_general(a, b, (((1,), (1,)), ((), ())), preferred_element_type=f32)


def _dot_tn(a, b):
    return lax.dot_general(a, b, (((0,), (0,)), ((), ())), preferred_element_type=f32)


def _rw_scan_kernel(r_ref, k_ref, v_ref, kk_ref, lw_ref, a_ref, kap_ref, s0_ref, o_ref, s_ref, *, reverse):
    c = pl.program_id(1)
    C, W, G = SCAN_CHUNK, PAIR_W, SCAN_GROUP

    @pl.when(c == 0)
    def _():
        s_ref[...] = s0_ref[...]

    row = lax.broadcasted_iota(jnp.int32, (C, W), 0)
    lane = lax.broadcasted_iota(jnp.int32, (C, W), 1)
    col = lane & (C - 1)
    first_head = lane < RW_HEAD
    before = (col > row) if reverse else (col < row)
    before_eq = before | (col == row)
    eye = (col == row).astype(f32)
    row_w = lax.broadcasted_iota(jnp.int32, (W, W), 0)
    lane_w = lax.broadcasted_iota(jnp.int32, (W, W), 1)
    same_head = (row_w >= RW_HEAD) == (lane_w >= RW_HEAD)
    diag_w = row_w == lane_w
    merge_masks = []
    s = 1
    while s < C:
        sh = s.bit_length() - 1
        same_pair = (row >> (sh + 1)) == (col >> (sh + 1))
        later, earlier = ((row >> sh) & 1, (col >> sh) & 1)
        merge_masks.append(same_pair & (((later == 0) & (earlier == 1)) if reverse
                                        else ((later == 1) & (earlier == 0))))
        s *= 2

    row_c = lax.broadcasted_iota(jnp.int32, (G * C, G * C), 0)
    col_c = lax.broadcasted_iota(jnp.int32, (G * C, G * C), 1)
    chunk_bits = C.bit_length() - 1
    same_chunk = (row_c >> chunk_bits) == (col_c >> chunk_bits)
    tri = (same_chunk & ((col_c >= row_c) if reverse else (col_c <= row_c))).astype(bf16)
    lw_all = lw_ref[...]
    cum_all = _dot_exact_rhs_left(tri, lw_all)
    n_pairs = D_MODEL // W

    def pairs(x):
        return [x[g * C:(g + 1) * C, p * W:(p + 1) * W] for g in range(G) for p in range(n_pairs)]

    def per_chunk_rows(rows):
        return [rows[g][:, p * W:(p + 1) * W] for g in range(G) for p in range(n_pairs)]

    def each(fn, *lists):
        return [fn(*args) for args in zip(*lists)]

    def stack(x):
        return _stack_heads(x, first_head)

    def mm(x, y):
        return jnp.dot(x.astype(bf16), stack(y), preferred_element_type=f32)

    def mm2(x, y0, y1):
        return jnp.dot(x.astype(bf16), jnp.concatenate([stack(y0), stack(y1)], axis=1),
                       preferred_element_type=f32)

    r, k, v, kk, a = (pairs(ref[...]) for ref in (r_ref, k_ref, v_ref, kk_ref, a_ref))
    lw, cum = pairs(lw_all), pairs(cum_all)
    tot = per_chunk_rows([jnp.sum(lw_all[g * C:(g + 1) * C, :], axis=0, keepdims=True) for g in range(G)])
    ka = per_chunk_rows([kap_ref[...]] * G)
    kd = each(lambda k_, a_, ka_: k_ * (1.0 + (a_ - 1.0) * ka_), k, a, ka)
    b = each(lambda kk_, a_: kk_ * a_, kk, a)
    e_in = each(jnp.exp, cum)
    e_out = each(lambda c_: jnp.exp(-c_), cum)
    e_end = each(lambda t_, c_: jnp.exp(t_ - c_), tot, cum)
    kap = each(lambda kk_, c_, l_: kk_ * jnp.exp(c_ - l_), kk, cum, lw)
    mul = lambda x_, y_: x_ * y_
    kt, bt, rt, kh, bh = each(mul, kd, e_out), each(mul, b, e_out), each(mul, r, e_in), each(mul, kd, e_end), \
        each(mul, b, e_end)

    lhs = each(lambda x_, y_: jnp.concatenate([x_, y_], axis=0).astype(bf16), kap, rt)
    gram = each(lambda l_, y0_, y1_: _dot_nt(l_, jnp.concatenate([stack(y0_), stack(y1_)], axis=0)), lhs, kt, bt)
    m_kk = each(lambda g_: jnp.where(before, g_[:C, :W], 0.0), gram)
    n_kb = each(lambda g_: jnp.where(before, g_[:C, W:], 0.0), gram)
    a_qk = each(lambda g_: jnp.where(before_eq, g_[C:, :W], 0.0), gram)
    a_qb = each(lambda g_: jnp.where(before_eq, g_[C:, W:], 0.0), gram)

    tinv = each(lambda n_: eye - jnp.where(merge_masks[0], n_, 0.0), n_kb)
    for off in merge_masks[1:]:
        tn = each(lambda t_, n_: mm(t_, jnp.where(off, n_, 0.0)), tinv, n_kb)
        tinv = each(lambda t_, tn_: t_ - mm(tn_, t_), tinv, tn)

    mva = each(lambda m_, a_, v_: mm(jnp.concatenate([m_, a_], axis=0), v_), m_kk, a_qk, v)
    mv, av = each(lambda t_: t_[:C, :], mva), each(lambda t_: t_[C:, :], mva)
    tk = each(mm2, tinv, kap, mv)
    kt_c, w1 = each(lambda t_: t_[:, :W], tk), each(lambda t_: t_[:, W:], tk)
    qa = each(mm2, a_qb, kt_c, w1)
    q_eff = each(lambda rt_, qa_: rt_ - qa_[:, :W], rt, qa)
    o_intra = each(lambda av_, qa_: av_ - qa_[:, W:], av, qa)
    kb = each(lambda x_, b_: _dot_tn(x_.astype(bf16), b_.astype(bf16)), kt_c, bh)
    phi = each(lambda t_, kb_: jnp.where(same_head, jnp.where(diag_w, jnp.exp(t_), 0.0) - kb_, 0.0), tot, kb)
    ds_full = each(lambda v_, w_, kh_, bh_: _dot_tn(jnp.concatenate([v_, -w_], axis=0).astype(bf16),
                                                    jnp.concatenate([kh_, bh_], axis=0).astype(bf16)),
                   v, w1, kh, bh)
    d_s = each(lambda d_: jnp.where(first_head, d_[:RW_HEAD, :], d_[RW_HEAD:, :]), ds_full)
    state = [s_ref[:, p * W:(p + 1) * W] for p in range(n_pairs)]
    for g in (range(G - 1, -1, -1) if reverse else range(G)):
        sel = slice(g * n_pairs, (g + 1) * n_pairs)
        o = each(lambda q_, s_, oi_: _dot_nt(q_.astype(bf16), stack(s_)) + oi_, q_eff[sel], state, o_intra[sel])
        state = each(lambda s_, ph_, ds_: jnp.dot(s_.astype(bf16), ph_.astype(bf16),
                                                  preferred_element_type=f32) + ds_, state, phi[sel], d_s[sel])
        o_ref[g * C:(g + 1) * C, :] = jnp.concatenate(o, axis=1)
    s_ref[...] = jnp.concatenate(state, axis=1)


def _rw_post_kernel(of_ref, ob_ref, bn_ref, g_ref, x_ref, mod_ref, ng_ref, lw_ref, lb_ref, ones_ref,
                    wo_ref, o_ref):
    o = of_ref[...] + ob_ref[...]
    ones = ones_ref[...]
    inv_n = 1.0 / RW_HEAD
    dev = o - _head_sum(o, ones) * inv_n
    var = _head_sum(dev * dev, ones) * inv_n
    on = dev * lax.rsqrt(var + RW_LN_EPS) * lw_ref[...] + lb_ref[...] + bn_ref[...]
    y = jnp.dot((on * g_ref[...]).astype(bf16), wo_ref[...], preferred_element_type=f32)
    o_ref[...] = _residual(x_ref[...], y, ng_ref[1:2, :], mod_ref[2:3, :])


@functools.lru_cache(maxsize=None)
def _pair_ones():
    idx = np.arange(PAIR_W) // RW_HEAD
    return np.asarray(idx[:, None] == idx[None, :], dtype=bf16)


def _rwkv_layer(x, trunk, layer, mod, norm_g, p, s0_fwd, s0_bwd):
    nb, seq = trunk.batch, trunk.seq
    tl = _tile_rows(trunk, 256)
    x8 = x.reshape(nb, seq // 8, 8, D_MODEL)
    prev_spec, next_spec = _halo_specs(trunk, tl)
    lora_w, lora_a, lora_g = p["w1"].shape[-1], p["a1"].shape[-1], p["g1"].shape[-1]
    ones = jnp.asarray(_pair_ones())
    outs = pl.pallas_call(
        _rw_pre_kernel,
        grid=(nb, seq // tl),
        in_specs=[_x_spec(tl), prev_spec, next_spec, _mod_spec(trunk, layer), _ng_spec(layer),
                  _resident((6, D_MODEL)),
                  _resident((D_MODEL, D_MODEL)), _resident((D_MODEL, D_MODEL)), _resident((D_MODEL, D_MODEL)),
                  _resident((D_MODEL, lora_g)), _resident((lora_g, D_MODEL)),
                  _resident((2, D_MODEL)), _resident((2, D_MODEL, lora_w)), _resident((2, lora_w, D_MODEL)),
                  _resident((2, D_MODEL)), _resident((2, D_MODEL, lora_a)), _resident((2, lora_a, D_MODEL)),
                  _resident((1, D_MODEL)), _resident((1, D_MODEL)), _resident((1, D_MODEL)),
                  _resident((PAIR_W, PAIR_W))],
        out_specs=[_x_spec(tl)] * 10,
        out_shape=[jax.ShapeDtypeStruct(x.shape, f32)] * 10,
        compiler_params=_params("parallel", "parallel"),
    )(x, x8, x8, mod, norm_g, p["mix"], p["wr"], p["wk"], p["wv"], p["g1"], p["g2"],
      p["w0"], p["w1"], p["w2"], p["a0"], p["a1"], p["a2"], p["kk"], p["ka"], p["rk"], ones)
    r, k, v, kk, gate, lw_f, lw_b, a_f, a_b, bonus = outs

    def state_in(s):
        return s.transpose(0, 2, 1, 3).reshape(nb, RW_HEAD, D_MODEL)

    def state_out(s):
        return s.reshape(nb, RW_HEAD, RW_HEADS, RW_HEAD).transpose(0, 2, 1, 3)

    rows = SCAN_CHUNK * SCAN_GROUP
    n_chunks = seq // rows
    state_spec = pl.BlockSpec((None, RW_HEAD, D_MODEL), lambda b, c: (b, 0, 0))
    results = []
    for reverse, lw, a, s0 in ((False, lw_f, a_f, s0_fwd), (True, lw_b, a_b, s0_bwd)):
        if reverse:
            chunk_spec = pl.BlockSpec((None, rows, D_MODEL), lambda b, c: (b, n_chunks - 1 - c, 0))
        else:
            chunk_spec = pl.BlockSpec((None, rows, D_MODEL), lambda b, c: (b, c, 0))
        results.append(pl.pallas_call(
            functools.partial(_rw_scan_kernel, reverse=reverse),
            grid=(nb, n_chunks),
            in_specs=[chunk_spec] * 6 + [_resident((1, D_MODEL)), state_spec],
            out_specs=[chunk_spec, state_spec],
            out_shape=[jax.ShapeDtypeStruct(x.shape, f32),
                       jax.ShapeDtypeStruct((nb, RW_HEAD, D_MODEL), f32)],
            compiler_params=_params("parallel", "arbitrary"),
        )(r, k, v, kk, lw, a, p["ka"], state_in(s0)))
    (o_f, s_f), (o_b, s_b) = results

    y = pl.pallas_call(
        _rw_post_kernel,
        grid=(nb, seq // tl),
        in_specs=[_x_spec(tl)] * 5 + [_mod_spec(trunk, layer), _ng_spec(layer),
                                      _resident((1, D_MODEL)), _resident((1, D_MODEL)),
                                      _resident((PAIR_W, PAIR_W)), _resident((D_MODEL, D_MODEL))],
        out_specs=_x_spec(tl),
        out_shape=jax.ShapeDtypeStruct(x.shape, f32),
        compiler_params=_params("parallel", "parallel"),
    )(o_f, o_b, bonus, gate, x, mod, norm_g, p["lnx_w"], p["lnx_b"], ones, p["wo"])
    return y, state_out(s_f), state_out(s_b)


def kernel(x_prompt, x_sample, state_wkv_fwd, state_wkv_bwd, c, c_ctx, ada_w, ada_b, norm_g, ffn_w1, ffn_w3, ffn_w2, fn_w_out, fn_b_out, hy_w_in, hy_b_in, hy_conv_w, hy_conv_b, hy_f_w1, hy_f_b1, hy_f_freq, hy_f_w2, hy_f_b2, hy_f_w3, hy_d, hy_w_out, hy_b_out, rw_mix, rw_wr, rw_wk, rw_wv, rw_wo, rw_w0, rw_w1, rw_w2, rw_a0, rw_a1, rw_a2, rw_g1, rw_g2, rw_kk, rw_ka, rw_rk, rw_lnx_w, rw_lnx_b):
    n_ctx, n_dec = x_prompt.shape[0], x_sample.shape[0]
    assert 1 + n_dec <= MOD_ROWS
    trunks = (Trunk(n_ctx, x_prompt.shape[1], 0, False), Trunk(n_dec, x_sample.shape[1], 1, True))
    cond = jnp.zeros((MOD_ROWS, D_MODEL), f32).at[0].set(c_ctx).at[1:1 + n_dec].set(c)
    mod = _adaln(cond, ada_w, ada_b)

    n_rwkv = rw_w0.shape[0]
    zero_state = jnp.zeros((n_ctx, n_rwkv, RW_HEADS, RW_HEAD, RW_HEAD), f32)
    states = ((zero_state, zero_state), (state_wkv_fwd, state_wkv_bwd))
    xs = [x_prompt, x_sample]
    ffn_bf16 = [w.astype(bf16) for w in (ffn_w1, ffn_w3, ffn_w2)]
    new_fwd, new_bwd = [], []
    for i in range(DEPTH):
        kind, j = i % N_MIXERS, i // N_MIXERS
        if kind == 0:
            w_out, b_out = fn_w_out[j].astype(bf16), fn_b_out[j][None]
            xs = [_fourier_layer(x, t, i, mod, norm_g, w_out, b_out) for x, t in zip(xs, trunks)]
        elif kind == 1:
            w1_pad = jnp.zeros((HY_EMB_PAD, HY_FILT), f32).at[:HY_EMB].set(hy_f_w1[j])
            p = dict(w_in=hy_w_in[j].astype(bf16), b_in=hy_b_in[j][None], conv_w=hy_conv_w[j],
                     conv_b=hy_conv_b[j][None], f_w1=w1_pad, f_b1=hy_f_b1[j][None], f_freq=hy_f_freq[j],
                     f_w2=hy_f_w2[j], f_b2=hy_f_b2[j][None], f_w3=hy_f_w3[j], d=hy_d[j][None],
                     w_out=hy_w_out[j].astype(bf16), b_out=hy_b_out[j][None])
            xs = [_hyena_layer(x, t, i, mod, norm_g, p) for x, t in zip(xs, trunks)]
        else:
            per_head = lambda t: t.reshape(1, D_MODEL)
            p = dict(mix=rw_mix[j], wr=rw_wr[j].astype(bf16), wk=rw_wk[j].astype(bf16),
                     wv=rw_wv[j].astype(bf16), wo=rw_wo[j].astype(bf16), w0=rw_w0[j],
                     w1=rw_w1[j].astype(bf16), w2=rw_w2[j].astype(bf16), a0=rw_a0[j],
                     a1=rw_a1[j].astype(bf16), a2=rw_a2[j].astype(bf16), g1=rw_g1[j].astype(bf16),
                     g2=rw_g2[j].astype(bf16), kk=per_head(rw_kk[j]), ka=per_head(rw_ka[j]),
                     rk=per_head(rw_rk[j]), lnx_w=rw_lnx_w[j][None], lnx_b=rw_lnx_b[j][None])
            outs = [_rwkv_layer(x, t, i, mod, norm_g, p, sf[:, j], sb[:, j])
                    for x, t, (sf, sb) in zip(xs, trunks, states)]
            xs = [o[0] for o in outs]
            new_fwd.append(outs[0][1])
            new_bwd.append(outs[0][2])
        xs = [_ffn(x, t, i, mod, norm_g, ffn_bf16[0], ffn_bf16[1], ffn_bf16[2]) for x, t in zip(xs, trunks)]
    return xs[0], xs[1], jnp.stack(new_fwd, axis=1), jnp.stack(new_bwd, axis=1)
```

```python
import functools
import math
from typing import NamedTuple

import numpy as np
import jax
import jax.numpy as jnp
from jax import lax
from jax.experimental import pallas as pl
from jax.experimental.pallas import tpu as pltpu

f32 = jnp.float32
bf16 = jnp.bfloat16

D_MODEL = 1024
DEPTH = 4
N_MIXERS = 3
D_FF = 2816
NORM_EPS = 1e-6
FN_GROUP_W = 256
FN_HALO = 16
HY_EMB = 33
HY_EMB_PAD = 128
HY_BANDS = 16
HY_FILT = 64
HY_TARGET = 1e-2
HY_FAST = 0.3
HY_SLOW = 1.5
RW_HEAD = 64
RW_HEADS = 16
RW_LN_EPS = 64e-5
SCAN_CHUNK = 64
SCAN_GROUP = 4
PAIR_W = 2 * RW_HEAD
assert SCAN_CHUNK == RW_HEAD, "the scan keeps (chunk, chunk) and (chunk, head) tiles in one lane layout"
MOD_ROWS = 8
VMEM_LIMIT = 56 * 1024 * 1024


class Trunk(NamedTuple):
    batch: int
    seq: int
    mod_base: int
    per_batch_mod: bool


def _params(*sem):
    return pltpu.CompilerParams(dimension_semantics=sem, vmem_limit_bytes=VMEM_LIMIT)


def _resident(shape):
    nd = len(shape)
    return pl.BlockSpec(shape, lambda *_: (0,) * nd, pipeline_mode=pl.Buffered(1))


def _layer_resident(shape, layer):
    nd = len(shape)
    return pl.BlockSpec((None,) + tuple(shape), lambda *_: (layer,) + (0,) * nd, pipeline_mode=pl.Buffered(1))


def _split(x):
    hi = x.astype(bf16)
    lo = (x - hi.astype(f32)).astype(bf16)
    return hi, lo


def _dot3(a, b):
    ah, al = _split(a)
    bh, bl = _split(b)
    d = functools.partial(jnp.dot, preferred_element_type=f32)
    return d(ah, bh) + (d(ah, bl) + d(al, bh))


def _dot_exact_rhs_left(a_exact, b):
    b0, b1 = _split(b)
    d = functools.partial(jnp.dot, preferred_element_type=f32)
    return d(a_exact, b0) + d(a_exact, b1)


def _sigmoid(x):
    return 0.5 * jnp.tanh(0.5 * x) + 0.5


def _norm_mod(x, g, shift, scale):
    ms = jnp.mean(x * x, axis=-1, keepdims=True)
    return (x * lax.rsqrt(ms + NORM_EPS)) * (g * (1.0 + scale)) + shift


def _residual(x, y, g, gate):
    ms = jnp.mean(y * y, axis=-1, keepdims=True)
    return x + gate * (y * lax.rsqrt(ms + NORM_EPS) * g)


def _shifted(h, prev_row, next_row):
    rows = h.shape[0]
    ridx = lax.broadcasted_iota(jnp.int32, h.shape, 0)
    h_prev = jnp.where(ridx == 0, prev_row, pltpu.roll(h, 1, 0))
    h_next = jnp.where(ridx == rows - 1, next_row, pltpu.roll(h, rows - 1, 0))
    return h_prev, h_next


def _tile_rows(trunk, cap):
    return min(trunk.seq, cap)


def _x_spec(tl):
    return pl.BlockSpec((None, tl, D_MODEL), lambda b, j: (b, j, 0))


def _halo_specs(trunk, tl):
    g = tl // 8
    last = trunk.seq // 8 - 1
    prev = pl.BlockSpec((None, None, 8, D_MODEL), lambda b, j: (b, jnp.maximum(j * g - 1, 0), 0, 0))
    nxt = pl.BlockSpec((None, None, 8, D_MODEL), lambda b, j: (b, jnp.minimum((j + 1) * g, last), 0, 0))
    return prev, nxt


def _mod_spec(trunk, layer):
    if trunk.per_batch_mod:
        return pl.BlockSpec((None, None, 6, D_MODEL), lambda b, *_: (layer, trunk.mod_base + b, 0, 0))
    return pl.BlockSpec((None, None, 6, D_MODEL), lambda b, *_: (layer, trunk.mod_base, 0, 0))


def _ng_spec(layer):
    return pl.BlockSpec((None, 4, D_MODEL), lambda *_: (layer, 0, 0))


def _mod_kernel(c_ref, w_ref, b_ref, o_ref):
    c = c_ref[...]
    o_ref[...] = _dot3(c * _sigmoid(c), w_ref[...]) + b_ref[...]


def _adaln(cond, ada_w, ada_b):
    tn = 1536
    out = pl.pallas_call(
        _mod_kernel,
        grid=(DEPTH, 6 * D_MODEL // tn),
        in_specs=[pl.BlockSpec((MOD_ROWS, D_MODEL), lambda l, j: (0, 0)),
                  pl.BlockSpec((None, D_MODEL, tn), lambda l, j: (l, 0, j)),
                  pl.BlockSpec((None, 1, tn), lambda l, j: (l, 0, j))],
        out_specs=pl.BlockSpec((None, MOD_ROWS, tn), lambda l, j: (l, 0, j)),
        out_shape=jax.ShapeDtypeStruct((DEPTH, MOD_ROWS, 6 * D_MODEL), f32),
        compiler_params=_params("arbitrary", "arbitrary"),
    )(cond, ada_w, ada_b.reshape(DEPTH, 1, 6 * D_MODEL))
    return out.reshape(DEPTH, MOD_ROWS, 6, D_MODEL)


def _ffn_kernel(x_ref, mod_ref, ng_ref, w1_ref, w3_ref, w2_ref, o_ref):
    tl = x_ref.shape[0]
    groups = 2 if tl >= 512 else 1
    tg = tl // groups
    for g in range(groups):
        rs = slice(g * tg, (g + 1) * tg)
        x = x_ref[rs, :]
        h = _norm_mod(x, ng_ref[2:3, :], mod_ref[3:4, :], mod_ref[4:5, :]).astype(bf16)
        a = jnp.dot(h, w1_ref[...], preferred_element_type=f32)
        b = jnp.dot(h, w3_ref[...], preferred_element_type=f32)
        gated = (a * _sigmoid(a) * b).astype(bf16)
        y = jnp.dot(gated, w2_ref[...], preferred_element_type=f32)
        o_ref[rs, :] = _residual(x, y, ng_ref[3:4, :], mod_ref[5:6, :])


def _merge_sequences(x, trunk):
    if trunk.per_batch_mod:
        return x, trunk
    return x.reshape(1, -1, D_MODEL), Trunk(1, trunk.batch * trunk.seq, trunk.mod_base, False)


def _ffn(x, trunk, layer, mod, norm_g, w1, w3, w2):
    shape = x.shape
    x, trunk = _merge_sequences(x, trunk)
    tl = _tile_rows(trunk, 512)
    return pl.pallas_call(
        _ffn_kernel,
        grid=(trunk.batch, trunk.seq // tl),
        in_specs=[_x_spec(tl), _mod_spec(trunk, layer), _ng_spec(layer),
                  _layer_resident((D_MODEL, D_FF), layer), _layer_resident((D_MODEL, D_FF), layer),
                  _layer_resident((D_FF, D_MODEL), layer)],
        out_specs=_x_spec(tl),
        out_shape=jax.ShapeDtypeStruct(x.shape, f32),
        compiler_params=_params("parallel", "parallel"),
    )(x, mod, norm_g, w1, w3, w2).reshape(shape)


@functools.lru_cache(maxsize=None)
def _channel_dft():
    w = FN_GROUP_W
    idx = np.arange(w)
    ang = 2.0 * np.pi * ((idx[:, None] * idx[None, :]) % w) / w
    return (np.concatenate([np.cos(ang), np.sin(ang)], axis=1) / math.sqrt(w)).astype(np.float32)


def _table_rows(cs_ref, tile, rows):
    return cs_ref[pl.ds(pl.multiple_of(tile * rows, rows), rows), :]


def _mxu_table(table):
    return jnp.asarray(table).astype(bf16)


@functools.lru_cache(maxsize=None)
def _fourier_half_table(seq):
    k = np.arange(seq // 2 + FN_HALO)
    t = np.arange(seq)
    ang = 2.0 * np.pi * ((k[:, None] * t[None, :]) % seq) / seq
    return (np.concatenate([np.cos(ang), np.sin(ang)], axis=1) / math.sqrt(seq)).astype(np.float32)


@functools.lru_cache(maxsize=None)
def _row_reversal(tm):
    out = np.zeros((tm, tm + FN_HALO), np.float32)
    out[np.arange(tm), tm - np.arange(tm)] = 1.0
    return np.asarray(out, dtype=bf16)


def _fn_kernel(cs_ref, rev_ref, wch_ref, xs_ref, x_ref, mod_ref, ng_ref, wo_ref, bo_ref, o_ref,
               a_ref, b_ref, mirror_ref, *, seq):
    s = pl.program_id(1)
    half_tiles = pl.num_programs(1) // 2
    n_seq, tm = x_ref.shape[0], x_ref.shape[1]
    src_rows = tm + FN_HALO

    @pl.when(s == 0)
    def _():
        w = FN_GROUP_W
        chunk = min(seq, 512)
        for i in range(n_seq):
            for c in range(seq // chunk):
                rs = slice(c * chunk, (c + 1) * chunk)
                h = _norm_mod(xs_ref[i, rs, :], ng_ref[0:1, :], mod_ref[0:1, :], mod_ref[1:2, :]).astype(bf16)
                for g in range(D_MODEL // w):
                    ab = jnp.dot(h[:, g * w:(g + 1) * w], wch_ref[...], preferred_element_type=f32).astype(bf16)
                    a_ref[i, rs, g * w:(g + 1) * w] = ab[:, :w]
                    b_ref[i, rs, g * w:(g + 1) * w] = ab[:, w:]

    def finish(f_bf16):
        f = jnp.concatenate(f_bf16, axis=0)
        y = jnp.dot(f, wo_ref[...], preferred_element_type=f32) + bo_ref[...]
        x = x_ref[...].reshape(n_seq * tm, D_MODEL)
        o_ref[...] = _residual(x, y, ng_ref[1:2, :], mod_ref[2:3, :]).reshape(n_seq, tm, D_MODEL)

    @pl.when(s < half_tiles)
    def _():
        first = pl.multiple_of(s * tm, tm)
        rows = cs_ref[pl.ds(first, src_rows), :]
        out = []
        for i in range(n_seq):
            p = jnp.dot(rows[:, :seq], a_ref[i], preferred_element_type=f32)
            q = jnp.dot(rows[:, seq:], b_ref[i], preferred_element_type=f32)
            mirror_ref[i, pl.ds(first, src_rows), :] = (p + q).astype(bf16)
            out.append((p - q)[:tm, :].astype(bf16))
        finish(out)

    @pl.when(s >= half_tiles)
    def _():
        first = pl.multiple_of((2 * half_tiles - 1 - s) * tm, tm)
        finish([jnp.dot(rev_ref[...], mirror_ref[i, pl.ds(first, src_rows), :],
                        preferred_element_type=f32).astype(bf16) for i in range(n_seq)])


def _fourier_layer(x, trunk, layer, mod, norm_g, w_out, b_out):
    nb, seq = trunk.batch, trunk.seq
    tm = min(seq // 2, 512)
    half_rows = seq // 2 + FN_HALO
    n_seq = 1 if trunk.per_batch_mod else math.gcd(nb, max(1, 512 // tm))
    tile = pl.BlockSpec((n_seq, tm, D_MODEL), lambda b, i: (b, i, 0))
    mode = pl.Buffered(1) if seq // tm > 2 else None
    whole = pl.BlockSpec((n_seq, seq, D_MODEL), lambda b, i: (b, 0, 0), pipeline_mode=mode)
    return pl.pallas_call(
        functools.partial(_fn_kernel, seq=seq),
        grid=(nb // n_seq, seq // tm),
        in_specs=[_resident((half_rows, 2 * seq)), _resident((tm, tm + FN_HALO)),
                  _resident((FN_GROUP_W, 2 * FN_GROUP_W)), whole,
                  tile, _mod_spec(trunk, layer), _ng_spec(layer),
                  _resident((D_MODEL, D_MODEL)), _resident((1, D_MODEL))],
        out_specs=tile,
        out_shape=jax.ShapeDtypeStruct(x.shape, f32),
        scratch_shapes=[pltpu.VMEM((n_seq, seq, D_MODEL), bf16), pltpu.VMEM((n_seq, seq, D_MODEL), bf16),
                        pltpu.VMEM((n_seq, half_rows, D_MODEL), bf16)],
        compiler_params=_params("parallel", "arbitrary"),
    )(_mxu_table(_fourier_half_table(seq)), jnp.asarray(_row_reversal(tm)), _mxu_table(_channel_dft()),
      x, x, mod, norm_g, w_out, b_out)


@functools.lru_cache(maxsize=None)
def _hyena_features(seq):
    t = np.linspace(0.0, 1.0, seq)[:, None]
    ang = 2.0 * np.pi * np.arange(seq)[:, None] / seq
    bands = np.linspace(1e-4, HY_BANDS - 1, HY_BANDS)[None]
    z = np.concatenate([t, np.cos(bands * ang), -np.sin(bands * ang)], axis=-1)
    out = np.zeros((seq, HY_EMB_PAD), np.float32)
    out[:, :HY_EMB] = np.concatenate([z[0::2], z[1::2]], axis=0)
    return out


@functools.lru_cache(maxsize=None)
def _hyena_decay_rates():
    d = np.linspace(math.log(HY_TARGET) / HY_FAST, math.log(HY_TARGET) / HY_SLOW, D_MODEL)
    return np.abs(d)[None].astype(np.float32)


@functools.lru_cache(maxsize=None)
def _hyena_tables(seq):
    half = seq // 2
    p = np.arange(half)[:, None]
    m = np.arange(half)[None, :]
    ang_e = np.pi * ((p * 2 * m) % (2 * seq)) / seq
    ang_o = np.pi * ((p * (2 * m + 1)) % (2 * seq)) / seq
    fwd = np.concatenate([np.cos(ang_e), np.cos(ang_o), np.sin(ang_e), np.sin(ang_o)], axis=1)
    inv = np.concatenate([np.cos(ang_e).T, -np.sin(ang_e).T, np.cos(ang_o).T, -np.sin(ang_o).T], axis=1)
    return fwd.astype(np.float32), inv.astype(np.float32)


@functools.lru_cache(maxsize=None)
def _parity_permutation(n):
    out = np.zeros((n, n), np.float32)
    half = n // 2
    out[np.arange(half), 2 * np.arange(half)] = 1.0
    out[half + np.arange(half), 2 * np.arange(half) + 1] = 1.0
    return out


def _hy_in_kernel(x_ref, xp_ref, xn_ref, mod_ref, ng_ref, w_ref, b_ref, cw_ref, cb_ref, sel_ref,
                  x0_ref, z_ref, zb_ref):
    j = pl.program_id(1)
    g, sh, sc = ng_ref[0:1, :], mod_ref[0:1, :], mod_ref[1:2, :]
    tl = x_ref.shape[0]
    rows = jnp.concatenate([x_ref[...], xp_ref[...], xn_ref[...]], axis=0)
    u_all = jnp.dot(_norm_mod(rows, g, sh, sc).astype(bf16), w_ref[...], preferred_element_type=f32) + b_ref[...]
    u = u_all[:tl, :]
    prev_row = jnp.where(j == 0, 0.0, u_all[tl + 7:tl + 8, :])
    next_row = jnp.where(j == pl.num_programs(1) - 1, 0.0, u_all[tl + 8:tl + 9, :])
    u_prev, u_next = _shifted(u, prev_row, next_row)
    uc = u_prev * cw_ref[0:1, :] + u * cw_ref[1:2, :] + u_next * cw_ref[2:3, :] + cb_ref[...]
    x0_ref[...] = uc[:, :D_MODEL]
    z = uc[:, 2 * D_MODEL:] * uc[:, D_MODEL:2 * D_MODEL]
    z_ref[...] = z
    split = jnp.dot(sel_ref[...], z.astype(bf16), preferred_element_type=f32).astype(bf16)
    zb_ref[0] = split[:tl // 2, :]
    zb_ref[1] = split[tl // 2:, :]


def _alternating_sign(first, rows):
    t = first + lax.broadcasted_iota(jnp.int32, (rows, 1), 0)
    return (1 - 2 * (t & 1)).astype(f32)


def _parity_dft(rows, even, odd, half):
    d = functools.partial(jnp.dot, preferred_element_type=f32)
    return (d(rows[:, :half], even), d(rows[:, half:2 * half], odd),
            d(rows[:, 2 * half:3 * half], even), d(rows[:, 3 * half:], odd))


def _hy_filter_kernel(feat_ref, w1_ref, b1_ref, fq_ref, w2_ref, b2_ref, w3f_ref, w3b_ref, rate_ref,
                      tab_ref, o_ref, hdn_ref, *, seq):
    half = seq // 2

    @pl.when(pl.program_id(0) == 0)
    def _():
        hdn = jnp.sin(fq_ref[0:1, :] * (_dot3(feat_ref[...], w1_ref[...]) + b1_ref[...]))
        hdn_ref[...] = jnp.sin(fq_ref[1:2, :] * (_dot3(hdn, w2_ref[...]) + b2_ref[...]))

    hdn = hdn_ref[...]
    rows = lax.broadcasted_iota(jnp.int32, (seq, 1), 0)
    lag = jnp.where(rows < half, 2 * rows, 2 * (rows - half) + 1)
    win = jnp.exp(-(lag.astype(f32) * (1.0 / (seq - 1))) * rate_ref[...])
    k_fwd = _dot3(hdn, w3f_ref[...]) * win
    k_bwd = _dot3(hdn, w3b_ref[...]) * win
    both = k_fwd + k_bwd
    diff = k_fwd - k_bwd
    ec, oc, _, _ = _parity_dft(tab_ref[...], both[:half].astype(bf16), both[half:].astype(bf16), half)
    _, _, es, os_ = _parity_dft(tab_ref[...], diff[:half].astype(bf16), diff[half:].astype(bf16), half)
    sign = _alternating_sign(0, half)
    mid_re = jnp.sum(both[:half] * sign, axis=0, keepdims=True)
    mid_im = -jnp.sum(diff[half:] * sign, axis=0, keepdims=True)
    first = lax.broadcasted_iota(jnp.int32, (half, 1), 0) == 0
    wgt = jnp.where(first, 0.5 / seq, 1.0 / seq)
    o_ref[0] = (ec + oc) * wgt
    o_ref[1] = jnp.where(first, mid_re * (1.0 / seq), -(es + os_) * wgt)
    o_ref[2] = (ec - oc) * wgt
    o_ref[3] = jnp.where(first, mid_im * (1.0 / seq), (es - os_) * wgt)


def _hy_fwd_kernel(tab_ref, z_ref, ks_ref, o_ref, mid_ref, *, seq):
    i = pl.program_id(1)
    half = seq // 2
    tf = o_ref.shape[1]
    z_even, z_odd = z_ref[0], z_ref[1]
    ec, oc, es, os_ = _parity_dft(_table_rows(tab_ref, i, tf), z_even, z_odd, half)
    zre_lo, zim_lo, zre_hi, zim_hi = ec + oc, -(es + os_), ec - oc, es - os_
    kre_lo, kim_lo, kre_hi, kim_hi = ks_ref[0], ks_ref[1], ks_ref[2], ks_ref[3]
    yre_lo = zre_lo * kre_lo - zim_lo * kim_lo
    yim_lo = zre_lo * kim_lo + zim_lo * kre_lo
    yre_hi = zre_hi * kre_hi - zim_hi * kim_hi
    yim_hi = zre_hi * kim_hi + zim_hi * kre_hi

    @pl.when(i == 0)
    def _():
        sign = _alternating_sign(0, half)
        mid_re = jnp.sum(z_even.astype(f32) * sign, axis=0, keepdims=True)
        mid_im = -jnp.sum(z_odd.astype(f32) * sign, axis=0, keepdims=True)
        k_re, k_im = ks_ref[1, 0:1, :], ks_ref[3, 0:1, :]
        mid_ref[0:1, :] = mid_re * k_re - mid_im * k_im
        mid_ref[1:2, :] = mid_re * k_im + mid_im * k_re

    @pl.when(i != 0)
    def _():
        mid_ref[...] = jnp.zeros_like(mid_ref)

    packed = (i * tf + lax.broadcasted_iota(jnp.int32, (tf, 1), 0)) == 0
    o_ref[0] = (yre_lo + yre_hi).astype(bf16)
    o_ref[1] = jnp.where(packed, mid_ref[0:1, :], yim_lo - yim_hi).astype(bf16)
    o_ref[2] = (yre_lo - yre_hi).astype(bf16)
    o_ref[3] = jnp.where(packed, mid_ref[1:2, :], yim_lo + yim_hi).astype(bf16)


def _hy_inv_kernel(tab_ref, il_ref, ys_ref, z_ref, x0_ref, x_ref, mod_ref, ng_ref, d_ref, wo_ref, bo_ref, o_ref, *,
                   seq):
    i = pl.program_id(1)
    tm = x_ref.shape[0] // 2
    half = seq // 2
    rows = _table_rows(tab_ref, i, tm)
    sign = _alternating_sign(i * tm, tm)
    conv_even = (jnp.dot(rows[:, :seq], ys_ref[:seq, :], preferred_element_type=f32)
                 + sign * ys_ref[half:half + 1, :].astype(f32))
    conv_odd = (jnp.dot(rows[:, seq:], ys_ref[seq:, :], preferred_element_type=f32)
                - sign * ys_ref[3 * half:3 * half + 1, :].astype(f32))
    conv = jnp.dot(il_ref[...], jnp.concatenate([conv_even, conv_odd], axis=0).astype(bf16),
                   preferred_element_type=f32)
    y = conv + z_ref[...] * d_ref[...]
    out = jnp.dot((y * x0_ref[...]).astype(bf16), wo_ref[...], preferred_element_type=f32) + bo_ref[...]
    o_ref[...] = _residual(x_ref[...], out, ng_ref[1:2, :], mod_ref[2:3, :])


def _hyena_layer(x, trunk, layer, mod, norm_g, p):
    nb, seq = trunk.batch, trunk.seq
    tl = _tile_rows(trunk, 256)
    x8 = x.reshape(nb, seq // 8, 8, D_MODEL)
    prev_spec, next_spec = _halo_specs(trunk, tl)
    x0, z, zb = pl.pallas_call(
        _hy_in_kernel,
        grid=(nb, seq // tl),
        in_specs=[_x_spec(tl), prev_spec, next_spec, _mod_spec(trunk, layer), _ng_spec(layer),
                  _resident((D_MODEL, 3 * D_MODEL)), _resident((1, 3 * D_MODEL)),
                  _resident((3, 3 * D_MODEL)), _resident((1, 3 * D_MODEL)), _resident((tl, tl))],
        out_specs=[_x_spec(tl), _x_spec(tl),
                   pl.BlockSpec((None, 2, tl // 2, D_MODEL), lambda b, j: (b, 0, j, 0))],
        out_shape=[jax.ShapeDtypeStruct(x.shape, f32)] * 2
        + [jax.ShapeDtypeStruct((nb, 2, seq // 2, D_MODEL), bf16)],
        compiler_params=_params("parallel", "parallel"),
    )(x, x8, x8, mod, norm_g, p["w_in"], p["b_in"], p["conv_w"], p["conv_b"],
      jnp.asarray(_parity_permutation(tl), dtype=bf16))

    half = seq // 2
    fwd_table, inv_table = (_mxu_table(t) for t in _hyena_tables(seq))
    table_spec = _resident((half, 2 * seq))
    tn = 256
    nblk = D_MODEL // tn
    ks = pl.pallas_call(
        functools.partial(_hy_filter_kernel, seq=seq),
        grid=(nblk,),
        in_specs=[_resident((seq, HY_EMB_PAD)), _resident((HY_EMB_PAD, HY_FILT)), _resident((1, HY_FILT)),
                  _resident((2, HY_FILT)), _resident((HY_FILT, HY_FILT)), _resident((1, HY_FILT)),
                  pl.BlockSpec((HY_FILT, tn), lambda j: (0, j)),
                  pl.BlockSpec((HY_FILT, tn), lambda j: (0, nblk + j)),
                  pl.BlockSpec((1, tn), lambda j: (0, j)),
                  table_spec],
        out_specs=pl.BlockSpec((4, half, tn), lambda j: (0, 0, j)),
        out_shape=jax.ShapeDtypeStruct((4, half, D_MODEL), f32),
        scratch_shapes=[pltpu.VMEM((seq, HY_FILT), f32)],
        compiler_params=_params("arbitrary"),
    )(jnp.asarray(_hyena_features(seq)), p["f_w1"], p["f_b1"], p["f_freq"], p["f_w2"], p["f_b2"],
      p["f_w3"], p["f_w3"], jnp.asarray(_hyena_decay_rates()), fwd_table)

    tf = min(half, 512)
    mode = pl.Buffered(1) if half // tf > 1 else None
    ys = pl.pallas_call(
        functools.partial(_hy_fwd_kernel, seq=seq),
        grid=(nb, half // tf),
        in_specs=[table_spec,
                  pl.BlockSpec((None, 2, half, D_MODEL), lambda b, i: (b, 0, 0, 0), pipeline_mode=mode),
                  pl.BlockSpec((4, tf, D_MODEL), lambda b, i: (0, i, 0))],
        out_specs=pl.BlockSpec((None, 4, tf, D_MODEL), lambda b, i: (b, 0, i, 0)),
        out_shape=jax.ShapeDtypeStruct((nb, 4, half, D_MODEL), bf16),
        scratch_shapes=[pltpu.VMEM((2, D_MODEL), f32)],
        compiler_params=_params("parallel", "arbitrary"),
    )(fwd_table, zb, ks)

    tm = min(half, 256)
    pair_tile = _x_spec(2 * tm)
    return pl.pallas_call(
        functools.partial(_hy_inv_kernel, seq=seq),
        grid=(nb, half // tm),
        in_specs=[table_spec, _resident((2 * tm, 2 * tm)),
                  pl.BlockSpec((None, 2 * seq, D_MODEL), lambda b, i: (b, 0, 0)),
                  pair_tile, pair_tile, pair_tile, _mod_spec(trunk, layer), _ng_spec(layer),
                  _resident((1, D_MODEL)), _resident((D_MODEL, D_MODEL)), _resident((1, D_MODEL))],
        out_specs=pair_tile,
        out_shape=jax.ShapeDtypeStruct(x.shape, f32),
        compiler_params=_params("parallel", "parallel"),
    )(inv_table, jnp.asarray(_parity_permutation(2 * tm).T, dtype=bf16), ys.reshape(nb, 2 * seq, D_MODEL),
      z, x0, x, mod, norm_g, p["d"], p["w_out"], p["b_out"])


def _head_sum(t, ones_pair):
    tb = t.astype(bf16)
    cols = [jnp.dot(tb[:, p * PAIR_W:(p + 1) * PAIR_W], ones_pair, preferred_element_type=f32)
            for p in range(D_MODEL // PAIR_W)]
    return jnp.concatenate(cols, axis=1)


def _rw_pre_kernel(x_ref, xp_ref, xn_ref, mod_ref, ng_ref, mix_ref, wr_ref, wk_ref, wv_ref, g1_ref, g2_ref,
                   w0_ref, w1_ref, w2_ref, a0_ref, a1_ref, a2_ref, kkp_ref, kap_ref, rkp_ref, ones_ref,
                   r_ref, k_ref, v_ref, kk_ref, g_ref, lwf_ref, lwb_ref, af_ref, ab_ref, bn_ref):
    j = pl.program_id(1)
    g, sh, sc = ng_ref[0:1, :], mod_ref[0:1, :], mod_ref[1:2, :]
    h = _norm_mod(x_ref[...], g, sh, sc)
    prev_row = jnp.where(j == 0, 0.0, _norm_mod(xp_ref[...], g, sh, sc)[7:8, :])
    next_row = jnp.where(j == pl.num_programs(1) - 1, 0.0, _norm_mod(xn_ref[...], g, sh, sc)[0:1, :])
    h_prev, h_next = _shifted(h, prev_row, next_row)
    xx = 0.5 * (h_prev + h_next) - h

    def mixed(m):
        return (h + xx * mix_ref[m:m + 1, :]).astype(bf16)

    r = jnp.dot(mixed(0), wr_ref[...], preferred_element_type=f32)
    xw = mixed(1)
    k = jnp.dot(mixed(2), wk_ref[...], preferred_element_type=f32)
    v = jnp.dot(mixed(3), wv_ref[...], preferred_element_type=f32)
    xa = mixed(4)
    r_ref[...] = r
    k_ref[...] = k
    v_ref[...] = v
    gate = _sigmoid(jnp.dot(mixed(5), g1_ref[...], preferred_element_type=f32))
    g_ref[...] = jnp.dot(gate.astype(bf16), g2_ref[...], preferred_element_type=f32)
    ones = ones_ref[...]
    kk = k * kkp_ref[...]
    kk_ref[...] = kk * lax.rsqrt(jnp.maximum(_head_sum(kk * kk, ones), 1e-24))
    a_sum = None
    for dd, (lw_ref, a_ref) in enumerate(((lwf_ref, af_ref), (lwb_ref, ab_ref))):
        lora = jnp.tanh(jnp.dot(xw, w1_ref[dd], preferred_element_type=f32))
        wl = w0_ref[dd:dd + 1, :] + jnp.dot(lora.astype(bf16), w2_ref[dd], preferred_element_type=f32)
        lw_ref[...] = -math.exp(-0.5) * _sigmoid(wl)
        al = jnp.dot(xa, a1_ref[dd], preferred_element_type=f32)
        a = _sigmoid(a0_ref[dd:dd + 1, :] + jnp.dot(al.astype(bf16), a2_ref[dd], preferred_element_type=f32))
        a_ref[...] = a
        a_sum = a if a_sum is None else a_sum + a
    kd_sum = k * (2.0 + (a_sum - 2.0) * kap_ref[...])
    bn_ref[...] = _head_sum(r * kd_sum * rkp_ref[...], ones) * v


def _stack_heads(x, first_head):
    return jnp.concatenate([jnp.where(first_head, x, 0.0), jnp.where(first_head, 0.0, x)], axis=0).astype(bf16)


def _dot_nt(a, b):
    return lax.dot_general(a, b, (((1,), (1,)), ((), ())), preferred_element_type=f32)


def _dot_tn(a, b):
    return lax.dot_general(a, b, (((0,), (0,)), ((), ())), preferred_element_type=f32)


def _rw_scan_kernel(r_ref, k_ref, v_ref, kk_ref, lw_ref, a_ref, kap_ref, s0_ref, o_ref, s_ref, *, reverse):
    c = pl.program_id(1)
    C, W, G = SCAN_CHUNK, PAIR_W, SCAN_GROUP

    @pl.when(c == 0)
    def _():
        s_ref[...] = s0_ref[...]

    row = lax.broadcasted_iota(jnp.int32, (C, W), 0)
    lane = lax.broadcasted_iota(jnp.int32, (C, W), 1)
    col = lane & (C - 1)
    first_head = lane < RW_HEAD
    before = (col > row) if reverse else (col < row)
    before_eq = before | (col == row)
    eye = (col == row).astype(f32)
    row_w = lax.broadcasted_iota(jnp.int32, (W, W), 0)
    lane_w = lax.broadcasted_iota(jnp.int32, (W, W), 1)
    same_head = (row_w >= RW_HEAD) == (lane_w >= RW_HEAD)
    diag_w = row_w == lane_w
    merge_masks = []
    s = 1
    while s < C:
        sh = s.bit_length() - 1
        same_pair = (row >> (sh + 1)) == (col >> (sh + 1))
        later, earlier = ((row >> sh) & 1, (col >> sh) & 1)
        merge_masks.append(same_pair & (((later == 0) & (earlier == 1)) if reverse
                                        else ((later == 1) & (earlier == 0))))
        s *= 2

    row_c = lax.broadcasted_iota(jnp.int32, (G * C, G * C), 0)
    col_c = lax.broadcasted_iota(jnp.int32, (G * C, G * C), 1)
    chunk_bits = C.bit_length() - 1
    same_chunk = (row_c >> chunk_bits) == (col_c >> chunk_bits)
    tri = (same_chunk & ((col_c >= row_c) if reverse else (col_c <= row_c))).astype(bf16)
    lw_all = lw_ref[...]
    cum_all = _dot_exact_rhs_left(tri, lw_all)
    n_pairs = D_MODEL // W

    def pairs(x):
        return [x[g * C:(g + 1) * C, p * W:(p + 1) * W] for g in range(G) for p in range(n_pairs)]

    def per_chunk_rows(rows):
        return [rows[g][:, p * W:(p + 1) * W] for g in range(G) for p in range(n_pairs)]

    def each(fn, *lists):
        return [fn(*args) for args in zip(*lists)]

    def stack(x):
        return _stack_heads(x, first_head)

    def mm(x, y):
        return jnp.dot(x.astype(bf16), stack(y), preferred_element_type=f32)

    def mm2(x, y0, y1):
        return jnp.dot(x.astype(bf16), jnp.concatenate([stack(y0), stack(y1)], axis=1),
                       preferred_element_type=f32)

    r, k, v, kk, a = (pairs(ref[...]) for ref in (r_ref, k_ref, v_ref, kk_ref, a_ref))
    lw, cum = pairs(lw_all), pairs(cum_all)
    tot = per_chunk_rows([jnp.sum(lw_all[g * C:(g + 1) * C, :], axis=0, keepdims=True) for g in range(G)])
    ka = per_chunk_rows([kap_ref[...]] * G)
    kd = each(lambda k_, a_, ka_: k_ * (1.0 + (a_ - 1.0) * ka_), k, a, ka)
    b = each(lambda kk_, a_: kk_ * a_, kk, a)
    e_in = each(jnp.exp, cum)
    e_out = each(lambda c_: jnp.exp(-c_), cum)
    e_end = each(lambda t_, c_: jnp.exp(t_ - c_), tot, cum)
    kap = each(lambda kk_, c_, l_: kk_ * jnp.exp(c_ - l_), kk, cum, lw)
    mul = lambda x_, y_: x_ * y_
    kt, bt, rt, kh, bh = each(mul, kd, e_out), each(mul, b, e_out), each(mul, r, e_in), each(mul, kd, e_end), \
        each(mul, b, e_end)

    lhs = each(lambda x_, y_: jnp.concatenate([x_, y_], axis=0).astype(bf16), kap, rt)
    gram = each(lambda l_, y0_, y1_: _dot_nt(l_, jnp.concatenate([stack(y0_), stack(y1_)], axis=0)), lhs, kt, bt)
    m_kk = each(lambda g_: jnp.where(before, g_[:C, :W], 0.0), gram)
    n_kb = each(lambda g_: jnp.where(before, g_[:C, W:], 0.0), gram)
    a_qk = each(lambda g_: jnp.where(before_eq, g_[C:, :W], 0.0), gram)
    a_qb = each(lambda g_: jnp.where(before_eq, g_[C:, W:], 0.0), gram)

    tinv = each(lambda n_: eye - jnp.where(merge_masks[0], n_, 0.0), n_kb)
    for off in merge_masks[1:]:
        tn = each(lambda t_, n_: mm(t_, jnp.where(off, n_, 0.0)), tinv, n_kb)
        tinv = each(lambda t_, tn_: t_ - mm(tn_, t_), tinv, tn)

    mva = each(lambda m_, a_, v_: mm(jnp.concatenate([m_, a_], axis=0), v_), m_kk, a_qk, v)
    mv, av = each(lambda t_: t_[:C, :], mva), each(lambda t_: t_[C:, :], mva)
    tk = each(mm2, tinv, kap, mv)
    kt_c, w1 = each(lambda t_: t_[:, :W], tk), each(lambda t_: t_[:, W:], tk)
    qa = each(mm2, a_qb, kt_c, w1)
    q_eff = each(lambda rt_, qa_: rt_ - qa_[:, :W], rt, qa)
    o_intra = each(lambda av_, qa_: av_ - qa_[:, W:], av, qa)
    kb = each(lambda x_, b_: _dot_tn(x_.astype(bf16), b_.astype(bf16)), kt_c, bh)
    phi = each(lambda t_, kb_: jnp.where(same_head, jnp.where(diag_w, jnp.exp(t_), 0.0) - kb_, 0.0), tot, kb)
    ds_full = each(lambda v_, w_, kh_, bh_: _dot_tn(jnp.concatenate([v_, -w_], axis=0).astype(bf16),
                                                    jnp.concatenate([kh_, bh_], axis=0).astype(bf16)),
                   v, w1, kh, bh)
    d_s = each(lambda d_: jnp.where(first_head, d_[:RW_HEAD, :], d_[RW_HEAD:, :]), ds_full)
    state = [s_ref[:, p * W:(p + 1) * W] for p in range(n_pairs)]
    for g in (range(G - 1, -1, -1) if reverse else range(G)):
        sel = slice(g * n_pairs, (g + 1) * n_pairs)
        o = each(lambda q_, s_, oi_: _dot_nt(q_.astype(bf16), stack(s_)) + oi_, q_eff[sel], state, o_intra[sel])
        state = each(lambda s_, ph_, ds_: jnp.dot(s_.astype(bf16), ph_.astype(bf16),
                                                  preferred_element_type=f32) + ds_, state, phi[sel], d_s[sel])
        o_ref[g * C:(g + 1) * C, :] = jnp.concatenate(o, axis=1)
    s_ref[...] = jnp.concatenate(state, axis=1)


def _rw_post_kernel(of_ref, ob_ref, bn_ref, g_ref, x_ref, mod_ref, ng_ref, lw_ref, lb_ref, ones_ref,
                    wo_ref, o_ref):
    o = of_ref[...] + ob_ref[...]
    ones = ones_ref[...]
    inv_n = 1.0 / RW_HEAD
    dev = o - _head_sum(o, ones) * inv_n
    var = _head_sum(dev * dev, ones) * inv_n
    on = dev * lax.rsqrt(var + RW_LN_EPS) * lw_ref[...] + lb_ref[...] + bn_ref[...]
    y = jnp.dot((on * g_ref[...]).astype(bf16), wo_ref[...], preferred_element_type=f32)
    o_ref[...] = _residual(x_ref[...], y, ng_ref[1:2, :], mod_ref[2:3, :])


@functools.lru_cache(maxsize=None)
def _pair_ones():
    idx = np.arange(PAIR_W) // RW_HEAD
    return np.asarray(idx[:, None] == idx[None, :], dtype=bf16)


def _rwkv_layer(x, trunk, layer, mod, norm_g, p, s0_fwd, s0_bwd):
    nb, seq = trunk.batch, trunk.seq
    tl = _tile_rows(trunk, 256)
    x8 = x.reshape(nb, seq // 8, 8, D_MODEL)
    prev_spec, next_spec = _halo_specs(trunk, tl)
    lora_w, lora_a, lora_g = p["w1"].shape[-1], p["a1"].shape[-1], p["g1"].shape[-1]
    ones = jnp.asarray(_pair_ones())
    outs = pl.pallas_call(
        _rw_pre_kernel,
        grid=(nb, seq // tl),
        in_specs=[_x_spec(tl), prev_spec, next_spec, _mod_spec(trunk, layer), _ng_spec(layer),
                  _resident((6, D_MODEL)),
                  _resident((D_MODEL, D_MODEL)), _resident((D_MODEL, D_MODEL)), _resident((D_MODEL, D_MODEL)),
                  _resident((D_MODEL, lora_g)), _resident((lora_g, D_MODEL)),
                  _resident((2, D_MODEL)), _resident((2, D_MODEL, lora_w)), _resident((2, lora_w, D_MODEL)),
                  _resident((2, D_MODEL)), _resident((2, D_MODEL, lora_a)), _resident((2, lora_a, D_MODEL)),
                  _resident((1, D_MODEL)), _resident((1, D_MODEL)), _resident((1, D_MODEL)),
                  _resident((PAIR_W, PAIR_W))],
        out_specs=[_x_spec(tl)] * 10,
        out_shape=[jax.ShapeDtypeStruct(x.shape, f32)] * 10,
        compiler_params=_params("parallel", "parallel"),
    )(x, x8, x8, mod, norm_g, p["mix"], p["wr"], p["wk"], p["wv"], p["g1"], p["g2"],
      p["w0"], p["w1"], p["w2"], p["a0"], p["a1"], p["a2"], p["kk"], p["ka"], p["rk"], ones)
    r, k, v, kk, gate, lw_f, lw_b, a_f, a_b, bonus = outs

    def state_in(s):
        return s.transpose(0, 2, 1, 3).reshape(nb, RW_HEAD, D_MODEL)

    def state_out(s):
        return s.reshape(nb, RW_HEAD, RW_HEADS, RW_HEAD).transpose(0, 2, 1, 3)

    rows = SCAN_CHUNK * SCAN_GROUP
    n_chunks = seq // rows
    state_spec = pl.BlockSpec((None, RW_HEAD, D_MODEL), lambda b, c: (b, 0, 0))
    results = []
    for reverse, lw, a, s0 in ((False, lw_f, a_f, s0_fwd), (True, lw_b, a_b, s0_bwd)):
        if reverse:
            chunk_spec = pl.BlockSpec((None, rows, D_MODEL), lambda b, c: (b, n_chunks - 1 - c, 0))
        else:
            chunk_spec = pl.BlockSpec((None, rows, D_MODEL), lambda b, c: (b, c, 0))
        results.append(pl.pallas_call(
            functools.partial(_rw_scan_kernel, reverse=reverse),
            grid=(nb, n_chunks),
            in_specs=[chunk_spec] * 6 + [_resident((1, D_MODEL)), state_spec],
            out_specs=[chunk_spec, state_spec],
            out_shape=[jax.ShapeDtypeStruct(x.shape, f32),
                       jax.ShapeDtypeStruct((nb, RW_HEAD, D_MODEL), f32)],
            compiler_params=_params("parallel", "arbitrary"),
        )(r, k, v, kk, lw, a, p["ka"], state_in(s0)))
    (o_f, s_f), (o_b, s_b) = results

    y = pl.pallas_call(
        _rw_post_kernel,
        grid=(nb, seq // tl),
        in_specs=[_x_spec(tl)] * 5 + [_mod_spec(trunk, layer), _ng_spec(layer),
                                      _resident((1, D_MODEL)), _resident((1, D_MODEL)),
                                      _resident((PAIR_W, PAIR_W)), _resident((D_MODEL, D_MODEL))],
        out_specs=_x_spec(tl),
        out_shape=jax.ShapeDtypeStruct(x.shape, f32),
        compiler_params=_params("parallel", "parallel"),
    )(o_f, o_b, bonus, gate, x, mod, norm_g, p["lnx_w"], p["lnx_b"], ones, p["wo"])
    return y, state_out(s_f), state_out(s_b)


def kernel(x_prompt, x_sample, state_wkv_fwd, state_wkv_bwd, c, c_ctx, ada_w, ada_b, norm_g, ffn_w1, ffn_w3, ffn_w2, fn_w_out, fn_b_out, hy_w_in, hy_b_in, hy_conv_w, hy_conv_b, hy_f_w1, hy_f_b1, hy_f_freq, hy_f_w2, hy_f_b2, hy_f_w3, hy_d, hy_w_out, hy_b_out, rw_mix, rw_wr, rw_wk, rw_wv, rw_wo, rw_w0, rw_w1, rw_w2, rw_a0, rw_a1, rw_a2, rw_g1, rw_g2, rw_kk, rw_ka, rw_rk, rw_lnx_w, rw_lnx_b):
    n_ctx, n_dec = x_prompt.shape[0], x_sample.shape[0]
    assert 1 + n_dec <= MOD_ROWS
    trunks = (Trunk(n_ctx, x_prompt.shape[1], 0, False), Trunk(n_dec, x_sample.shape[1], 1, True))
    cond = jnp.zeros((MOD_ROWS, D_MODEL), f32).at[0].set(c_ctx).at[1:1 + n_dec].set(c)
    mod = _adaln(cond, ada_w, ada_b)

    n_rwkv = rw_w0.shape[0]
    zero_state = jnp.zeros((n_ctx, n_rwkv, RW_HEADS, RW_HEAD, RW_HEAD), f32)
    states = ((zero_state, zero_state), (state_wkv_fwd, state_wkv_bwd))
    xs = [x_prompt, x_sample]
    ffn_bf16 = [w.astype(bf16) for w in (ffn_w1, ffn_w3, ffn_w2)]
    new_fwd, new_bwd = [], []
    for i in range(DEPTH):
        kind, j = i % N_MIXERS, i // N_MIXERS
        if kind == 0:
            w_out, b_out = fn_w_out[j].astype(bf16), fn_b_out[j][None]
            xs = [_fourier_layer(x, t, i, mod, norm_g, w_out, b_out) for x, t in zip(xs, trunks)]
        elif kind == 1:
            w1_pad = jnp.zeros((HY_EMB_PAD, HY_FILT), f32).at[:HY_EMB].set(hy_f_w1[j])
            p = dict(w_in=hy_w_in[j].astype(bf16), b_in=hy_b_in[j][None], conv_w=hy_conv_w[j],
                     conv_b=hy_conv_b[j][None], f_w1=w1_pad, f_b1=hy_f_b1[j][None], f_freq=hy_f_freq[j],
                     f_w2=hy_f_w2[j], f_b2=hy_f_b2[j][None], f_w3=hy_f_w3[j], d=hy_d[j][None],
                     w_out=hy_w_out[j].astype(bf16), b_out=hy_b_out[j][None])
            xs = [_hyena_layer(x, t, i, mod, norm_g, p) for x, t in zip(xs, trunks)]
        else:
            per_head = lambda t: t.reshape(1, D_MODEL)
            p = dict(mix=rw_mix[j], wr=rw_wr[j].astype(bf16), wk=rw_wk[j].astype(bf16),
                     wv=rw_wv[j].astype(bf16), wo=rw_wo[j].astype(bf16), w0=rw_w0[j],
                     w1=rw_w1[j].astype(bf16), w2=rw_w2[j].astype(bf16), a0=rw_a0[j],
                     a1=rw_a1[j].astype(bf16), a2=rw_a2[j].astype(bf16), g1=rw_g1[j].astype(bf16),
                     g2=rw_g2[j].astype(bf16), kk=per_head(rw_kk[j]), ka=per_head(rw_ka[j]),
                     rk=per_head(rw_rk[j]), lnx_w=rw_lnx_w[j][None], lnx_b=rw_lnx_b[j][None])
            outs = [_rwkv_layer(x, t, i, mod, norm_g, p, sf[:, j], sb[:, j])
                    for x, t, (sf, sb) in zip(xs, trunks, states)]
            xs = [o[0] for o in outs]
            new_fwd.append(outs[0][1])
            new_bwd.append(outs[0][2])
        xs = [_ffn(x, t, i, mod, norm_g, ffn_bf16[0], ffn_bf16[1], ffn_bf16[2]) for x, t in zip(xs, trunks)]
    return xs[0], xs[1], jnp.stack(new_fwd, axis=1), jnp.stack(new_bwd, axis=1)
```
